```python
import math
import jax, jax.numpy as jnp
from jax import lax
import numpy as np

D_MODEL = 1024
BATCH = 4
SEQ = 4096
DEPTH = 2

N_A = DEPTH // 2
N_B = DEPTH - N_A

GLA_HEADS = 4
GLA_DK = (D_MODEL // 2) // GLA_HEADS
GLA_DV = D_MODEL // GLA_HEADS
GLA_GATE_RANK = 16
GLA_TAU = 16.0
GLA_CHUNK = 64
GLA_COLS = (GLA_HEADS * GLA_DK, GLA_HEADS * GLA_DK, GLA_HEADS * GLA_DV, GLA_HEADS * GLA_DV, GLA_GATE_RANK)
GLA_SPLITS = list(np.cumsum(GLA_COLS)[:-1].tolist())
GLA_IN = int(sum(GLA_COLS))

DIFF_HEADS = 8
DIFF_DH = 64
DIFF_VD = 2 * DIFF_DH
Q_BLOCK = 128

MLP_HIDDEN = 4 * D_MODEL
NORM_EPS = 1e-6

kernel_name = "yoco_gla_diffattn_hybrid"


def rms_norm(x, g):
    xf = x.astype(jnp.float32)
    y = xf * lax.rsqrt(jnp.mean(xf * xf, axis=-1, keepdims=True) + NORM_EPS)
    return (y * g.astype(jnp.float32)).astype(x.dtype)


def sqrelu_mlp(xn, w1, w2):
    return jnp.square(jax.nn.relu(xn @ w1)) @ w2


def gla_chunked(q, k, v, log_a):
    B, S, H, DK = q.shape
    DV = v.shape[-1]
    C = GLA_CHUNK
    n = S // C

    def chunks(t):
        return t.astype(jnp.float32).reshape(B, n, C, H, t.shape[-1]).transpose(1, 0, 3, 2, 4)

    qc = chunks(q) * (DK ** -0.5)
    kc = chunks(k)
    vc = chunks(v)
    bc = jnp.cumsum(chunks(log_a), axis=-2)
    causal = jnp.tril(jnp.ones((C, C), bool))

    def step(state, inp):
        qi, ki, vi, bi = inp
        diff = bi[:, :, :, None, :] - bi[:, :, None, :, :]
        decay = jnp.exp(jnp.where(causal[:, :, None], diff, -jnp.inf))
        attn = jnp.einsum('bhid,bhjd,bhijd->bhij', qi, ki, decay)
        o = (jnp.einsum('bhij,bhjv->bhiv', attn, vi)
             + jnp.einsum('bhid,bhdv->bhiv', qi * jnp.exp(bi), state))
        b_last = bi[:, :, -1:, :]
        state = (jnp.exp(b_last[:, :, 0, :, None]) * state
                 + jnp.einsum('bhjd,bhjv->bhdv', ki * jnp.exp(b_last - bi), vi))
        return state, o

    _, o = lax.scan(step, jnp.zeros((B, H, DK, DV), jnp.float32), (qc, kc, vc, bc))
    return o.transpose(1, 0, 3, 2, 4).reshape(B, S, H, DV)


def gla_mixer(xn, w_in, w_gate_up, b_gate, head_gain, w_out):
    B, S, _ = xn.shape
    proj = xn @ w_in
    q, k, v, g, z = jnp.split(proj, GLA_SPLITS, axis=-1)
    log_a = jax.nn.log_sigmoid((z @ w_gate_up + b_gate).astype(jnp.float32)) / GLA_TAU
    hk = lambda t: t.reshape(B, S, GLA_HEADS, GLA_DK)
    o = gla_chunked(hk(q), hk(k), v.reshape(B, S, GLA_HEADS, GLA_DV), hk(log_a)).astype(xn.dtype)
    o = rms_norm(o, head_gain) * jax.nn.silu(g.reshape(B, S, GLA_HEADS, GLA_DV))
    return o.reshape(B, S, GLA_HEADS * GLA_DV) @ w_out


def shared_kv(h, kv_norm, w_kv, k_norm):
    B, S, _ = h.shape
    kv = rms_norm(h, kv_norm) @ w_kv
    k, v = jnp.split(kv, [DIFF_HEADS * 2 * DIFF_DH], axis=-1)
    k = rms_norm(k.reshape(B, S, DIFF_HEADS, 2, DIFF_DH), k_norm).transpose(0, 2, 3, 1, 4)
    v = v.reshape(B, S, DIFF_HEADS, DIFF_VD).transpose(0, 2, 1, 3)
    return k, v


def diff_mixer(xn, k, v, w_q, q_gain, lam_params, head_gain, w_out, lambda_init):
    B, S, _ = xn.shape
    q = rms_norm((xn @ w_q).reshape(B, S, DIFF_HEADS, 2, DIFF_DH), q_gain) * (DIFF_DH ** -0.5)
    lp = lam_params.astype(jnp.float32)
    lam = jnp.exp(jnp.sum(lp[0] * lp[1])) - jnp.exp(jnp.sum(lp[2] * lp[3])) + lambda_init
    nq = S // Q_BLOCK
    qb = q.transpose(0, 2, 3, 1, 4).reshape(B, DIFF_HEADS, 2, nq, Q_BLOCK, DIFF_DH).transpose(3, 0, 1, 2, 4, 5)
    kpos = jnp.arange(S)

    def attend(args):
        q_blk, bi = args
        s = jnp.einsum('bhcqd,bhckd->bhcqk', q_blk, k).astype(jnp.float32)
        qpos = bi * Q_BLOCK + jnp.arange(Q_BLOCK)
        s = jnp.where(kpos[None, :] <= qpos[:, None], s, -jnp.inf)
        p = jax.nn.softmax(s, axis=-1)
        a = p[:, :, 0] - lam * p[:, :, 1]
        return jnp.einsum('bhqk,bhkv->bhqv', a.astype(v.dtype), v)

    o = lax.map(attend, (qb, jnp.arange(nq)))
    o = o.transpose(1, 0, 3, 2, 4).reshape(B, S, DIFF_HEADS, DIFF_VD)
    o = rms_norm(o, head_gain) * (1.0 - lambda_init)
    return o.reshape(B, S, DIFF_HEADS * DIFF_VD) @ w_out


def setup_inputs(seed: int = 0) -> dict:
    key = jax.random.key(seed)
    ks = jax.random.split(key, 24)
    f32 = jnp.float32
    nrm = lambda k, shape, fan_in: jax.random.normal(k, shape, f32) * (fan_in ** -0.5)
    gain = lambda k, shape: 1.0 + 0.02 * jax.random.normal(k, shape, f32)
    D = D_MODEL
    return {
        "x": jax.random.normal(ks[0], (BATCH, SEQ, D), f32),
        "a_norm": gain(ks[1], (N_A, D)),
        "a_w_in": nrm(ks[2], (N_A, D, GLA_IN), D),
        "a_w_gate_up": nrm(ks[3], (N_A, GLA_GATE_RANK, GLA_HEADS * GLA_DK), GLA_GATE_RANK),
        "a_b_gate": 0.1 * jax.random.normal(ks[4], (N_A, GLA_HEADS * GLA_DK), f32),
        "a_head_norm": gain(ks[5], (N_A, GLA_DV)),
        "a_w_out": nrm(ks[6], (N_A, GLA_HEADS * GLA_DV, D), GLA_HEADS * GLA_DV),
        "kv_norm": gain(ks[7], (D,)),
        "w_kv": nrm(ks[8], (D, DIFF_HEADS * (2 * DIFF_DH + DIFF_VD)), D),
        "k_norm": gain(ks[9], (DIFF_DH,)),
        "b_norm": gain(ks[10], (N_B, D)),
        "b_w_q": nrm(ks[11], (N_B, D, DIFF_HEADS * 2 * DIFF_DH), D),
        "b_q_norm": gain(ks[12], (N_B, DIFF_DH)),
        "b_lambda": 0.1 * jax.random.normal(ks[13], (N_B, 4, DIFF_DH), f32),
        "b_head_norm": gain(ks[14], (N_B, DIFF_VD)),
        "b_w_out": nrm(ks[15], (N_B, DIFF_HEADS * DIFF_VD, D), DIFF_HEADS * DIFF_VD),
        "mlp_norm": gain(ks[16], (DEPTH, D)),
        "mlp_w1": nrm(ks[17], (DEPTH, D, MLP_HIDDEN), D),
        "mlp_w2": nrm(ks[18], (DEPTH, MLP_HIDDEN, D), MLP_HIDDEN),
    }


def reference(x, a_norm, a_w_in, a_w_gate_up, a_b_gate, a_head_norm, a_w_out,
              kv_norm, w_kv, k_norm, b_norm, b_w_q, b_q_norm, b_lambda, b_head_norm, b_w_out,
              mlp_norm, mlp_w1, mlp_w2):
    h = x
    k_sh, v_sh = None, None
    for layer in range(DEPTH):
        if layer < N_A:
            i = layer
            h = h + gla_mixer(rms_norm(h, a_norm[i]), a_w_in[i], a_w_gate_up[i], a_b_gate[i],
                              a_head_norm[i], a_w_out[i])
        else:
            if layer == N_A:
                k_sh, v_sh = shared_kv(h, kv_norm, w_kv, k_norm)
            j = layer - N_A
            lambda_init = 0.8 - 0.6 * math.exp(-0.3 * layer)
            h = h + diff_mixer(rms_norm(h, b_norm[j]), k_sh, v_sh, b_w_q[j], b_q_norm[j],
                               b_lambda[j], b_head_norm[j], b_w_out[j], lambda_init)
        h = h + sqrelu_mlp(rms_norm(h, mlp_norm[layer]), mlp_w1[layer], mlp_w2[layer])
    return h
```

```python
import functools
import math

import numpy as np
import jax
import jax.numpy as jnp
from jax import lax
from jax.experimental import pallas as pl
from jax.experimental.pallas import tpu as pltpu

D_MODEL = 1024
GLA_HEADS = 4
GLA_DK = 128
GLA_DV = 256
GLA_RANK = 16
GLA_TAU = 16.0
GLA_CHUNK = 64
DIFF_HEADS = 8
DIFF_DH = 64
DIFF_VD = 128
MLP_HIDDEN = 4 * D_MODEL
NORM_EPS = 1e-6

LANES = 128
VMEM_LIMIT = 56 * 1024 * 1024
NEG_BIG = -1e30

ROW_TILE = 512
GLA_ROWS = 512
ATT_TQ = 512
ATT_TK = 512
GLA_LEVELS = (32, 16, 8, 4, 2, 1)

_NT = (((1,), (1,)), ((), ()))
_TN = (((0,), (0,)), ((), ()))


def _dot(a, b, dims=None):
    if dims is None:
        return jnp.dot(a, b, preferred_element_type=jnp.float32)
    return lax.dot_general(a, b, dims, preferred_element_type=jnp.float32)


def _params(*sem):
    return pltpu.CompilerParams(dimension_semantics=sem, vmem_limit_bytes=VMEM_LIMIT)


def _resident(shape):
    nd = len(shape)
    return pl.BlockSpec(shape, lambda *_: (0,) * nd, pipeline_mode=pl.Buffered(1))


def _rms_scale(x):
    return lax.rsqrt(jnp.mean(x * x, axis=-1, keepdims=True) + NORM_EPS)


def _gla_inproj_kernel(x_ref, g_ref, w_ref, wz_ref, o_ref, z_ref):
    x = x_ref[...]
    xn = (x * _rms_scale(x) * g_ref[...]).astype(jnp.bfloat16)
    n_out = o_ref.shape[1]
    step = 768
    for n0 in range(0, n_out, step):
        o_ref[:, n0:n0 + step] = _dot(xn, w_ref[:, n0:n0 + step]).astype(o_ref.dtype)
    z_ref[...] = _dot(xn, wz_ref[...])


def _gla_inproj(x, gain, w_main, w_z):
    t, d = x.shape
    n = w_main.shape[1]
    return pl.pallas_call(
        _gla_inproj_kernel,
        grid=(t // ROW_TILE,),
        in_specs=[
            pl.BlockSpec((ROW_TILE, d), lambda i: (i, 0)),
            _resident((1, d)),
            _resident((d, n)),
            _resident((d, LANES)),
        ],
        out_specs=[
            pl.BlockSpec((ROW_TILE, n), lambda i: (i, 0)),
            pl.BlockSpec((ROW_TILE, LANES), lambda i: (i, 0)),
        ],
        out_shape=[
            jax.ShapeDtypeStruct((t, n), jnp.bfloat16),
            jax.ShapeDtypeStruct((t, LANES), jnp.float32),
        ],
        compiler_params=_params("parallel"),
        name="gla_inproj",
    )(x, gain, w_main, w_z)


def _gla_constants():
    c = GLA_CHUNK
    t = np.arange(c)
    row, col = t[:, None], t[None, :]
    mats = [col <= row]
    masks = []
    for m in GLA_LEVELS:
        blk = 2 * m
        start = (t // blk) * blk
        mid = (start + m)[:, None]
        second = ((t % blk) >= m)[:, None]
        q_side = second & (col > mid) & (col <= row)
        k_side = (~second) & (col > row) & (col <= mid)
        mats.append(q_side | k_side)
        masks.append((start[:, None] == start[None, :]) & second & (~second).T)
    masks.append(np.eye(c, dtype=bool))
    lm = np.concatenate(mats, axis=0).astype(np.float32)
    mk = np.stack(masks, axis=0).astype(np.float32)
    return lm, mk


def _gla_core_kernel(q_ref, k_ref, v_ref, g_ref, z_ref, wg_ref, bg_ref, hg_ref,
                     lm_ref, mk_ref, o_ref, la_scr, st_scr):
    c = GLA_CHUNK
    rows = q_ref.shape[0]

    @pl.when(pl.program_id(2) == 0)
    def _():
        st_scr[...] = jnp.zeros_like(st_scr)

    logits = _dot(z_ref[...].astype(jnp.bfloat16), wg_ref[...]) + bg_ref[...]
    log_sig = jnp.minimum(logits, 0.0) - jnp.log1p(jnp.exp(-jnp.abs(logits)))
    la_scr[...] = log_sig * (1.0 / GLA_TAU)

    scale = GLA_DK ** -0.5
    gain = hg_ref[...]

    def chunk(ci, carry):
        r0 = pl.multiple_of(ci * c, c)
        rs = pl.ds(r0, c)
        la = la_scr[rs, :]
        la_hi = la.astype(jnp.bfloat16)
        la_lo = (la - la_hi.astype(jnp.float32)).astype(jnp.bfloat16)
        sums = _dot(lm_ref[...], la_hi) + _dot(lm_ref[...], la_lo)
        b = sums[0:c]
        q = q_ref[rs, :].astype(jnp.float32)
        k = k_ref[rs, :].astype(jnp.float32)
        v = v_ref[rs, :]

        attn = mk_ref[len(GLA_LEVELS)] * _dot(q_ref[rs, :], k_ref[rs, :], _NT)
        for li in range(len(GLA_LEVELS)):
            e = jnp.exp(sums[(li + 1) * c:(li + 2) * c])
            ql = (q * e).astype(jnp.bfloat16)
            kl = (k * e).astype(jnp.bfloat16)
            attn = attn + mk_ref[li] * _dot(ql, kl, _NT)

        st = st_scr[...]
        qd = (q * jnp.exp(b)).astype(jnp.bfloat16)
        o = _dot(attn.astype(jnp.bfloat16), v) + _dot(qd, st.astype(jnp.bfloat16), _NT)
        o = o * scale

        b_last = b[c - 1:c, :]
        kd = (k * jnp.exp(b_last - b)).astype(jnp.bfloat16)
        st_scr[...] = st * jnp.exp(b_last) + _dot(v, kd, _TN)

        on = o * _rms_scale(o) * gain
        gate = g_ref[rs, :].astype(jnp.float32)
        o_ref[rs, :] = (on * (gate * jax.nn.sigmoid(gate))).astype(o_ref.dtype)
        return carry

    lax.fori_loop(0, rows // c, chunk, 0)


def _gla_core(qkvg, z, w_gate, b_gate, head_gain, batch, seq):
    t = qkvg.shape[0]
    nblk = seq // GLA_ROWS
    lm, mk = _gla_constants()
    lm = jnp.asarray(lm, jnp.bfloat16)
    mk = jnp.asarray(mk, jnp.float32)
    kq = (GLA_HEADS * GLA_DK) // GLA_DK
    kv = (2 * GLA_HEADS * GLA_DK) // GLA_DV
    kg = kv + GLA_HEADS
    row = lambda b, h, s: b * nblk + s
    return pl.pallas_call(
        _gla_core_kernel,
        grid=(batch, GLA_HEADS, nblk),
        in_specs=[
            pl.BlockSpec((GLA_ROWS, GLA_DK), lambda b, h, s: (row(b, h, s), h)),
            pl.BlockSpec((GLA_ROWS, GLA_DK), lambda b, h, s: (row(b, h, s), kq + h)),
            pl.BlockSpec((GLA_ROWS, GLA_DV), lambda b, h, s: (row(b, h, s), kv + h)),
            pl.BlockSpec((GLA_ROWS, GLA_DV), lambda b, h, s: (row(b, h, s), kg + h)),
            pl.BlockSpec((GLA_ROWS, LANES), lambda b, h, s: (row(b, h, s), 0)),
            pl.BlockSpec((LANES, GLA_DK), lambda b, h, s: (0, h)),
            pl.BlockSpec((1, GLA_DK), lambda b, h, s: (0, h)),
            pl.BlockSpec((1, GLA_DV), lambda b, h, s: (0, 0)),
            pl.BlockSpec(lm.shape, lambda b, h, s: (0, 0)),
            pl.BlockSpec(mk.shape, lambda b, h, s: (0, 0, 0)),
        ],
        out_specs=pl.BlockSpec((GLA_ROWS, GLA_DV), lambda b, h, s: (row(b, h, s), h)),
        out_shape=jax.ShapeDtypeStruct((t, GLA_HEADS * GLA_DV), jnp.bfloat16),
        scratch_shapes=[
            pltpu.VMEM((GLA_ROWS, GLA_DK), jnp.float32),
            pltpu.VMEM((GLA_DV, GLA_DK), jnp.float32),
        ],
        compiler_params=_params("parallel", "parallel", "arbitrary"),
        name="gla_core",
    )(qkvg, qkvg, qkvg, qkvg, z, w_gate, b_gate, head_gain, lm, mk)


def _outproj_mlp_kernel(a_ref, h_ref, wo_ref, g_ref, w1_ref, w2_ref, o_ref):
    h1 = h_ref[...] + _dot(a_ref[...], wo_ref[...])
    xn = (h1 * _rms_scale(h1) * g_ref[...]).astype(jnp.bfloat16)
    acc = h1
    d = h1.shape[1]
    for c0 in range(0, w1_ref.shape[1], d):
        hid = jnp.maximum(_dot(xn, w1_ref[:, c0:c0 + d]), 0.0)
        acc = acc + _dot((hid * hid).astype(jnp.bfloat16), w2_ref[c0:c0 + d, :])
    o_ref[...] = acc


def _outproj_mlp(a, h, w_out, gain, w1, w2):
    t, d = h.shape
    tile = pl.BlockSpec((ROW_TILE, d), lambda i: (i, 0))
    return pl.pallas_call(
        _outproj_mlp_kernel,
        grid=(t // ROW_TILE,),
        in_specs=[tile, tile, _resident(w_out.shape), _resident((1, d)),
                  _resident(w1.shape), _resident(w2.shape)],
        out_specs=tile,
        out_shape=jax.ShapeDtypeStruct((t, d), jnp.float32),
        compiler_params=_params("parallel"),
        name="outproj_mlp",
    )(a, h, w_out, gain, w1, w2)


def _segment_norm(y, seg_ones, gain):
    outs = []
    for c0 in range(0, y.shape[1], LANES):
        yc = y[:, c0:c0 + LANES]
        ss = _dot((yc * yc).astype(jnp.bfloat16), seg_ones)
        outs.append(yc * lax.rsqrt(ss * (1.0 / DIFF_DH) + NORM_EPS) * gain)
    return outs


def _diff_proj_kernel(h_ref, gkv_ref, gq_ref, wkv_ref, wq_ref, kg_ref, qg_ref, so_ref,
                      k_ref, v_ref, q_ref):
    x = h_ref[...]
    xhat = x * _rms_scale(x)
    xkv = (xhat * gkv_ref[...]).astype(jnp.bfloat16)
    xq = (xhat * gq_ref[...]).astype(jnp.bfloat16)
    nk = k_ref.shape[1]
    seg_ones = so_ref[...]
    kraw = _dot(xkv, wkv_ref[:, :nk])
    for i, blk in enumerate(_segment_norm(kraw, seg_ones, kg_ref[...])):
        k_ref[:, i * LANES:(i + 1) * LANES] = blk.astype(k_ref.dtype)
    v_ref[...] = _dot(xkv, wkv_ref[:, nk:]).astype(v_ref.dtype)
    qraw = _dot(xq, wq_ref[...])
    qscale = DIFF_DH ** -0.5
    for i, blk in enumerate(_segment_norm(qraw, seg_ones, qg_ref[...])):
        q_ref[:, i * LANES:(i + 1) * LANES] = (blk * qscale).astype(q_ref.dtype)


def _diff_proj(h, g_kv, g_q, w_kv, w_q, k_gain, q_gain):
    t, d = h.shape
    nk = DIFF_HEADS * 2 * DIFF_DH
    nv = DIFF_HEADS * DIFF_VD
    lane = np.arange(LANES)
    seg_ones = jnp.asarray(lane[:, None] // DIFF_DH == lane[None, :] // DIFF_DH, jnp.bfloat16)
    tile = lambda n: pl.BlockSpec((ROW_TILE, n), lambda i: (i, 0))
    return pl.pallas_call(
        _diff_proj_kernel,
        grid=(t // ROW_TILE,),
        in_specs=[tile(d), _resident((1, d)), _resident((1, d)), _resident(w_kv.shape),
                  _resident(w_q.shape), _resident((1, LANES)), _resident((1, LANES)),
                  _resident((LANES, LANES))],
        out_specs=[tile(nk), tile(nv), tile(nk)],
        out_shape=[jax.ShapeDtypeStruct((t, nk), jnp.bfloat16),
                   jax.ShapeDtypeStruct((t, nv), jnp.bfloat16),
                   jax.ShapeDtypeStruct((t, nk), jnp.bfloat16)],
        compiler_params=_params("parallel"),
        name="diff_proj",
    )(h, g_kv, g_q, w_kv, w_q, k_gain, q_gain, seg_ones)


def _diff_attn_kernel(q_ref, k_ref, v_ref, lam_ref, hg_ref, o_ref,
                      m_scr, l_scr, acc_scr, *, lambda_init):
    qi = pl.program_id(2)
    tq = q_ref.shape[0]
    tk = ATT_TK

    m_scr[...] = jnp.full_like(m_scr, NEG_BIG)
    l_scr[...] = jnp.zeros_like(l_scr)
    acc_scr[...] = jnp.zeros_like(acc_scr)

    q = q_ref[...]
    lane = lax.broadcasted_iota(jnp.int32, q.shape, 1)
    qcs = [jnp.where((lane // DIFF_DH) == c, q, jnp.zeros_like(q)) for c in range(2)]

    def step(j, masked):
        rs = pl.ds(pl.multiple_of(j * tk, tk), tk)
        k = k_ref[rs, :]
        v = v_ref[rs, :]
        if masked:
            rowi = lax.broadcasted_iota(jnp.int32, (tq, tk), 0)
            coli = lax.broadcasted_iota(jnp.int32, (tq, tk), 1)
            keep = coli <= rowi
        for c in range(2):
            s = _dot(qcs[c], k, _NT)
            if masked:
                s = jnp.where(keep, s, NEG_BIG)
            m_prev = m_scr[c]
            m_new = jnp.maximum(m_prev, jnp.max(s, axis=-1, keepdims=True))
            alpha = jnp.exp(m_prev - m_new)
            p = jnp.exp(s - m_new)
            l_scr[c] = alpha * l_scr[c] + jnp.sum(p, axis=-1, keepdims=True)
            acc_scr[c] = alpha * acc_scr[c] + _dot(p.astype(jnp.bfloat16), v)
            m_scr[c] = m_new

    def body(j, carry):
        step(j, False)
        return carry

    lax.fori_loop(0, qi, body, 0)
    step(qi, True)

    lp = lam_ref[...]
    lam = (jnp.exp(jnp.sum(lp[0:1] * lp[1:2], axis=-1, keepdims=True))
           - jnp.exp(jnp.sum(lp[2:3] * lp[3:4], axis=-1, keepdims=True)) + lambda_init)
    o = acc_scr[0] / l_scr[0] - lam * (acc_scr[1] / l_scr[1])
    on = o * _rms_scale(o) * hg_ref[...]
    o_ref[...] = (on * (1.0 - lambda_init)).astype(o_ref.dtype)


def _diff_attn(q, k, v, lam_params, head_gain, batch, seq, lambda_init):
    t = q.shape[0]
    assert ATT_TQ == ATT_TK
    nq = seq // ATT_TQ
    kern = functools.partial(_diff_attn_kernel, lambda_init=lambda_init)
    return pl.pallas_call(
        kern,
        grid=(batch, DIFF_HEADS, nq),
        in_specs=[
            pl.BlockSpec((ATT_TQ, 2 * DIFF_DH), lambda b, h, i: (b * nq + i, h)),
            pl.BlockSpec((seq, 2 * DIFF_DH), lambda b, h, i: (b, h)),
            pl.BlockSpec((seq, DIFF_VD), lambda b, h, i: (b, h)),
            pl.BlockSpec(lam_params.shape, lambda b, h, i: (0, 0)),
            pl.BlockSpec((1, DIFF_VD), lambda b, h, i: (0, 0)),
        ],
        out_specs=pl.BlockSpec((ATT_TQ, DIFF_VD), lambda b, h, i: (b * nq + i, h)),
        out_shape=jax.ShapeDtypeStruct((t, DIFF_HEADS * DIFF_VD), jnp.bfloat16),
        scratch_shapes=[
            pltpu.VMEM((2, ATT_TQ, 1), jnp.float32),
            pltpu.VMEM((2, ATT_TQ, 1), jnp.float32),
            pltpu.VMEM((2, ATT_TQ, DIFF_VD), jnp.float32),
        ],
        compiler_params=_params("parallel", "parallel", "arbitrary"),
        name="diff_attn",
    )(q, k, v, lam_params, head_gain)


def kernel(x, a_norm, a_w_in, a_w_gate_up, a_b_gate, a_head_norm, a_w_out, kv_norm, w_kv, k_norm,
           b_norm, b_w_q, b_q_norm, b_lambda, b_head_norm, b_w_out, mlp_norm, mlp_w1, mlp_w2):
    batch, seq, d = x.shape
    bf = jnp.bfloat16
    row = lambda p: p.reshape(1, -1)
    h = x.reshape(batch * seq, d)

    n_main = 2 * GLA_HEADS * GLA_DK + 2 * GLA_HEADS * GLA_DV
    w_in = a_w_in[0]
    w_main = w_in[:, :n_main].astype(bf)
    w_z = jnp.pad(w_in[:, n_main:], ((0, 0), (0, LANES - GLA_RANK))).astype(bf)
    w_gate = jnp.pad(a_w_gate_up[0], ((0, LANES - GLA_RANK), (0, 0))).astype(bf)
    qkvg, z = _gla_inproj(h, row(a_norm[0]), w_main, w_z)
    og = _gla_core(qkvg, z, w_gate, row(a_b_gate[0]), row(a_head_norm[0]), batch, seq)
    h = _outproj_mlp(og, h, a_w_out[0].astype(bf), row(mlp_norm[0]),
                     mlp_w1[0].astype(bf), mlp_w2[0].astype(bf))

    layer = 1
    lambda_init = 0.8 - 0.6 * math.exp(-0.3 * layer)
    reps = LANES // DIFF_DH
    kn, v, qn = _diff_proj(h, row(kv_norm), row(b_norm[0]), w_kv.astype(bf), b_w_q[0].astype(bf),
                           row(jnp.tile(k_norm, reps)), row(jnp.tile(b_q_norm[0], reps)))
    oa = _diff_attn(qn, kn, v, b_lambda[0], row(b_head_norm[0]), batch, seq, lambda_init)
    h = _outproj_mlp(oa, h, b_w_out[0].astype(bf), row(mlp_norm[1]),
                     mlp_w1[1].astype(bf), mlp_w2[1].astype(bf))
    return h.reshape(batch, seq, d)
```

```python
import functools
import math

import numpy as np
import jax
import jax.numpy as jnp
from jax import lax
from jax.experimental import pallas as pl
from jax.experimental.pallas import tpu as pltpu

D_MODEL = 1024
GLA_HEADS = 4
GLA_DK = 128
GLA_DV = 256
GLA_RANK = 16
GLA_TAU = 16.0
GLA_CHUNK = 64
DIFF_HEADS = 8
DIFF_DH = 64
DIFF_VD = 128
MLP_HIDDEN = 4 * D_MODEL
NORM_EPS = 1e-6

LANES = 128
VMEM_LIMIT = 56 * 1024 * 1024
NEG_BIG = -1e30

ROW_TILE = 512
GLA_ROWS = 512
ATT_TQ = 512
ATT_TK = 512
GLA_LEVELS = (32, 16, 8, 4, 2, 1)

_NT = (((1,), (1,)), ((), ()))
_TN = (((0,), (0,)), ((), ()))


def _dot(a, b, dims=None):
    if dims is None:
        return jnp.dot(a, b, preferred_element_type=jnp.float32)
    return lax.dot_general(a, b, dims, preferred_element_type=jnp.float32)


def _params(*sem):
    return pltpu.CompilerParams(dimension_semantics=sem, vmem_limit_bytes=VMEM_LIMIT)


def _resident(shape):
    nd = len(shape)
    return pl.BlockSpec(shape, lambda *_: (0,) * nd, pipeline_mode=pl.Buffered(1))


def _rms_scale(x):
    return lax.rsqrt(jnp.mean(x * x, axis=-1, keepdims=True) + NORM_EPS)


def _gla_inproj_kernel(x_ref, g_ref, w_ref, wz_ref, o_ref, z_ref):
    x = x_ref[...]
    xn = (x * _rms_scale(x) * g_ref[...]).astype(jnp.bfloat16)
    n_out = o_ref.shape[1]
    step = 768
    for n0 in range(0, n_out, step):
        o_ref[:, n0:n0 + step] = _dot(xn, w_ref[:, n0:n0 + step]).astype(o_ref.dtype)
    z_ref[...] = _dot(xn, wz_ref[...])


def _gla_inproj(x, gain, w_main, w_z):
    t, d = x.shape
    n = w_main.shape[1]
    return pl.pallas_call(
        _gla_inproj_kernel,
        grid=(t // ROW_TILE,),
        in_specs=[
            pl.BlockSpec((ROW_TILE, d), lambda i: (i, 0)),
            _resident((1, d)),
            _resident((d, n)),
            _resident((d, LANES)),
        ],
        out_specs=[
            pl.BlockSpec((ROW_TILE, n), lambda i: (i, 0)),
            pl.BlockSpec((ROW_TILE, LANES), lambda i: (i, 0)),
        ],
        out_shape=[
            jax.ShapeDtypeStruct((t, n), jnp.bfloat16),
            jax.ShapeDtypeStruct((t, LANES), jnp.float32),
        ],
        compiler_params=_params("parallel"),
        name="gla_inproj",
    )(x, gain, w_main, w_z)


def _gla_constants():
    c = GLA_CHUNK
    t = np.arange(c)
    row, col = t[:, None], t[None, :]
    mats = [col <= row]
    masks = []
    for m in GLA_LEVELS:
        blk = 2 * m
        start = (t // blk) * blk
        mid = (start + m)[:, None]
        second = ((t % blk) >= m)[:, None]
        q_side = second & (col > mid) & (col <= row)
        k_side = (~second) & (col > row) & (col <= mid)
        mats.append(q_side | k_side)
        masks.append((start[:, None] == start[None, :]) & second & (~second).T)
    masks.append(np.eye(c, dtype=bool))
    lm = np.concatenate(mats, axis=0).astype(np.float32)
    mk = np.stack(masks, axis=0).astype(np.float32)
    return lm, mk


def _gla_core_kernel(q_ref, k_ref, v_ref, g_ref, z_ref, wg_ref, bg_ref, hg_ref,
                     lm_ref, mk_ref, o_ref, la_scr, st_scr):
    c = GLA_CHUNK
    rows = q_ref.shape[0]

    @pl.when(pl.program_id(2) == 0)
    def _():
        st_scr[...] = jnp.zeros_like(st_scr)

    logits = _dot(z_ref[...].astype(jnp.bfloat16), wg_ref[...]) + bg_ref[...]
    log_sig = jnp.minimum(logits, 0.0) - jnp.log1p(jnp.exp(-jnp.abs(logits)))
    la_scr[...] = log_sig * (1.0 / GLA_TAU)

    scale = GLA_DK ** -0.5
    gain = hg_ref[...]

    def chunk(ci, carry):
        r0 = pl.multiple_of(ci * c, c)
        rs = pl.ds(r0, c)
        la = la_scr[rs, :]
        la_hi = la.astype(jnp.bfloat16)
        la_lo = (la - la_hi.astype(jnp.float32)).astype(jnp.bfloat16)
        sums = _dot(lm_ref[...], la_hi) + _dot(lm_ref[...], la_lo)
        b = sums[0:c]
        q = q_ref[rs, :].astype(jnp.float32)
        k = k_ref[rs, :].astype(jnp.float32)
        v = v_ref[rs, :]

        attn = mk_ref[len(GLA_LEVELS)] * _dot(q_ref[rs, :], k_ref[rs, :], _NT)
        for li in range(len(GLA_LEVELS)):
            e = jnp.exp(sums[(li + 1) * c:(li + 2) * c])
            ql = (q * e).astype(jnp.bfloat16)
            kl = (k * e).astype(jnp.bfloat16)
            attn = attn + mk_ref[li] * _dot(ql, kl, _NT)

        st = st_scr[...]
        qd = (q * jnp.exp(b)).astype(jnp.bfloat16)
        o = _dot(attn.astype(jnp.bfloat16), v) + _dot(qd, st.astype(jnp.bfloat16), _NT)
        o = o * scale

        b_last = b[c - 1:c, :]
        kd = (k * jnp.exp(b_last - b)).astype(jnp.bfloat16)
        st_scr[...] = st * jnp.exp(b_last) + _dot(v, kd, _TN)

        on = o * _rms_scale(o) * gain
        gate = g_ref[rs, :].astype(jnp.float32)
        o_ref[rs, :] = (on * (gate * jax.nn.sigmoid(gate))).astype(o_ref.dtype)
        return carry

    lax.fori_loop(0, rows // c, chunk, 0)


def _gla_core(qkvg, z, w_gate, b_gate, head_gain, batch, seq):
    t = qkvg.shape[0]
    nblk = seq // GLA_ROWS
    lm, mk = _gla_constants()
    lm = jnp.asarray(lm, jnp.bfloat16)
    mk = jnp.asarray(mk, jnp.float32)
    kq = (GLA_HEADS * GLA_DK) // GLA_DK
    kv = (2 * GLA_HEADS * GLA_DK) // GLA_DV
    kg = kv + GLA_HEADS
    row = lambda b, h, s: b * nblk + s
    return pl.pallas_call(
        _gla_core_kernel,
        grid=(batch, GLA_HEADS, nblk),
        in_specs=[
            pl.BlockSpec((GLA_ROWS, GLA_DK), lambda b, h, s: (row(b, h, s), h)),
            pl.BlockSpec((GLA_ROWS, GLA_DK), lambda b, h, s: (row(b, h, s), kq + h)),
            pl.BlockSpec((GLA_ROWS, GLA_DV), lambda b, h, s: (row(b, h, s), kv + h)),
            pl.BlockSpec((GLA_ROWS, GLA_DV), lambda b, h, s: (row(b, h, s), kg + h)),
            pl.BlockSpec((GLA_ROWS, LANES), lambda b, h, s: (row(b, h, s), 0)),
            pl.BlockSpec((LANES, GLA_DK), lambda b, h, s: (0, h)),
            pl.BlockSpec((1, GLA_DK), lambda b, h, s: (0, h)),
            pl.BlockSpec((1, GLA_DV), lambda b, h, s: (0, 0)),
            pl.BlockSpec(lm.shape, lambda b, h, s: (0, 0)),
            pl.BlockSpec(mk.shape, lambda b, h, s: (0, 0, 0)),
        ],
        out_specs=pl.BlockSpec((GLA_ROWS, GLA_DV), lambda b, h, s: (row(b, h, s), h)),
        out_shape=jax.ShapeDtypeStruct((t, GLA_HEADS * GLA_DV), jnp.bfloat16),
        scratch_shapes=[
            pltpu.VMEM((GLA_ROWS, GLA_DK), jnp.float32),
            pltpu.VMEM((GLA_DV, GLA_DK), jnp.float32),
        ],
        compiler_params=_params("parallel", "parallel", "arbitrary"),
        name="gla_core",
    )(qkvg, qkvg, qkvg, qkvg, z, w_gate, b_gate, head_gain, lm, mk)


def _outproj_mlp_kernel(a_ref, h_ref, wo_ref, g_ref, w1_ref, w2_ref, o_ref):
    h1 = h_ref[...] + _dot(a_ref[...], wo_ref[...])
    xn = (h1 * _rms_scale(h1) * g_ref[...]).astype(jnp.bfloat16)
    acc = h1
    d = h1.shape[1]
    for c0 in range(0, w1_ref.shape[1], d):
        hid = jnp.maximum(_dot(xn, w1_ref[:, c0:c0 + d]), 0.0)
        acc = acc + _dot((hid * hid).astype(jnp.bfloat16), w2_ref[c0:c0 + d, :])
    o_ref[...] = acc


def _outproj_mlp(a, h, w_out, gain, w1, w2):
    t, d = h.shape
    tile = pl.BlockSpec((ROW_TILE, d), lambda i: (i, 0))
    return pl.pallas_call(
        _outproj_mlp_kernel,
        grid=(t // ROW_TILE,),
        in_specs=[tile, tile, _resident(w_out.shape), _resident((1, d)),
                  _resident(w1.shape), _resident(w2.shape)],
        out_specs=tile,
        out_shape=jax.ShapeDtypeStruct((t, d), jnp.float32),
        compiler_params=_params("parallel"),
        name="outproj_mlp",
    )(a, h, w_out, gain, w1, w2)


def _segment_norm(y, seg_ones, gain):
    outs = []
    for c0 in range(0, y.shape[1], LANES):
        yc = y[:, c0:c0 + LANES]
        ss = _dot((yc * yc).astype(jnp.bfloat16), seg_ones)
        outs.append(yc * lax.rsqrt(ss * (1.0 / DIFF_DH) + NORM_EPS) * gain)
    return outs


def _diff_proj_kernel(h_ref, gkv_ref, gq_ref, wkv_ref, wq_ref, kg_ref, qg_ref, so_ref,
                      k_ref, v_ref, q_ref):
    x = h_ref[...]
    xhat = x * _rms_scale(x)
    xkv = (xhat * gkv_ref[...]).astype(jnp.bfloat16)
    xq = (xhat * gq_ref[...]).astype(jnp.bfloat16)
    nk = k_ref.shape[1]
    seg_ones = so_ref[...]
    kraw = _dot(xkv, wkv_ref[:, :nk])
    for i, blk in enumerate(_segment_norm(kraw, seg_ones, kg_ref[...])):
        k_ref[:, i * LANES:(i + 1) * LANES] = blk.astype(k_ref.dtype)
    v_ref[...] = _dot(xkv, wkv_ref[:, nk:]).astype(v_ref.dtype)
    qraw = _dot(xq, wq_ref[...])
    qscale = DIFF_DH ** -0.5 * math.log2(math.e)
    for i, blk in enumerate(_segment_norm(qraw, seg_ones, qg_ref[...])):
        q_ref[:, i * LANES:(i + 1) * LANES] = (blk * qscale).astype(q_ref.dtype)


def _diff_proj(h, g_kv, g_q, w_kv, w_q, k_gain, q_gain):
    t, d = h.shape
    nk = DIFF_HEADS * 2 * DIFF_DH
    nv = DIFF_HEADS * DIFF_VD
    lane = np.arange(LANES)
    seg_ones = jnp.asarray(lane[:, None] // DIFF_DH == lane[None, :] // DIFF_DH, jnp.bfloat16)
    tile = lambda n: pl.BlockSpec((ROW_TILE, n), lambda i: (i, 0))
    return pl.pallas_call(
        _diff_proj_kernel,
        grid=(t // ROW_TILE,),
        in_specs=[tile(d), _resident((1, d)), _resident((1, d)), _resident(w_kv.shape),
                  _resident(w_q.shape), _resident((1, LANES)), _resident((1, LANES)),
                  _resident((LANES, LANES))],
        out_specs=[tile(nk), tile(nv), tile(nk)],
        out_shape=[jax.ShapeDtypeStruct((t, nk), jnp.bfloat16),
                   jax.ShapeDtypeStruct((t, nv), jnp.bfloat16),
                   jax.ShapeDtypeStruct((t, nk), jnp.bfloat16)],
        compiler_params=_params("parallel"),
        name="diff_proj",
    )(h, g_kv, g_q, w_kv, w_q, k_gain, q_gain, seg_ones)


def _diff_attn_kernel(q_ref, k_ref, v_ref, lam_ref, hg_ref, o_ref,
                      m_scr, l_scr, acc_scr, *, lambda_init):
    qi = pl.program_id(2)
    tq = q_ref.shape[0]
    tk = ATT_TK

    m_scr[...] = jnp.full_like(m_scr, NEG_BIG)
    l_scr[...] = jnp.zeros_like(l_scr)
    acc_scr[...] = jnp.zeros_like(acc_scr)

    q = q_ref[...]
    lane = lax.broadcasted_iota(jnp.int32, q.shape, 1)
    qcs = [jnp.where((lane // DIFF_DH) == c, q, jnp.zeros_like(q)) for c in range(2)]

    def step(j, masked):
        rs = pl.ds(pl.multiple_of(j * tk, tk), tk)
        k = k_ref[rs, :]
        v = v_ref[rs, :]
        if masked:
            rowi = lax.broadcasted_iota(jnp.int32, (tq, tk), 0)
            coli = lax.broadcasted_iota(jnp.int32, (tq, tk), 1)
            keep = coli <= rowi
        scores = [_dot(qcs[c], k, _NT) for c in range(2)]
        for c in range(2):
            s = scores[c]
            if masked:
                s = jnp.where(keep, s, NEG_BIG)
            m_prev = m_scr[c]
            m_new = jnp.maximum(m_prev, jnp.max(s, axis=-1, keepdims=True))
            alpha = jnp.exp2(m_prev - m_new)
            p = jnp.exp2(s - jnp.tile(m_new, (1, tk // LANES)))
            psum = p[:, 0:LANES]
            for t0 in range(LANES, tk, LANES):
                psum = psum + p[:, t0:t0 + LANES]
            l_scr[c] = alpha * l_scr[c] + psum
            acc_scr[c] = alpha * acc_scr[c] + _dot(p.astype(jnp.bfloat16), v)
            m_scr[c] = m_new

    def body(j, carry):
        step(j, False)
        return carry

    lax.fori_loop(0, qi, body, 0)
    step(qi, True)

    lp = lam_ref[...]
    lam = (jnp.exp(jnp.sum(lp[0:1] * lp[1:2], axis=-1, keepdims=True))
           - jnp.exp(jnp.sum(lp[2:3] * lp[3:4], axis=-1, keepdims=True)) + lambda_init)
    l0 = jnp.sum(l_scr[0], axis=-1, keepdims=True)
    l1 = jnp.sum(l_scr[1], axis=-1, keepdims=True)
    o = acc_scr[0] / l0 - lam * (acc_scr[1] / l1)
    on = o * _rms_scale(o) * hg_ref[...]
    o_ref[...] = (on * (1.0 - lambda_init)).astype(o_ref.dtype)


def _diff_attn(q, k, v, lam_params, head_gain, batch, seq, lambda_init):
    t = q.shape[0]
    assert ATT_TQ == ATT_TK
    nq = seq // ATT_TQ
    kern = functools.partial(_diff_attn_kernel, lambda_init=lambda_init)
    return pl.pallas_call(
        kern,
        grid=(batch, DIFF_HEADS, nq),
        in_specs=[
            pl.BlockSpec((ATT_TQ, 2 * DIFF_DH), lambda b, h, i: (b * nq + i, h)),
            pl.BlockSpec((seq, 2 * DIFF_DH), lambda b, h, i: (b, h)),
            pl.BlockSpec((seq, DIFF_VD), lambda b, h, i: (b, h)),
            pl.BlockSpec(lam_params.shape, lambda b, h, i: (0, 0)),
            pl.BlockSpec((1, DIFF_VD), lambda b, h, i: (0, 0)),
        ],
        out_specs=pl.BlockSpec((ATT_TQ, DIFF_VD), lambda b, h, i: (b * nq + i, h)),
        out_shape=jax.ShapeDtypeStruct((t, DIFF_HEADS * DIFF_VD), jnp.bfloat16),
        scratch_shapes=[
            pltpu.VMEM((2, ATT_TQ, LANES), jnp.float32),
            pltpu.VMEM((2, ATT_TQ, LANES), jnp.float32),
            pltpu.VMEM((2, ATT_TQ, DIFF_VD), jnp.float32),
        ],
        compiler_params=_params("parallel", "parallel", "arbitrary"),
        name="diff_attn",
    )(q, k, v, lam_params, head_gain)


def kernel(x, a_norm, a_w_in, a_w_gate_up, a_b_gate, a_head_norm, a_w_out, kv_norm, w_kv, k_norm,
           b_norm, b_w_q, b_q_norm, b_lambda, b_head_norm, b_w_out, mlp_norm, mlp_w1, mlp_w2):
    batch, seq, d = x.shape
    bf = jnp.bfloat16
    row = lambda p: p.reshape(1, -1)
    h = x.reshape(batch * seq, d)

    n_main = 2 * GLA_HEADS * GLA_DK + 2 * GLA_HEADS * GLA_DV
    w_in = a_w_in[0]
    w_main = w_in[:, :n_main].astype(bf)
    w_z = jnp.pad(w_in[:, n_main:], ((0, 0), (0, LANES - GLA_RANK))).astype(bf)
    w_gate = jnp.pad(a_w_gate_up[0], ((0, LANES - GLA_RANK), (0, 0))).astype(bf)
    qkvg, z = _gla_inproj(h, row(a_norm[0]), w_main, w_z)
    og = _gla_core(qkvg, z, w_gate, row(a_b_gate[0]), row(a_head_norm[0]), batch, seq)
    h = _outproj_mlp(og, h, a_w_out[0].astype(bf), row(mlp_norm[0]),
                     mlp_w1[0].astype(bf), mlp_w2[0].astype(bf))

    layer = 1
    lambda_init = 0.8 - 0.6 * math.exp(-0.3 * layer)
    reps = LANES // DIFF_DH
    kn, v, qn = _diff_proj(h, row(kv_norm), row(b_norm[0]), w_kv.astype(bf), b_w_q[0].astype(bf),
                           row(jnp.tile(k_norm, reps)), row(jnp.tile(b_q_norm[0], reps)))
    oa = _diff_attn(qn, kn, v, b_lambda[0], row(b_head_norm[0]), batch, seq, lambda_init)
    h = _outproj_mlp(oa, h, b_w_out[0].astype(bf), row(mlp_norm[1]),
                     mlp_w1[1].astype(bf), mlp_w2[1].astype(bf))
    return h.reshape(batch, seq, d)
```

```python
import functools
import math

import numpy as np
import jax
import jax.numpy as jnp
from jax import lax
from jax.experimental import pallas as pl
from jax.experimental.pallas import tpu as pltpu

D_MODEL = 1024
GLA_HEADS = 4
GLA_DK = 128
GLA_DV = 256
GLA_RANK = 16
GLA_TAU = 16.0
GLA_CHUNK = 64
DIFF_HEADS = 8
DIFF_DH = 64
DIFF_VD = 128
MLP_HIDDEN = 4 * D_MODEL
NORM_EPS = 1e-6

LANES = 128
VMEM_LIMIT = 56 * 1024 * 1024
NEG_BIG = -1e30

ROW_TILE = 512
GLA_ROWS = 512
ATT_TQ = 512
ATT_TK = 512
ATT_ROW_BLOCK = 32
GLA_LEVELS = (32, 16, 8, 4, 2, 1)

_NT = (((1,), (1,)), ((), ()))
_TN = (((0,), (0,)), ((), ()))


def _dot(a, b, dims=None):
    if dims is None:
        return jnp.dot(a, b, preferred_element_type=jnp.float32)
    return lax.dot_general(a, b, dims, preferred_element_type=jnp.float32)


def _params(*sem):
    return pltpu.CompilerParams(dimension_semantics=sem, vmem_limit_bytes=VMEM_LIMIT)


def _resident(shape):
    nd = len(shape)
    return pl.BlockSpec(shape, lambda *_: (0,) * nd, pipeline_mode=pl.Buffered(1))


def _rms_scale(x):
    return lax.rsqrt(jnp.mean(x * x, axis=-1, keepdims=True) + NORM_EPS)


def _gla_inproj_kernel(x_ref, g_ref, w_ref, wz_ref, o_ref, z_ref):
    x = x_ref[...]
    xn = (x * _rms_scale(x) * g_ref[...]).astype(jnp.bfloat16)
    n_out = o_ref.shape[1]
    step = 768
    for n0 in range(0, n_out, step):
        o_ref[:, n0:n0 + step] = _dot(xn, w_ref[:, n0:n0 + step]).astype(o_ref.dtype)
    z_ref[...] = _dot(xn, wz_ref[...])


def _gla_inproj(x, gain, w_main, w_z):
    t, d = x.shape
    n = w_main.shape[1]
    return pl.pallas_call(
        _gla_inproj_kernel,
        grid=(t // ROW_TILE,),
        in_specs=[
            pl.BlockSpec((ROW_TILE, d), lambda i: (i, 0)),
            _resident((1, d)),
            _resident((d, n)),
            _resident((d, LANES)),
        ],
        out_specs=[
            pl.BlockSpec((ROW_TILE, n), lambda i: (i, 0)),
            pl.BlockSpec((ROW_TILE, LANES), lambda i: (i, 0)),
        ],
        out_shape=[
            jax.ShapeDtypeStruct((t, n), jnp.bfloat16),
            jax.ShapeDtypeStruct((t, LANES), jnp.float32),
        ],
        compiler_params=_params("parallel"),
        name="gla_inproj",
    )(x, gain, w_main, w_z)


def _gla_constants():
    c = GLA_CHUNK
    t = np.arange(c)
    row, col = t[:, None], t[None, :]
    mats = [col <= row]
    masks = []
    for m in GLA_LEVELS:
        blk = 2 * m
        start = (t // blk) * blk
        mid = (start + m)[:, None]
        second = ((t % blk) >= m)[:, None]
        q_side = second & (col > mid) & (col <= row)
        k_side = (~second) & (col > row) & (col <= mid)
        mats.append(q_side | k_side)
        masks.append((start[:, None] == start[None, :]) & second & (~second).T)
    masks.append(np.eye(c, dtype=bool))
    lm = np.concatenate(mats, axis=0).astype(np.float32)
    mk = np.stack(masks, axis=0).astype(np.float32)
    return lm, mk


def _gla_core_kernel(q_ref, k_ref, v_ref, g_ref, z_ref, wg_ref, bg_ref, hg_ref,
                     lm_ref, mk_ref, o_ref, st_scr):
    c = GLA_CHUNK
    rows = q_ref.shape[0]

    @pl.when(pl.program_id(2) == 0)
    def _():
        st_scr[...] = jnp.zeros_like(st_scr)

    logits = _dot(z_ref[...].astype(jnp.bfloat16), wg_ref[...]) + bg_ref[...]
    log_sig = jnp.minimum(logits, 0.0) - jnp.log1p(jnp.exp(-jnp.abs(logits)))
    la = log_sig * (1.0 / GLA_TAU)
    la_hi = la.astype(jnp.bfloat16)
    la_lo = (la - la_hi.astype(jnp.float32)).astype(jnp.bfloat16)
    la_split = jnp.concatenate([la_hi, la_lo], axis=1)

    scale = GLA_DK ** -0.5
    gain = hg_ref[...]
    st = st_scr[...]

    for ci in range(rows // c):
        rs = slice(ci * c, (ci + 1) * c)
        sums2 = _dot(lm_ref[...], la_split[rs])
        sums = sums2[:, :GLA_DK] + sums2[:, GLA_DK:]
        b = sums[0:c]
        q = q_ref[rs, :].astype(jnp.float32)
        k = k_ref[rs, :].astype(jnp.float32)
        v = v_ref[rs, :]

        attn = mk_ref[len(GLA_LEVELS)] * _dot(q_ref[rs, :], k_ref[rs, :], _NT)
        for li in range(len(GLA_LEVELS)):
            e = jnp.exp(sums[(li + 1) * c:(li + 2) * c])
            ql = (q * e).astype(jnp.bfloat16)
            kl = (k * e).astype(jnp.bfloat16)
            attn = attn + mk_ref[li] * _dot(ql, kl, _NT)

        b_last = b[c - 1:c, :]
        qd = (q * jnp.exp(b)).astype(jnp.bfloat16)
        kd = (k * jnp.exp(b_last - b)).astype(jnp.bfloat16)
        o = _dot(attn.astype(jnp.bfloat16), v) + _dot(qd, st.astype(jnp.bfloat16), _NT)
        o = o * scale
        st = st * jnp.exp(b_last) + _dot(v, kd, _TN)

        on = o * _rms_scale(o) * gain
        gate = g_ref[rs, :].astype(jnp.float32)
        o_ref[rs, :] = (on * (gate * jax.nn.sigmoid(gate))).astype(o_ref.dtype)

    st_scr[...] = st


def _gla_core(qkvg, z, w_gate, b_gate, head_gain, batch, seq):
    t = qkvg.shape[0]
    nblk = seq // GLA_ROWS
    lm, mk = _gla_constants()
    lm = jnp.asarray(lm, jnp.bfloat16)
    mk = jnp.asarray(mk, jnp.float32)
    kq = (GLA_HEADS * GLA_DK) // GLA_DK
    kv = (2 * GLA_HEADS * GLA_DK) // GLA_DV
    kg = kv + GLA_HEADS
    row = lambda b, h, s: b * nblk + s
    return pl.pallas_call(
        _gla_core_kernel,
        grid=(batch, GLA_HEADS, nblk),
        in_specs=[
            pl.BlockSpec((GLA_ROWS, GLA_DK), lambda b, h, s: (row(b, h, s), h)),
            pl.BlockSpec((GLA_ROWS, GLA_DK), lambda b, h, s: (row(b, h, s), kq + h)),
            pl.BlockSpec((GLA_ROWS, GLA_DV), lambda b, h, s: (row(b, h, s), kv + h)),
            pl.BlockSpec((GLA_ROWS, GLA_DV), lambda b, h, s: (row(b, h, s), kg + h)),
            pl.BlockSpec((GLA_ROWS, LANES), lambda b, h, s: (row(b, h, s), 0)),
            pl.BlockSpec((LANES, GLA_DK), lambda b, h, s: (0, h)),
            pl.BlockSpec((1, GLA_DK), lambda b, h, s: (0, h)),
            pl.BlockSpec((1, GLA_DV), lambda b, h, s: (0, 0)),
            pl.BlockSpec(lm.shape, lambda b, h, s: (0, 0)),
            pl.BlockSpec(mk.shape, lambda b, h, s: (0, 0, 0)),
        ],
        out_specs=pl.BlockSpec((GLA_ROWS, GLA_DV), lambda b, h, s: (row(b, h, s), h)),
        out_shape=jax.ShapeDtypeStruct((t, GLA_HEADS * GLA_DV), jnp.bfloat16),
        scratch_shapes=[pltpu.VMEM((GLA_DV, GLA_DK), jnp.float32)],
        compiler_params=_params("parallel", "parallel", "arbitrary"),
        name="gla_core",
    )(qkvg, qkvg, qkvg, qkvg, z, w_gate, b_gate, head_gain, lm, mk)


def _outproj_mlp_kernel(a_ref, h_ref, wo_ref, g_ref, w1_ref, w2_ref, o_ref):
    h1 = h_ref[...] + _dot(a_ref[...], wo_ref[...])
    xn = (h1 * _rms_scale(h1) * g_ref[...]).astype(jnp.bfloat16)
    acc = h1
    d = h1.shape[1]
    for c0 in range(0, w1_ref.shape[1], d):
        hid = jnp.maximum(_dot(xn, w1_ref[:, c0:c0 + d]), 0.0)
        acc = acc + _dot((hid * hid).astype(jnp.bfloat16), w2_ref[c0:c0 + d, :])
    o_ref[...] = acc


def _outproj_mlp(a, h, w_out, gain, w1, w2):
    t, d = h.shape
    tile = pl.BlockSpec((ROW_TILE, d), lambda i: (i, 0))
    return pl.pallas_call(
        _outproj_mlp_kernel,
        grid=(t // ROW_TILE,),
        in_specs=[tile, tile, _resident(w_out.shape), _resident((1, d)),
                  _resident(w1.shape), _resident(w2.shape)],
        out_specs=tile,
        out_shape=jax.ShapeDtypeStruct((t, d), jnp.float32),
        compiler_params=_params("parallel"),
        name="outproj_mlp",
    )(a, h, w_out, gain, w1, w2)


def _segment_norm(y, seg_ones, gain):
    outs = []
    for c0 in range(0, y.shape[1], LANES):
        yc = y[:, c0:c0 + LANES]
        ss = _dot((yc * yc).astype(jnp.bfloat16), seg_ones)
        outs.append(yc * lax.rsqrt(ss * (1.0 / DIFF_DH) + NORM_EPS) * gain)
    return outs


def _diff_proj_kernel(h_ref, gkv_ref, gq_ref, wkv_ref, wq_ref, kg_ref, qg_ref, so_ref,
                      k_ref, v_ref, q_ref):
    x = h_ref[...]
    xhat = x * _rms_scale(x)
    xkv = (xhat * gkv_ref[...]).astype(jnp.bfloat16)
    xq = (xhat * gq_ref[...]).astype(jnp.bfloat16)
    nk = k_ref.shape[1]
    seg_ones = so_ref[...]
    kraw = _dot(xkv, wkv_ref[:, :nk])
    for i, blk in enumerate(_segment_norm(kraw, seg_ones, kg_ref[...])):
        k_ref[:, i * LANES:(i + 1) * LANES] = blk.astype(k_ref.dtype)
    v_ref[...] = _dot(xkv, wkv_ref[:, nk:]).astype(v_ref.dtype)
    qraw = _dot(xq, wq_ref[...])
    qscale = DIFF_DH ** -0.5 * math.log2(math.e)
    for i, blk in enumerate(_segment_norm(qraw, seg_ones, qg_ref[...])):
        q_ref[:, i * LANES:(i + 1) * LANES] = (blk * qscale).astype(q_ref.dtype)


def _diff_proj(h, g_kv, g_q, w_kv, w_q, k_gain, q_gain):
    t, d = h.shape
    nk = DIFF_HEADS * 2 * DIFF_DH
    nv = DIFF_HEADS * DIFF_VD
    lane = np.arange(LANES)
    seg_ones = jnp.asarray(lane[:, None] // DIFF_DH == lane[None, :] // DIFF_DH, jnp.bfloat16)
    tile = lambda n: pl.BlockSpec((ROW_TILE, n), lambda i: (i, 0))
    return pl.pallas_call(
        _diff_proj_kernel,
        grid=(t // ROW_TILE,),
        in_specs=[tile(d), _resident((1, d)), _resident((1, d)), _resident(w_kv.shape),
                  _resident(w_q.shape), _resident((1, LANES)), _resident((1, LANES)),
                  _resident((LANES, LANES))],
        out_specs=[tile(nk), tile(nv), tile(nk)],
        out_shape=[jax.ShapeDtypeStruct((t, nk), jnp.bfloat16),
                   jax.ShapeDtypeStruct((t, nv), jnp.bfloat16),
                   jax.ShapeDtypeStruct((t, nk), jnp.bfloat16)],
        compiler_params=_params("parallel"),
        name="diff_proj",
    )(h, g_kv, g_q, w_kv, w_q, k_gain, q_gain, seg_ones)


def _diff_attn_kernel(q_ref, k_ref, v_ref, lam_ref, hg_ref, o_ref,
                      m_scr, l_scr, acc_scr, s_bufs, p_bufs, *, lambda_init):
    qi = pl.program_id(2)
    tq = q_ref.shape[0]
    tk = ATT_TK

    m_scr[...] = jnp.full_like(m_scr, NEG_BIG)
    l_scr[...] = jnp.zeros_like(l_scr)
    acc_scr[...] = jnp.zeros_like(acc_scr)

    q = q_ref[...]
    lane = lax.broadcasted_iota(jnp.int32, q.shape, 1)
    qcs = [jnp.where((lane // DIFF_DH) == c, q, jnp.zeros_like(q)) for c in range(2)]

    def block(j):
        return pl.ds(pl.multiple_of(j * tk, tk), tk)

    rb = ATT_ROW_BLOCK
    col_minus_row = (lax.broadcasted_iota(jnp.int32, (rb, tk), 1)
                     - lax.broadcasted_iota(jnp.int32, (rb, tk), 0))

    def step(j, masked, buf):
        k = k_ref[block(j), :]
        v = v_ref[block(j), :]
        s_scr = s_bufs.at[buf]
        p_scr = p_bufs.at[buf]
        for c in range(2):
            s_scr[c] = _dot(qcs[c], k, _NT)
        for c in range(2):
            for r0 in range(0, tq, rb):
                rows = slice(r0, r0 + rb)
                kmax = min(tk, -(-(r0 + rb) // LANES) * LANES) if masked else tk
                s = s_scr[c, rows, 0:kmax]
                if masked:
                    s = jnp.where(col_minus_row[:, 0:kmax] <= r0, s, NEG_BIG)
                m_prev = m_scr[c, rows, :]
                m_new = jnp.maximum(m_prev, jnp.max(s, axis=-1, keepdims=True))
                alpha = jnp.exp2(m_prev - m_new)
                p = jnp.exp2(s - jnp.tile(m_new, (1, kmax // LANES)))
                psum = p[:, 0:LANES]
                for t0 in range(LANES, kmax, LANES):
                    psum = psum + p[:, t0:t0 + LANES]
                l_scr[c, rows, :] = alpha * l_scr[c, rows, :] + psum
                m_scr[c, rows, :] = m_new
                acc_scr[c, rows, :] = alpha * acc_scr[c, rows, :]
                p_scr[c, rows, 0:kmax] = p.astype(jnp.bfloat16)
                if kmax < tk:
                    p_scr[c, rows, kmax:tk] = jnp.zeros((rb, tk - kmax), jnp.bfloat16)
            acc_scr[c] += _dot(p_scr[c], v)

    def pair(i, carry):
        step(2 * i, False, 0)
        step(2 * i + 1, False, 1)
        return carry

    lax.fori_loop(0, qi // 2, pair, 0)

    @pl.when(qi % 2 == 1)
    def _():
        step(qi - 1, False, 0)

    step(qi, True, 1)

    lp = lam_ref[...]
    lam = (jnp.exp(jnp.sum(lp[0:1] * lp[1:2], axis=-1, keepdims=True))
           - jnp.exp(jnp.sum(lp[2:3] * lp[3:4], axis=-1, keepdims=True)) + lambda_init)
    l0 = jnp.sum(l_scr[0], axis=-1, keepdims=True)
    l1 = jnp.sum(l_scr[1], axis=-1, keepdims=True)
    o = acc_scr[0] / l0 - lam * (acc_scr[1] / l1)
    on = o * _rms_scale(o) * hg_ref[...]
    o_ref[...] = (on * (1.0 - lambda_init)).astype(o_ref.dtype)


def _diff_attn(q, k, v, lam_params, head_gain, batch, seq, lambda_init):
    t = q.shape[0]
    assert ATT_TQ == ATT_TK
    nq = seq // ATT_TQ
    kern = functools.partial(_diff_attn_kernel, lambda_init=lambda_init)
    return pl.pallas_call(
        kern,
        grid=(batch, DIFF_HEADS, nq),
        in_specs=[
            pl.BlockSpec((ATT_TQ, 2 * DIFF_DH), lambda b, h, i: (b * nq + i, h)),
            pl.BlockSpec((seq, 2 * DIFF_DH), lambda b, h, i: (b, h)),
            pl.BlockSpec((seq, DIFF_VD), lambda b, h, i: (b, h)),
            pl.BlockSpec(lam_params.shape, lambda b, h, i: (0, 0)),
            pl.BlockSpec((1, DIFF_VD), lambda b, h, i: (0, 0)),
        ],
        out_specs=pl.BlockSpec((ATT_TQ, DIFF_VD), lambda b, h, i: (b * nq + i, h)),
        out_shape=jax.ShapeDtypeStruct((t, DIFF_HEADS * DIFF_VD), jnp.bfloat16),
        scratch_shapes=[
            pltpu.VMEM((2, ATT_TQ, LANES), jnp.float32),
            pltpu.VMEM((2, ATT_TQ, LANES), jnp.float32),
            pltpu.VMEM((2, ATT_TQ, DIFF_VD), jnp.float32),
            pltpu.VMEM((2, 2, ATT_TQ, ATT_TK), jnp.float32),
            pltpu.VMEM((2, 2, ATT_TQ, ATT_TK), jnp.bfloat16),
        ],
        compiler_params=_params("parallel", "parallel", "arbitrary"),
        name="diff_attn",
    )(q, k, v, lam_params, head_gain)


def kernel(x, a_norm, a_w_in, a_w_gate_up, a_b_gate, a_head_norm, a_w_out, kv_norm, w_kv, k_norm,
           b_norm, b_w_q, b_q_norm, b_lambda, b_head_norm, b_w_out, mlp_norm, mlp_w1, mlp_w2):
    batch, seq, d = x.shape
    bf = jnp.bfloat16
    row = lambda p: p.reshape(1, -1)
    h = x.reshape(batch * seq, d)

    n_main = 2 * GLA_HEADS * GLA_DK + 2 * GLA_HEADS * GLA_DV
    w_in = a_w_in[0]
    w_main = w_in[:, :n_main].astype(bf)
    w_z = jnp.pad(w_in[:, n_main:], ((0, 0), (0, LANES - GLA_RANK))).astype(bf)
    w_gate = jnp.pad(a_w_gate_up[0], ((0, LANES - GLA_RANK), (0, 0))).astype(bf)
    qkvg, z = _gla_inproj(h, row(a_norm[0]), w_main, w_z)
    og = _gla_core(qkvg, z, w_gate, row(a_b_gate[0]), row(a_head_norm[0]), batch, seq)
    h = _outproj_mlp(og, h, a_w_out[0].astype(bf), row(mlp_norm[0]),
                     mlp_w1[0].astype(bf), mlp_w2[0].astype(bf))

    layer = 1
    lambda_init = 0.8 - 0.6 * math.exp(-0.3 * layer)
    reps = LANES // DIFF_DH
    kn, v, qn = _diff_proj(h, row(kv_norm), row(b_norm[0]), w_kv.astype(bf), b_w_q[0].astype(bf),
                           row(jnp.tile(k_norm, reps)), row(jnp.tile(b_q_norm[0], reps)))
    oa = _diff_attn(qn, kn, v, b_lambda[0], row(b_head_norm[0]), batch, seq, lambda_init)
    h = _outproj_mlp(oa, h, b_w_out[0].astype(bf), row(mlp_norm[1]),
                     mlp_w1[1].astype(bf), mlp_w2[1].astype(bf))
    return h.reshape(batch, seq, d)
```

```python
import functools
import math

import numpy as np
import jax
import jax.numpy as jnp
from jax import lax
from jax.experimental import pallas as pl
from jax.experimental.pallas import tpu as pltpu

D_MODEL = 1024
GLA_HEADS = 4
GLA_DK = 128
GLA_DV = 256
GLA_RANK = 16
GLA_TAU = 16.0
GLA_CHUNK = 64
DIFF_HEADS = 8
DIFF_DH = 64
DIFF_VD = 128
MLP_HIDDEN = 4 * D_MODEL
NORM_EPS = 1e-6

LANES = 128
VMEM_LIMIT = 56 * 1024 * 1024
NEG_BIG = -1e30

ROW_TILE = 512
GLA_ROWS = 1024
ATT_TQ = 1024
ATT_TK = 512
ATT_ROW_BLOCK = 32
GLA_LEVELS = (32, 16, 8, 4, 2, 1)
SUBLANES = 8
_GLA_N_VPU_LEVELS = sum(m >= SUBLANES for m in GLA_LEVELS)

_NT = (((1,), (1,)), ((), ()))
_TN = (((0,), (0,)), ((), ()))


def _dot(a, b, dims=None):
    if dims is None:
        return jnp.dot(a, b, preferred_element_type=jnp.float32)
    return lax.dot_general(a, b, dims, preferred_element_type=jnp.float32)


def _params(*sem):
    return pltpu.CompilerParams(dimension_semantics=sem, vmem_limit_bytes=VMEM_LIMIT)


def _resident(shape):
    nd = len(shape)
    return pl.BlockSpec(shape, lambda *_: (0,) * nd, pipeline_mode=pl.Buffered(1))


def _rms_scale(x):
    return lax.rsqrt(jnp.mean(x * x, axis=-1, keepdims=True) + NORM_EPS)


def _gla_inproj_kernel(x_ref, g_ref, w_ref, wz_ref, o_ref, z_ref):
    x = x_ref[...]
    xn = (x * _rms_scale(x) * g_ref[...]).astype(jnp.bfloat16)
    n_out = o_ref.shape[1]
    step = 768
    for n0 in range(0, n_out, step):
        o_ref[:, n0:n0 + step] = _dot(xn, w_ref[:, n0:n0 + step]).astype(o_ref.dtype)
    z_ref[...] = _dot(xn, wz_ref[...])


def _gla_inproj(x, gain, w_main, w_z):
    t, d = x.shape
    n = w_main.shape[1]
    return pl.pallas_call(
        _gla_inproj_kernel,
        grid=(t // ROW_TILE,),
        in_specs=[
            pl.BlockSpec((ROW_TILE, d), lambda i: (i, 0)),
            _resident((1, d)),
            _resident((d, n)),
            _resident((d, LANES)),
        ],
        out_specs=[
            pl.BlockSpec((ROW_TILE, n), lambda i: (i, 0)),
            pl.BlockSpec((ROW_TILE, LANES), lambda i: (i, 0)),
        ],
        out_shape=[
            jax.ShapeDtypeStruct((t, n), jnp.bfloat16),
            jax.ShapeDtypeStruct((t, LANES), jnp.float32),
        ],
        compiler_params=_params("parallel"),
        name="gla_inproj",
    )(x, gain, w_main, w_z)


def _gla_constants():
    c = GLA_CHUNK
    t = np.arange(c)
    row, col = t[:, None], t[None, :]
    mats = [col <= row]
    masks = []
    for m in GLA_LEVELS:
        blk = 2 * m
        start = (t // blk) * blk
        mid = (start + m)[:, None]
        second = ((t % blk) >= m)[:, None]
        q_side = second & (col > mid) & (col <= row)
        k_side = (~second) & (col > row) & (col <= mid)
        if m < SUBLANES:
            mats.append(q_side | k_side)
        masks.append((start[:, None] == start[None, :]) & second & (~second).T)
    masks.append(np.eye(c, dtype=bool))
    lm = np.concatenate(mats, axis=0).astype(np.float32)
    mk = np.stack(masks, axis=0).astype(np.float32)
    return lm, mk


def _gla_core_kernel(q_ref, k_ref, v_ref, g_ref, z_ref, wg_ref, bg_ref, hg_ref,
                     lm_ref, mk_ref, o_ref, st_scr):
    c = GLA_CHUNK
    rows = q_ref.shape[0]

    @pl.when(pl.program_id(2) == 0)
    def _():
        st_scr[...] = jnp.zeros_like(st_scr)

    logits = _dot(z_ref[...].astype(jnp.bfloat16), wg_ref[...]) + bg_ref[...]
    log_sig = jnp.minimum(logits, 0.0) - jnp.log1p(jnp.exp(-jnp.abs(logits)))
    la = log_sig * (math.log2(math.e) / GLA_TAU)
    la_hi = la.astype(jnp.bfloat16)
    la_lo = (la - la_hi.astype(jnp.float32)).astype(jnp.bfloat16)
    la_split = jnp.concatenate([la_hi, la_lo], axis=1)

    scale = GLA_DK ** -0.5
    gain = hg_ref[...]
    n = rows // c
    rs = [slice(ci * c, (ci + 1) * c) for ci in range(n)]
    n_lvl = len(GLA_LEVELS)

    sums = []
    for ci in range(n):
        sums2 = _dot(lm_ref[...], la_split[rs[ci]])
        sums.append(sums2[:, :GLA_DK] + sums2[:, GLA_DK:])
    cum = [s[0:c] for s in sums]

    def level_exponent(ci, li):
        m = GLA_LEVELS[li]
        if m < SUBLANES:
            lo = (1 + li - _GLA_N_VPU_LEVELS) * c
            return sums[ci][lo:lo + c]
        b = cum[ci]
        parts = []
        for start in range(0, c, 2 * m):
            mid = b[start + m:start + m + 1, :]
            parts.append(mid - b[start:start + m])
            parts.append(b[start + m:start + 2 * m] - mid)
        return jnp.concatenate(parts, axis=0)

    qf = [q_ref[r, :].astype(jnp.float32) for r in rs]
    kf = [k_ref[r, :].astype(jnp.float32) for r in rs]
    attn = [mk_ref[n_lvl] * _dot(q_ref[r, :], k_ref[r, :], _NT) for r in rs]
    for li in range(n_lvl):
        for ci in range(n):
            e = jnp.exp2(level_exponent(ci, li))
            ql = (qf[ci] * e).astype(jnp.bfloat16)
            kl = (kf[ci] * e).astype(jnp.bfloat16)
            attn[ci] = attn[ci] + mk_ref[li] * _dot(ql, kl, _NT)

    o_intra, qd, decay, upd = [], [], [], []
    for ci in range(n):
        b = cum[ci]
        b_last = b[c - 1:c, :]
        v = v_ref[rs[ci], :]
        kd = (kf[ci] * jnp.exp2(b_last - b)).astype(jnp.bfloat16)
        qd.append((qf[ci] * jnp.exp2(b)).astype(jnp.bfloat16))
        decay.append(jnp.exp2(b_last))
        upd.append(_dot(v, kd, _TN))
        o_intra.append(_dot(attn[ci].astype(jnp.bfloat16), v))

    st = st_scr[...]
    for ci in range(n):
        o = (o_intra[ci] + _dot(qd[ci], st.astype(jnp.bfloat16), _NT)) * scale
        st = st * decay[ci] + upd[ci]
        on = o * _rms_scale(o) * gain
        gate = g_ref[rs[ci], :].astype(jnp.float32)
        o_ref[rs[ci], :] = (on * (gate * jax.nn.sigmoid(gate))).astype(o_ref.dtype)
    st_scr[...] = st


def _gla_core(qkvg, z, w_gate, b_gate, head_gain, batch, seq):
    t = qkvg.shape[0]
    nblk = seq // GLA_ROWS
    lm, mk = _gla_constants()
    lm = jnp.asarray(lm, jnp.bfloat16)
    mk = jnp.asarray(mk, jnp.float32)
    kq = (GLA_HEADS * GLA_DK) // GLA_DK
    kv = (2 * GLA_HEADS * GLA_DK) // GLA_DV
    kg = kv + GLA_HEADS
    row = lambda b, h, s: b * nblk + s
    return pl.pallas_call(
        _gla_core_kernel,
        grid=(batch, GLA_HEADS, nblk),
        in_specs=[
            pl.BlockSpec((GLA_ROWS, GLA_DK), lambda b, h, s: (row(b, h, s), h)),
            pl.BlockSpec((GLA_ROWS, GLA_DK), lambda b, h, s: (row(b, h, s), kq + h)),
            pl.BlockSpec((GLA_ROWS, GLA_DV), lambda b, h, s: (row(b, h, s), kv + h)),
            pl.BlockSpec((GLA_ROWS, GLA_DV), lambda b, h, s: (row(b, h, s), kg + h)),
            pl.BlockSpec((GLA_ROWS, LANES), lambda b, h, s: (row(b, h, s), 0)),
            pl.BlockSpec((LANES, GLA_DK), lambda b, h, s: (0, h)),
            pl.BlockSpec((1, GLA_DK), lambda b, h, s: (0, h)),
            pl.BlockSpec((1, GLA_DV), lambda b, h, s: (0, 0)),
            pl.BlockSpec(lm.shape, lambda b, h, s: (0, 0)),
            pl.BlockSpec(mk.shape, lambda b, h, s: (0, 0, 0)),
        ],
        out_specs=pl.BlockSpec((GLA_ROWS, GLA_DV), lambda b, h, s: (row(b, h, s), h)),
        out_shape=jax.ShapeDtypeStruct((t, GLA_HEADS * GLA_DV), jnp.bfloat16),
        scratch_shapes=[pltpu.VMEM((GLA_DV, GLA_DK), jnp.float32)],
        compiler_params=_params("parallel", "parallel", "arbitrary"),
        name="gla_core",
    )(qkvg, qkvg, qkvg, qkvg, z, w_gate, b_gate, head_gain, lm, mk)


def _outproj_mlp_kernel(a_ref, h_ref, wo_ref, g_ref, w1_ref, w2_ref, o_ref):
    h1 = h_ref[...] + _dot(a_ref[...], wo_ref[...])
    xn = (h1 * _rms_scale(h1) * g_ref[...]).astype(jnp.bfloat16)
    acc = h1
    d = h1.shape[1]
    for c0 in range(0, w1_ref.shape[1], d):
        hid = jnp.maximum(_dot(xn, w1_ref[:, c0:c0 + d]), 0.0)
        acc = acc + _dot((hid * hid).astype(jnp.bfloat16), w2_ref[c0:c0 + d, :])
    o_ref[...] = acc


def _outproj_mlp(a, h, w_out, gain, w1, w2):
    t, d = h.shape
    tile = pl.BlockSpec((ROW_TILE, d), lambda i: (i, 0))
    return pl.pallas_call(
        _outproj_mlp_kernel,
        grid=(t // ROW_TILE,),
        in_specs=[tile, tile, _resident(w_out.shape), _resident((1, d)),
                  _resident(w1.shape), _resident(w2.shape)],
        out_specs=tile,
        out_shape=jax.ShapeDtypeStruct((t, d), jnp.float32),
        compiler_params=_params("parallel"),
        name="outproj_mlp",
    )(a, h, w_out, gain, w1, w2)


def _segment_norm(y, seg_ones, gain):
    outs = []
    for c0 in range(0, y.shape[1], LANES):
        yc = y[:, c0:c0 + LANES]
        ss = _dot((yc * yc).astype(jnp.bfloat16), seg_ones)
        outs.append(yc * lax.rsqrt(ss * (1.0 / DIFF_DH) + NORM_EPS) * gain)
    return outs


def _diff_proj_kernel(h_ref, gkv_ref, gq_ref, wkv_ref, wq_ref, kg_ref, qg_ref, so_ref,
                      k_ref, v_ref, q_ref):
    x = h_ref[...]
    xhat = x * _rms_scale(x)
    xkv = (xhat * gkv_ref[...]).astype(jnp.bfloat16)
    xq = (xhat * gq_ref[...]).astype(jnp.bfloat16)
    nk = k_ref.shape[1]
    seg_ones = so_ref[...]
    kraw = _dot(xkv, wkv_ref[:, :nk])
    for i, blk in enumerate(_segment_norm(kraw, seg_ones, kg_ref[...])):
        k_ref[:, i * LANES:(i + 1) * LANES] = blk.astype(k_ref.dtype)
    v_ref[...] = _dot(xkv, wkv_ref[:, nk:]).astype(v_ref.dtype)
    qraw = _dot(xq, wq_ref[...])
    qscale = DIFF_DH ** -0.5 * math.log2(math.e)
    for i, blk in enumerate(_segment_norm(qraw, seg_ones, qg_ref[...])):
        q_ref[:, i * LANES:(i + 1) * LANES] = (blk * qscale).astype(q_ref.dtype)


def _diff_proj(h, g_kv, g_q, w_kv, w_q, k_gain, q_gain):
    t, d = h.shape
    nk = DIFF_HEADS * 2 * DIFF_DH
    nv = DIFF_HEADS * DIFF_VD
    lane = np.arange(LANES)
    seg_ones = jnp.asarray(lane[:, None] // DIFF_DH == lane[None, :] // DIFF_DH, jnp.bfloat16)
    tile = lambda n: pl.BlockSpec((ROW_TILE, n), lambda i: (i, 0))
    return pl.pallas_call(
        _diff_proj_kernel,
        grid=(t // ROW_TILE,),
        in_specs=[tile(d), _resident((1, d)), _resident((1, d)), _resident(w_kv.shape),
                  _resident(w_q.shape), _resident((1, LANES)), _resident((1, LANES)),
                  _resident((LANES, LANES))],
        out_specs=[tile(nk), tile(nv), tile(nk)],
        out_shape=[jax.ShapeDtypeStruct((t, nk), jnp.bfloat16),
                   jax.ShapeDtypeStruct((t, nv), jnp.bfloat16),
                   jax.ShapeDtypeStruct((t, nk), jnp.bfloat16)],
        compiler_params=_params("parallel"),
        name="diff_proj",
    )(h, g_kv, g_q, w_kv, w_q, k_gain, q_gain, seg_ones)


def _diff_attn_kernel(q_ref, k_ref, v_ref, lam_ref, hg_ref, o_ref,
                      m_scr, l_scr, acc_scr, s_bufs, p_bufs, *, lambda_init):
    qi = pl.program_id(2)
    tq = q_ref.shape[0]
    tk = ATT_TK

    m_scr[...] = jnp.full_like(m_scr, NEG_BIG)
    l_scr[...] = jnp.zeros_like(l_scr)
    acc_scr[...] = jnp.zeros_like(acc_scr)

    q = q_ref[...]
    lane = lax.broadcasted_iota(jnp.int32, q.shape, 1)
    qcs = [jnp.where((lane // DIFF_DH) == c, q, jnp.zeros_like(q)) for c in range(2)]

    def block(j):
        return pl.ds(pl.multiple_of(j * tk, tk), tk)

    rb = ATT_ROW_BLOCK
    col_minus_row = (lax.broadcasted_iota(jnp.int32, (rb, tk), 1)
                     - lax.broadcasted_iota(jnp.int32, (rb, tk), 0))

    def step(j, buf, row_lo, diag_row0):
        k = k_ref[block(j), :]
        v = v_ref[block(j), :]
        s_scr = s_bufs.at[buf]
        p_scr = p_bufs.at[buf]
        all_rows = slice(row_lo, tq)
        for c in range(2):
            s_scr[c, all_rows, :] = _dot(qcs[c][all_rows], k, _NT)
        for c in range(2):
            for r0 in range(row_lo, tq, rb):
                rows = slice(r0, r0 + rb)
                masked = diag_row0 is not None and r0 - diag_row0 < tk
                kmax = min(tk, -(-(r0 - diag_row0 + rb) // LANES) * LANES) if masked else tk
                s = s_scr[c, rows, 0:kmax]
                if masked:
                    s = jnp.where(col_minus_row[:, 0:kmax] <= r0 - diag_row0, s, NEG_BIG)
                m_prev = m_scr[c, rows, :]
                m_new = jnp.maximum(m_prev, jnp.max(s, axis=-1, keepdims=True))
                alpha = jnp.exp2(m_prev - m_new)
                p = jnp.exp2(s - jnp.tile(m_new, (1, kmax // LANES)))
                psum = p[:, 0:LANES]
                for t0 in range(LANES, kmax, LANES):
                    psum = psum + p[:, t0:t0 + LANES]
                l_scr[c, rows, :] = alpha * l_scr[c, rows, :] + psum
                m_scr[c, rows, :] = m_new
                acc_scr[c, rows, :] = alpha * acc_scr[c, rows, :]
                p_scr[c, rows, 0:kmax] = p.astype(jnp.bfloat16)
                if kmax < tk:
                    p_scr[c, rows, kmax:tk] = jnp.zeros((rb, tk - kmax), jnp.bfloat16)
            acc_scr[c, all_rows, :] += _dot(p_scr[c, all_rows, :], v)

    n_sub = tq // tk

    def pair(i, carry):
        step(2 * i, 0, 0, None)
        step(2 * i + 1, 1, 0, None)
        return carry

    lax.fori_loop(0, (n_sub // 2) * qi, pair, 0)
    for t in range(n_sub):
        step(n_sub * qi + t, t % 2, t * tk, t * tk)

    lp = lam_ref[...]
    lam = (jnp.exp(jnp.sum(lp[0:1] * lp[1:2], axis=-1, keepdims=True))
           - jnp.exp(jnp.sum(lp[2:3] * lp[3:4], axis=-1, keepdims=True)) + lambda_init)
    l0 = jnp.sum(l_scr[0], axis=-1, keepdims=True)
    l1 = jnp.sum(l_scr[1], axis=-1, keepdims=True)
    o = acc_scr[0] / l0 - lam * (acc_scr[1] / l1)
    on = o * _rms_scale(o) * hg_ref[...]
    o_ref[...] = (on * (1.0 - lambda_init)).astype(o_ref.dtype)


def _diff_attn(q, k, v, lam_params, head_gain, batch, seq, lambda_init):
    t = q.shape[0]
    assert ATT_TQ % (2 * ATT_TK) == 0
    nq = seq // ATT_TQ
    kern = functools.partial(_diff_attn_kernel, lambda_init=lambda_init)
    return pl.pallas_call(
        kern,
        grid=(batch, DIFF_HEADS, nq),
        in_specs=[
            pl.BlockSpec((ATT_TQ, 2 * DIFF_DH), lambda b, h, i: (b * nq + i, h)),
            pl.BlockSpec((seq, 2 * DIFF_DH), lambda b, h, i: (b, h)),
            pl.BlockSpec((seq, DIFF_VD), lambda b, h, i: (b, h)),
            pl.BlockSpec(lam_params.shape, lambda b, h, i: (0, 0)),
            pl.BlockSpec((1, DIFF_VD), lambda b, h, i: (0, 0)),
        ],
        out_specs=pl.BlockSpec((ATT_TQ, DIFF_VD), lambda b, h, i: (b * nq + i, h)),
        out_shape=jax.ShapeDtypeStruct((t, DIFF_HEADS * DIFF_VD), jnp.bfloat16),
        scratch_shapes=[
            pltpu.VMEM((2, ATT_TQ, LANES), jnp.float32),
            pltpu.VMEM((2, ATT_TQ, LANES), jnp.float32),
            pltpu.VMEM((2, ATT_TQ, DIFF_VD), jnp.float32),
            pltpu.VMEM((2, 2, ATT_TQ, ATT_TK), jnp.float32),
            pltpu.VMEM((2, 2, ATT_TQ, ATT_TK), jnp.bfloat16),
        ],
        compiler_params=_params("parallel", "parallel", "arbitrary"),
        name="diff_attn",
    )(q, k, v, lam_params, head_gain)


def kernel(x, a_norm, a_w_in, a_w_gate_up, a_b_gate, a_head_norm, a_w_out, kv_norm, w_kv, k_norm,
           b_norm, b_w_q, b_q_norm, b_lambda, b_head_norm, b_w_out, mlp_norm, mlp_w1, mlp_w2):
    batch, seq, d = x.shape
    bf = jnp.bfloat16
    row = lambda p: p.reshape(1, -1)
    h = x.reshape(batch * seq, d)

    n_main = 2 * GLA_HEADS * GLA_DK + 2 * GLA_HEADS * GLA_DV
    w_in = a_w_in[0]
    w_main = w_in[:, :n_main].astype(bf)
    w_z = jnp.pad(w_in[:, n_main:], ((0, 0), (0, LANES - GLA_RANK))).astype(bf)
    w_gate = jnp.pad(a_w_gate_up[0], ((0, LANES - GLA_RANK), (0, 0))).astype(bf)
    qkvg, z = _gla_inproj(h, row(a_norm[0]), w_main, w_z)
    og = _gla_core(qkvg, z, w_gate, row(a_b_gate[0]), row(a_head_norm[0]), batch, seq)
    h = _outproj_mlp(og, h, a_w_out[0].astype(bf), row(mlp_norm[0]),
                     mlp_w1[0].astype(bf), mlp_w2[0].astype(bf))

    layer = 1
    lambda_init = 0.8 - 0.6 * math.exp(-0.3 * layer)
    reps = LANES // DIFF_DH
    kn, v, qn = _diff_proj(h, row(kv_norm), row(b_norm[0]), w_kv.astype(bf), b_w_q[0].astype(bf),
                           row(jnp.tile(k_norm, reps)), row(jnp.tile(b_q_norm[0], reps)))
    oa = _diff_attn(qn, kn, v, b_lambda[0], row(b_head_norm[0]), batch, seq, lambda_init)
    h = _outproj_mlp(oa, h, b_w_out[0].astype(bf), row(mlp_norm[1]),
                     mlp_w1[1].astype(bf), mlp_w2[1].astype(bf))
    return h.reshape(batch, seq, d)
```

```python
import functools
import math

import numpy as np
import jax
import jax.numpy as jnp
from jax import lax
from jax.experimental import pallas as pl
from jax.experimental.pallas import tpu as pltpu

D_MODEL = 1024
GLA_HEADS = 4
GLA_DK = 128
GLA_DV = 256
GLA_RANK = 16
GLA_TAU = 16.0
GLA_CHUNK = 64
DIFF_HEADS = 8
DIFF_DH = 64
DIFF_VD = 128
MLP_HIDDEN = 4 * D_MODEL
NORM_EPS = 1e-6

LANES = 128
VMEM_LIMIT = 56 * 1024 * 1024
NEG_BIG = -1e30

ROW_TILE = 512
GLA_ROWS = 1024
SEG_TILE = 256
ATT_TQ = 1024
ATT_TK = 512
ATT_ROW_BLOCK = 32
GLA_LEVELS = (32, 16, 8, 4, 2, 1)
SUBLANES = 8
_GLA_N_VPU_LEVELS = sum(m >= SUBLANES for m in GLA_LEVELS)

_NT = (((1,), (1,)), ((), ()))
_TN = (((0,), (0,)), ((), ()))


def _dot(a, b, dims=None):
    if dims is None:
        return jnp.dot(a, b, preferred_element_type=jnp.float32)
    return lax.dot_general(a, b, dims, preferred_element_type=jnp.float32)


def _params(*sem):
    return pltpu.CompilerParams(dimension_semantics=sem, vmem_limit_bytes=VMEM_LIMIT)


def _resident(shape):
    nd = len(shape)
    return pl.BlockSpec(shape, lambda *_: (0,) * nd, pipeline_mode=pl.Buffered(1))


def _rms_scale(x):
    return lax.rsqrt(jnp.mean(x * x, axis=-1, keepdims=True) + NORM_EPS)


def _gla_inproj_kernel(x_ref, g_ref, w_ref, wz_ref, o_ref, z_ref):
    x = x_ref[...]
    xn = (x * _rms_scale(x) * g_ref[...]).astype(jnp.bfloat16)
    n_out = o_ref.shape[1]
    step = 768
    for n0 in range(0, n_out, step):
        w = w_ref[:, n0:n0 + step].astype(jnp.bfloat16)
        o_ref[:, n0:n0 + step] = _dot(xn, w).astype(o_ref.dtype)
    z_ref[...] = _dot(xn, wz_ref[...])


def _gla_inproj(x, gain, w_in, w_z, n):
    t, d = x.shape
    return pl.pallas_call(
        _gla_inproj_kernel,
        grid=(t // ROW_TILE,),
        in_specs=[
            pl.BlockSpec((ROW_TILE, d), lambda i: (i, 0)),
            _resident((1, d)),
            _layer_resident(w_in, 0),
            _resident((d, LANES)),
        ],
        out_specs=[
            pl.BlockSpec((ROW_TILE, n), lambda i: (i, 0)),
            pl.BlockSpec((ROW_TILE, LANES), lambda i: (i, 0)),
        ],
        out_shape=[
            jax.ShapeDtypeStruct((t, n), jnp.bfloat16),
            jax.ShapeDtypeStruct((t, LANES), jnp.float32),
        ],
        compiler_params=_params("parallel"),
        name="gla_inproj",
    )(x, gain, w_in, w_z)


def _gla_constants():
    c = GLA_CHUNK
    t = np.arange(c)
    row, col = t[:, None], t[None, :]
    mats = [col <= row]
    masks = []
    for m in GLA_LEVELS:
        blk = 2 * m
        start = (t // blk) * blk
        mid = (start + m)[:, None]
        second = ((t % blk) >= m)[:, None]
        q_side = second & (col > mid) & (col <= row)
        k_side = (~second) & (col > row) & (col <= mid)
        if m < SUBLANES:
            mats.append(q_side | k_side)
        masks.append((start[:, None] == start[None, :]) & second & (~second).T)
    masks.append(np.eye(c, dtype=bool))
    lm = np.concatenate(mats, axis=0).astype(np.float32)
    mk = np.stack(masks, axis=0).astype(np.float32)
    return lm, mk


def _gla_core_kernel(q_ref, k_ref, v_ref, g_ref, z_ref, wg_ref, bg_ref, hg_ref,
                     lm_ref, mk_ref, o_ref, st_scr):
    c = GLA_CHUNK
    rows = q_ref.shape[0]

    @pl.when(pl.program_id(2) == 0)
    def _():
        st_scr[...] = jnp.zeros_like(st_scr)

    logits = _dot(z_ref[...].astype(jnp.bfloat16), wg_ref[...]) + bg_ref[...]
    log_sig = jnp.minimum(logits, 0.0) - jnp.log1p(jnp.exp(-jnp.abs(logits)))
    la = log_sig * (math.log2(math.e) / GLA_TAU)
    la_hi = la.astype(jnp.bfloat16)
    la_lo = (la - la_hi.astype(jnp.float32)).astype(jnp.bfloat16)
    la_split = jnp.concatenate([la_hi, la_lo], axis=1)

    scale = GLA_DK ** -0.5
    gain = hg_ref[...]
    n = rows // c
    rs = [slice(ci * c, (ci + 1) * c) for ci in range(n)]
    n_lvl = len(GLA_LEVELS)

    sums = []
    for ci in range(n):
        sums2 = _dot(lm_ref[...], la_split[rs[ci]])
        sums.append(sums2[:, :GLA_DK] + sums2[:, GLA_DK:])
    cum = [s[0:c] for s in sums]

    def level_exponent(ci, li):
        m = GLA_LEVELS[li]
        if m < SUBLANES:
            lo = (1 + li - _GLA_N_VPU_LEVELS) * c
            return sums[ci][lo:lo + c]
        b = cum[ci]
        parts = []
        for start in range(0, c, 2 * m):
            mid = b[start + m:start + m + 1, :]
            parts.append(mid - b[start:start + m])
            parts.append(b[start + m:start + 2 * m] - mid)
        return jnp.concatenate(parts, axis=0)

    qf = [q_ref[r, :].astype(jnp.float32) for r in rs]
    kf = [k_ref[r, :].astype(jnp.float32) for r in rs]
    attn = [mk_ref[n_lvl] * _dot(q_ref[r, :], k_ref[r, :], _NT) for r in rs]
    for li in range(n_lvl):
        for ci in range(n):
            e = jnp.exp2(level_exponent(ci, li))
            ql = (qf[ci] * e).astype(jnp.bfloat16)
            kl = (kf[ci] * e).astype(jnp.bfloat16)
            attn[ci] = attn[ci] + mk_ref[li] * _dot(ql, kl, _NT)

    o_intra, qd, decay, upd = [], [], [], []
    for ci in range(n):
        b = cum[ci]
        b_last = b[c - 1:c, :]
        v = v_ref[rs[ci], :]
        kd = (kf[ci] * jnp.exp2(b_last - b)).astype(jnp.bfloat16)
        qd.append((qf[ci] * jnp.exp2(b)).astype(jnp.bfloat16))
        decay.append(jnp.exp2(b_last))
        upd.append(_dot(v, kd, _TN))
        o_intra.append(_dot(attn[ci].astype(jnp.bfloat16), v))

    st = st_scr[...]
    for ci in range(n):
        o = (o_intra[ci] + _dot(qd[ci], st.astype(jnp.bfloat16), _NT)) * scale
        st = st * decay[ci] + upd[ci]
        on = o * _rms_scale(o) * gain
        gate = g_ref[rs[ci], :].astype(jnp.float32)
        o_ref[rs[ci], :] = (on * (gate * jax.nn.sigmoid(gate))).astype(o_ref.dtype)
    st_scr[...] = st


def _gla_core(qkvg, z, w_gate, b_gate, head_gain, batch, seq):
    t = qkvg.shape[0]
    nblk = seq // GLA_ROWS
    lm, mk = _gla_constants()
    lm = jnp.asarray(lm, jnp.bfloat16)
    mk = jnp.asarray(mk, jnp.float32)
    kq = (GLA_HEADS * GLA_DK) // GLA_DK
    kv = (2 * GLA_HEADS * GLA_DK) // GLA_DV
    kg = kv + GLA_HEADS
    row = lambda b, h, s: b * nblk + s
    return pl.pallas_call(
        _gla_core_kernel,
        grid=(batch, GLA_HEADS, nblk),
        in_specs=[
            pl.BlockSpec((GLA_ROWS, GLA_DK), lambda b, h, s: (row(b, h, s), h)),
            pl.BlockSpec((GLA_ROWS, GLA_DK), lambda b, h, s: (row(b, h, s), kq + h)),
            pl.BlockSpec((GLA_ROWS, GLA_DV), lambda b, h, s: (row(b, h, s), kv + h)),
            pl.BlockSpec((GLA_ROWS, GLA_DV), lambda b, h, s: (row(b, h, s), kg + h)),
            pl.BlockSpec((GLA_ROWS, LANES), lambda b, h, s: (row(b, h, s), 0)),
            pl.BlockSpec((LANES, GLA_DK), lambda b, h, s: (0, h)),
            pl.BlockSpec((1, GLA_DK), lambda b, h, s: (0, h)),
            pl.BlockSpec((1, GLA_DV), lambda b, h, s: (0, 0)),
            pl.BlockSpec(lm.shape, lambda b, h, s: (0, 0)),
            pl.BlockSpec(mk.shape, lambda b, h, s: (0, 0, 0)),
        ],
        out_specs=pl.BlockSpec((GLA_ROWS, GLA_DV), lambda b, h, s: (row(b, h, s), h)),
        out_shape=jax.ShapeDtypeStruct((t, GLA_HEADS * GLA_DV), jnp.bfloat16),
        scratch_shapes=[pltpu.VMEM((GLA_DV, GLA_DK), jnp.float32)],
        compiler_params=_params("parallel", "parallel", "arbitrary"),
        name="gla_core",
    )(qkvg, qkvg, qkvg, qkvg, z, w_gate, b_gate, head_gain, lm, mk)


def _outproj_mlp_kernel(a_ref, h_ref, wo_ref, g_ref, w1_ref, w2_ref, o_ref):
    bf = jnp.bfloat16
    h1 = h_ref[...] + _dot(a_ref[...], wo_ref[...].astype(bf))
    xn = (h1 * _rms_scale(h1) * g_ref[...]).astype(bf)
    acc = h1
    d = h1.shape[1]
    for c0 in range(0, w1_ref.shape[1], d):
        hid = jnp.maximum(_dot(xn, w1_ref[:, c0:c0 + d].astype(bf)), 0.0)
        acc = acc + _dot((hid * hid).astype(bf), w2_ref[c0:c0 + d, :].astype(bf))
    o_ref[...] = acc


def _layer_resident(w, layer):
    return pl.BlockSpec((None,) + w.shape[1:], lambda *_: (layer, 0, 0),
                        pipeline_mode=pl.Buffered(1))


def _outproj_mlp(a, h, w_out, gain, w1, w2, layer_out, layer_mlp):
    t, d = h.shape
    tile = pl.BlockSpec((ROW_TILE, d), lambda i: (i, 0))
    return pl.pallas_call(
        _outproj_mlp_kernel,
        grid=(t // ROW_TILE,),
        in_specs=[tile, tile, _layer_resident(w_out, layer_out), _resident((1, d)),
                  _layer_resident(w1, layer_mlp), _layer_resident(w2, layer_mlp)],
        out_specs=tile,
        out_shape=jax.ShapeDtypeStruct((t, d), jnp.float32),
        compiler_params=_params("parallel"),
        name="outproj_mlp",
    )(a, h, w_out, gain, w1, w2)


def _segment_norm(y, seg_ones, gain):
    outs = []
    for c0 in range(0, y.shape[1], SEG_TILE):
        yc = y[:, c0:c0 + SEG_TILE]
        ss = _dot((yc * yc).astype(jnp.bfloat16), seg_ones)
        outs.append(yc * lax.rsqrt(ss * (1.0 / DIFF_DH) + NORM_EPS) * gain)
    return outs


def _diff_proj_kernel(h_ref, gkv_ref, gq_ref, wkv_ref, wq_ref, kg_ref, qg_ref, so_ref,
                      k_ref, v_ref, q_ref):
    x = h_ref[...]
    xhat = x * _rms_scale(x)
    xkv = (xhat * gkv_ref[...]).astype(jnp.bfloat16)
    xq = (xhat * gq_ref[...]).astype(jnp.bfloat16)
    nk = k_ref.shape[1]
    seg_ones = so_ref[...]
    bf = jnp.bfloat16
    kraw = _dot(xkv, wkv_ref[:, :nk].astype(bf))
    for i, blk in enumerate(_segment_norm(kraw, seg_ones, kg_ref[...])):
        k_ref[:, i * SEG_TILE:(i + 1) * SEG_TILE] = blk.astype(k_ref.dtype)
    v_ref[...] = _dot(xkv, wkv_ref[:, nk:].astype(bf)).astype(v_ref.dtype)
    qraw = _dot(xq, wq_ref[...].astype(bf))
    qscale = DIFF_DH ** -0.5 * math.log2(math.e)
    for i, blk in enumerate(_segment_norm(qraw, seg_ones, qg_ref[...])):
        q_ref[:, i * SEG_TILE:(i + 1) * SEG_TILE] = (blk * qscale).astype(q_ref.dtype)


def _diff_proj(h, g_kv, g_q, w_kv, w_q, k_gain, q_gain):
    t, d = h.shape
    nk = DIFF_HEADS * 2 * DIFF_DH
    nv = DIFF_HEADS * DIFF_VD
    lane = np.arange(SEG_TILE)
    seg_ones = jnp.asarray(lane[:, None] // DIFF_DH == lane[None, :] // DIFF_DH, jnp.bfloat16)
    tile = lambda n: pl.BlockSpec((ROW_TILE, n), lambda i: (i, 0))
    return pl.pallas_call(
        _diff_proj_kernel,
        grid=(t // ROW_TILE,),
        in_specs=[tile(d), _resident((1, d)), _resident((1, d)), _resident(w_kv.shape),
                  _layer_resident(w_q, 0), _resident((1, SEG_TILE)), _resident((1, SEG_TILE)),
                  _resident((SEG_TILE, SEG_TILE))],
        out_specs=[tile(nk), tile(nv), tile(nk)],
        out_shape=[jax.ShapeDtypeStruct((t, nk), jnp.bfloat16),
                   jax.ShapeDtypeStruct((t, nv), jnp.bfloat16),
                   jax.ShapeDtypeStruct((t, nk), jnp.bfloat16)],
        compiler_params=_params("parallel"),
        name="diff_proj",
    )(h, g_kv, g_q, w_kv, w_q, k_gain, q_gain, seg_ones)


def _diff_attn_kernel(q_ref, k_ref, v_ref, lam_ref, hg_ref, o_ref,
                      m_scr, l_scr, acc_scr, s_bufs, p_bufs, *, lambda_init):
    qi = pl.program_id(2)
    tq = q_ref.shape[0]
    tk = ATT_TK

    m_scr[...] = jnp.full_like(m_scr, NEG_BIG)
    l_scr[...] = jnp.zeros_like(l_scr)
    acc_scr[...] = jnp.zeros_like(acc_scr)

    q = q_ref[...]
    lane = lax.broadcasted_iota(jnp.int32, q.shape, 1)
    qcs = [jnp.where((lane // DIFF_DH) == c, q, jnp.zeros_like(q)) for c in range(2)]

    def block(j):
        return pl.ds(pl.multiple_of(j * tk, tk), tk)

    rb = ATT_ROW_BLOCK
    col_minus_row = (lax.broadcasted_iota(jnp.int32, (rb, tk), 1)
                     - lax.broadcasted_iota(jnp.int32, (rb, tk), 0))

    def step(j, buf, row_lo, diag_row0):
        k = k_ref[block(j), :]
        v = v_ref[block(j), :]
        s_scr = s_bufs.at[buf]
        p_scr = p_bufs.at[buf]
        all_rows = slice(row_lo, tq)
        for c in range(2):
            s_scr[c, all_rows, :] = _dot(qcs[c][all_rows], k, _NT)
        for c in range(2):
            for r0 in range(row_lo, tq, rb):
                rows = slice(r0, r0 + rb)
                masked = diag_row0 is not None and r0 - diag_row0 < tk
                kmax = min(tk, -(-(r0 - diag_row0 + rb) // LANES) * LANES) if masked else tk
                s = s_scr[c, rows, 0:kmax]
                if masked:
                    s = jnp.where(col_minus_row[:, 0:kmax] <= r0 - diag_row0, s, NEG_BIG)
                m_prev = m_scr[c, rows, :]
                m_new = jnp.maximum(m_prev, jnp.max(s, axis=-1, keepdims=True))
                alpha = jnp.exp2(m_prev - m_new)
                p = jnp.exp2((s - jnp.tile(m_new, (1, kmax // LANES))).astype(jnp.bfloat16))
                psum = p[:, 0:LANES]
                for t0 in range(LANES, kmax, LANES):
                    psum = psum + p[:, t0:t0 + LANES]
                l_scr[c, rows, :] = alpha * l_scr[c, rows, :] + psum.astype(jnp.float32)
                m_scr[c, rows, :] = m_new
                acc_scr[c, rows, :] = alpha * acc_scr[c, rows, :]
                p_scr[c, rows, 0:kmax] = p
                if kmax < tk:
                    p_scr[c, rows, kmax:tk] = jnp.zeros((rb, tk - kmax), jnp.bfloat16)
            acc_scr[c, all_rows, :] += _dot(p_scr[c, all_rows, :], v)

    n_sub = tq // tk

    def pair(i, carry):
        step(2 * i, 0, 0, None)
        step(2 * i + 1, 1, 0, None)
        return carry

    lax.fori_loop(0, (n_sub // 2) * qi, pair, 0)
    for t in range(n_sub):
        step(n_sub * qi + t, t % 2, t * tk, t * tk)

    lp = lam_ref[...]
    lam = (jnp.exp(jnp.sum(lp[0:1] * lp[1:2], axis=-1, keepdims=True))
           - jnp.exp(jnp.sum(lp[2:3] * lp[3:4], axis=-1, keepdims=True)) + lambda_init)
    l0 = jnp.sum(l_scr[0], axis=-1, keepdims=True)
    l1 = jnp.sum(l_scr[1], axis=-1, keepdims=True)
    o = acc_scr[0] / l0 - lam * (acc_scr[1] / l1)
    on = o * _rms_scale(o) * hg_ref[...]
    o_ref[...] = (on * (1.0 - lambda_init)).astype(o_ref.dtype)


def _diff_attn(q, k, v, lam_params, head_gain, batch, seq, lambda_init):
    t = q.shape[0]
    assert ATT_TQ % (2 * ATT_TK) == 0
    nq = seq // ATT_TQ
    kern = functools.partial(_diff_attn_kernel, lambda_init=lambda_init)
    return pl.pallas_call(
        kern,
        grid=(batch, DIFF_HEADS, nq),
        in_specs=[
            pl.BlockSpec((ATT_TQ, 2 * DIFF_DH), lambda b, h, i: (b * nq + i, h)),
            pl.BlockSpec((seq, 2 * DIFF_DH), lambda b, h, i: (b, h)),
            pl.BlockSpec((seq, DIFF_VD), lambda b, h, i: (b, h)),
            pl.BlockSpec(lam_params.shape, lambda b, h, i: (0, 0)),
            pl.BlockSpec((1, DIFF_VD), lambda b, h, i: (0, 0)),
        ],
        out_specs=pl.BlockSpec((ATT_TQ, DIFF_VD), lambda b, h, i: (b * nq + i, h)),
        out_shape=jax.ShapeDtypeStruct((t, DIFF_HEADS * DIFF_VD), jnp.bfloat16),
        scratch_shapes=[
            pltpu.VMEM((2, ATT_TQ, LANES), jnp.float32),
            pltpu.VMEM((2, ATT_TQ, LANES), jnp.float32),
            pltpu.VMEM((2, ATT_TQ, DIFF_VD), jnp.float32),
            pltpu.VMEM((2, 2, ATT_TQ, ATT_TK), jnp.float32),
            pltpu.VMEM((2, 2, ATT_TQ, ATT_TK), jnp.bfloat16),
        ],
        compiler_params=_params("parallel", "parallel", "arbitrary"),
        name="diff_attn",
    )(q, k, v, lam_params, head_gain)


def kernel(x, a_norm, a_w_in, a_w_gate_up, a_b_gate, a_head_norm, a_w_out, kv_norm, w_kv, k_norm,
           b_norm, b_w_q, b_q_norm, b_lambda, b_head_norm, b_w_out, mlp_norm, mlp_w1, mlp_w2):
    batch, seq, d = x.shape
    bf = jnp.bfloat16
    row = lambda p: p.reshape(1, -1)
    h = x.reshape(batch * seq, d)

    n_main = 2 * GLA_HEADS * GLA_DK + 2 * GLA_HEADS * GLA_DV
    w_z = jnp.pad(a_w_in[0, :, n_main:], ((0, 0), (0, LANES - GLA_RANK))).astype(bf)
    w_gate = jnp.pad(a_w_gate_up[0], ((0, LANES - GLA_RANK), (0, 0))).astype(bf)
    qkvg, z = _gla_inproj(h, row(a_norm[0]), a_w_in, w_z, n_main)
    og = _gla_core(qkvg, z, w_gate, row(a_b_gate[0]), row(a_head_norm[0]), batch, seq)
    h = _outproj_mlp(og, h, a_w_out, row(mlp_norm[0]), mlp_w1, mlp_w2, 0, 0)

    layer = 1
    lambda_init = 0.8 - 0.6 * math.exp(-0.3 * layer)
    reps = SEG_TILE // DIFF_DH
    kn, v, qn = _diff_proj(h, row(kv_norm), row(b_norm[0]), w_kv, b_w_q,
                           row(jnp.tile(k_norm, reps)), row(jnp.tile(b_q_norm[0], reps)))
    oa = _diff_attn(qn, kn, v, b_lambda[0], row(b_head_norm[0]), batch, seq, lambda_init)
    h = _outproj_mlp(oa, h, b_w_out, row(mlp_norm[1]), mlp_w1, mlp_w2, 0, 1)
    return h.reshape(batch, seq, d)
```

```python
import functools
import math

import numpy as np
import jax
import jax.numpy as jnp
from jax import lax
from jax.experimental import pallas as pl
from jax.experimental.pallas import tpu as pltpu

D_MODEL = 1024
GLA_HEADS = 4
GLA_DK = 128
GLA_DV = 256
GLA_RANK = 16
GLA_TAU = 16.0
GLA_CHUNK = 64
DIFF_HEADS = 8
DIFF_DH = 64
DIFF_VD = 128
MLP_HIDDEN = 4 * D_MODEL
NORM_EPS = 1e-6

LANES = 128
VMEM_LIMIT = 56 * 1024 * 1024
NEG_BIG = -1e30

ROW_TILE = 512
GLA_ROWS = 1024
SEG_TILE = 256
ATT_TQ = 1024
ATT_TK = 512
ATT_ROW_BLOCK = 32
GLA_LEVELS = (32, 16, 8, 4, 2, 1)
SUBLANES = 8
_GLA_N_VPU_LEVELS = sum(m >= SUBLANES for m in GLA_LEVELS)

_NT = (((1,), (1,)), ((), ()))
_TN = (((0,), (0,)), ((), ()))


def _dot(a, b, dims=None):
    if dims is None:
        return jnp.dot(a, b, preferred_element_type=jnp.float32)
    return lax.dot_general(a, b, dims, preferred_element_type=jnp.float32)


def _params(*sem):
    return pltpu.CompilerParams(dimension_semantics=sem, vmem_limit_bytes=VMEM_LIMIT)


def _resident(shape):
    nd = len(shape)
    return pl.BlockSpec(shape, lambda *_: (0,) * nd, pipeline_mode=pl.Buffered(1))


def _rms_scale(x):
    return lax.rsqrt(jnp.mean(x * x, axis=-1, keepdims=True) + NORM_EPS)


def _gla_inproj_kernel(x_ref, g_ref, w_ref, wz_ref, o_ref, z_ref):
    x = x_ref[...]
    xn = (x * _rms_scale(x) * g_ref[...]).astype(jnp.bfloat16)
    n_out = o_ref.shape[1]
    step = 768
    for n0 in range(0, n_out, step):
        w = w_ref[:, n0:n0 + step].astype(jnp.bfloat16)
        o_ref[:, n0:n0 + step] = _dot(xn, w).astype(o_ref.dtype)
    z_ref[...] = _dot(xn, wz_ref[...])


def _gla_inproj(x, gain, w_in, w_z, n):
    t, d = x.shape
    return pl.pallas_call(
        _gla_inproj_kernel,
        grid=(t // ROW_TILE,),
        in_specs=[
            pl.BlockSpec((ROW_TILE, d), lambda i: (i, 0)),
            _resident((1, d)),
            _layer_resident(w_in, 0),
            _resident((d, LANES)),
        ],
        out_specs=[
            pl.BlockSpec((ROW_TILE, n), lambda i: (i, 0)),
            pl.BlockSpec((ROW_TILE, LANES), lambda i: (i, 0)),
        ],
        out_shape=[
            jax.ShapeDtypeStruct((t, n), jnp.bfloat16),
            jax.ShapeDtypeStruct((t, LANES), jnp.float32),
        ],
        compiler_params=_params("parallel"),
        name="gla_inproj",
    )(x, gain, w_in, w_z)


def _gla_constants():
    c = GLA_CHUNK
    t = np.arange(c)
    row, col = t[:, None], t[None, :]
    mats = [col <= row]
    masks = []
    for m in GLA_LEVELS:
        blk = 2 * m
        start = (t // blk) * blk
        mid = (start + m)[:, None]
        second = ((t % blk) >= m)[:, None]
        q_side = second & (col > mid) & (col <= row)
        k_side = (~second) & (col > row) & (col <= mid)
        if m < SUBLANES:
            mats.append(q_side | k_side)
        masks.append((start[:, None] == start[None, :]) & second & (~second).T)
    masks.append(np.eye(c, dtype=bool))
    lm = np.concatenate(mats, axis=0).astype(np.float32)
    mk = np.stack(masks, axis=0).astype(np.float32)
    return lm, mk


def _gla_core_kernel(q_ref, k_ref, v_ref, g_ref, z_ref, wg_ref, bg_ref, hg_ref,
                     lm_ref, mk_ref, o_ref, st_scr):
    c = GLA_CHUNK
    rows = q_ref.shape[0]

    @pl.when(pl.program_id(2) == 0)
    def _():
        st_scr[...] = jnp.zeros_like(st_scr)

    logits = _dot(z_ref[...].astype(jnp.bfloat16), wg_ref[...]) + bg_ref[...]
    log_sig = jnp.minimum(logits, 0.0) - jnp.log1p(jnp.exp(-jnp.abs(logits)))
    la = log_sig * (math.log2(math.e) / GLA_TAU)
    la_hi = la.astype(jnp.bfloat16)
    la_lo = (la - la_hi.astype(jnp.float32)).astype(jnp.bfloat16)

    scale = GLA_DK ** -0.5
    gain = hg_ref[...]
    n = rows // c
    rs = [slice(ci * c, (ci + 1) * c) for ci in range(n)]
    n_lvl = len(GLA_LEVELS)

    sums = []
    for ci in range(n):
        la_stack = jnp.concatenate([la_hi[rs[ci]], la_lo[rs[ci]]], axis=0)
        sums.append(_dot(lm_ref[...], la_stack))
    cum = [s[0:c] for s in sums]

    def level_exponent(ci, li):
        m = GLA_LEVELS[li]
        if m < SUBLANES:
            lo = (1 + li - _GLA_N_VPU_LEVELS) * c
            return sums[ci][lo:lo + c]
        b = cum[ci]
        parts = []
        for start in range(0, c, 2 * m):
            mid = b[start + m:start + m + 1, :]
            parts.append(mid - b[start:start + m])
            parts.append(b[start + m:start + 2 * m] - mid)
        return jnp.concatenate(parts, axis=0)

    qf = [q_ref[r, :].astype(jnp.float32) for r in rs]
    kf = [k_ref[r, :].astype(jnp.float32) for r in rs]
    attn = [mk_ref[n_lvl] * _dot(q_ref[r, :], k_ref[r, :], _NT) for r in rs]
    for li in range(n_lvl):
        for ci in range(n):
            e = jnp.exp2(level_exponent(ci, li))
            ql = (qf[ci] * e).astype(jnp.bfloat16)
            kl = (kf[ci] * e).astype(jnp.bfloat16)
            attn[ci] = attn[ci] + mk_ref[li] * _dot(ql, kl, _NT)

    o_intra, qd, decay, upd = [], [], [], []
    for ci in range(n):
        b = cum[ci]
        b_last = b[c - 1:c, :]
        v = v_ref[rs[ci], :]
        kd = (kf[ci] * jnp.exp2(b_last - b)).astype(jnp.bfloat16)
        qd.append((qf[ci] * jnp.exp2(b)).astype(jnp.bfloat16))
        decay.append(jnp.exp2(b_last))
        upd.append(_dot(v, kd, _TN))
        o_intra.append(_dot(attn[ci].astype(jnp.bfloat16), v))

    st = st_scr[...]
    for ci in range(n):
        o = (o_intra[ci] + _dot(qd[ci], st.astype(jnp.bfloat16), _NT)) * scale
        st = st * decay[ci] + upd[ci]
        on = o * _rms_scale(o) * gain
        gate = g_ref[rs[ci], :].astype(jnp.float32)
        o_ref[rs[ci], :] = (on * (gate * jax.nn.sigmoid(gate))).astype(o_ref.dtype)
    st_scr[...] = st


def _gla_core(qkvg, z, w_gate, b_gate, head_gain, batch, seq):
    t = qkvg.shape[0]
    nblk = seq // GLA_ROWS
    lm, mk = _gla_constants()
    lm = jnp.asarray(np.concatenate([lm, lm], axis=1), jnp.bfloat16)
    mk = jnp.asarray(mk, jnp.float32)
    kq = (GLA_HEADS * GLA_DK) // GLA_DK
    kv = (2 * GLA_HEADS * GLA_DK) // GLA_DV
    kg = kv + GLA_HEADS
    row = lambda b, h, s: b * nblk + s
    return pl.pallas_call(
        _gla_core_kernel,
        grid=(batch, GLA_HEADS, nblk),
        in_specs=[
            pl.BlockSpec((GLA_ROWS, GLA_DK), lambda b, h, s: (row(b, h, s), h)),
            pl.BlockSpec((GLA_ROWS, GLA_DK), lambda b, h, s: (row(b, h, s), kq + h)),
            pl.BlockSpec((GLA_ROWS, GLA_DV), lambda b, h, s: (row(b, h, s), kv + h)),
            pl.BlockSpec((GLA_ROWS, GLA_DV), lambda b, h, s: (row(b, h, s), kg + h)),
            pl.BlockSpec((GLA_ROWS, LANES), lambda b, h, s: (row(b, h, s), 0)),
            pl.BlockSpec((LANES, GLA_DK), lambda b, h, s: (0, h)),
            pl.BlockSpec((1, GLA_DK), lambda b, h, s: (0, h)),
            pl.BlockSpec((1, GLA_DV), lambda b, h, s: (0, 0)),
            pl.BlockSpec(lm.shape, lambda b, h, s: (0, 0)),
            pl.BlockSpec(mk.shape, lambda b, h, s: (0, 0, 0)),
        ],
        out_specs=pl.BlockSpec((GLA_ROWS, GLA_DV), lambda b, h, s: (row(b, h, s), h)),
        out_shape=jax.ShapeDtypeStruct((t, GLA_HEADS * GLA_DV), jnp.bfloat16),
        scratch_shapes=[pltpu.VMEM((GLA_DV, GLA_DK), jnp.float32)],
        compiler_params=_params("parallel", "parallel", "arbitrary"),
        name="gla_core",
    )(qkvg, qkvg, qkvg, qkvg, z, w_gate, b_gate, head_gain, lm, mk)


def _outproj_mlp_kernel(a_ref, h_ref, wo_ref, g_ref, w1_ref, w2_ref, o_ref):
    bf = jnp.bfloat16
    h1 = h_ref[...] + _dot(a_ref[...], wo_ref[...].astype(bf))
    xn = (h1 * _rms_scale(h1) * g_ref[...]).astype(bf)
    acc = h1
    d = h1.shape[1]
    for c0 in range(0, w1_ref.shape[1], d):
        hid = jnp.maximum(_dot(xn, w1_ref[:, c0:c0 + d].astype(bf)), 0.0)
        acc = acc + _dot((hid * hid).astype(bf), w2_ref[c0:c0 + d, :].astype(bf))
    o_ref[...] = acc


def _layer_resident(w, layer):
    return pl.BlockSpec((None,) + w.shape[1:], lambda *_: (layer, 0, 0),
                        pipeline_mode=pl.Buffered(1))


def _outproj_mlp(a, h, w_out, gain, w1, w2, layer_out, layer_mlp):
    t, d = h.shape
    tile = pl.BlockSpec((ROW_TILE, d), lambda i: (i, 0))
    return pl.pallas_call(
        _outproj_mlp_kernel,
        grid=(t // ROW_TILE,),
        in_specs=[tile, tile, _layer_resident(w_out, layer_out), _resident((1, d)),
                  _layer_resident(w1, layer_mlp), _layer_resident(w2, layer_mlp)],
        out_specs=tile,
        out_shape=jax.ShapeDtypeStruct((t, d), jnp.float32),
        compiler_params=_params("parallel"),
        name="outproj_mlp",
    )(a, h, w_out, gain, w1, w2)


def _segment_norm(y, seg_ones, gain):
    outs = []
    for c0 in range(0, y.shape[1], SEG_TILE):
        yc = y[:, c0:c0 + SEG_TILE]
        ss = _dot((yc * yc).astype(jnp.bfloat16), seg_ones)
        outs.append(yc * lax.rsqrt(ss * (1.0 / DIFF_DH) + NORM_EPS) * gain)
    return outs


def _diff_proj_kernel(h_ref, gkv_ref, gq_ref, wkv_ref, wq_ref, kg_ref, qg_ref, so_ref,
                      k_ref, v_ref, q_ref):
    x = h_ref[...]
    xhat = x * _rms_scale(x)
    xkv = (xhat * gkv_ref[...]).astype(jnp.bfloat16)
    xq = (xhat * gq_ref[...]).astype(jnp.bfloat16)
    nk = k_ref.shape[1]
    seg_ones = so_ref[...]
    bf = jnp.bfloat16
    kraw = _dot(xkv, wkv_ref[:, :nk].astype(bf))
    for i, blk in enumerate(_segment_norm(kraw, seg_ones, kg_ref[...])):
        k_ref[:, i * SEG_TILE:(i + 1) * SEG_TILE] = blk.astype(k_ref.dtype)
    v_ref[...] = _dot(xkv, wkv_ref[:, nk:].astype(bf)).astype(v_ref.dtype)
    qraw = _dot(xq, wq_ref[...].astype(bf))
    qscale = DIFF_DH ** -0.5 * math.log2(math.e)
    for i, blk in enumerate(_segment_norm(qraw, seg_ones, qg_ref[...])):
        q_ref[:, i * SEG_TILE:(i + 1) * SEG_TILE] = (blk * qscale).astype(q_ref.dtype)


def _diff_proj(h, g_kv, g_q, w_kv, w_q, k_gain, q_gain):
    t, d = h.shape
    nk = DIFF_HEADS * 2 * DIFF_DH
    nv = DIFF_HEADS * DIFF_VD
    lane = np.arange(SEG_TILE)
    seg_ones = jnp.asarray(lane[:, None] // DIFF_DH == lane[None, :] // DIFF_DH, jnp.bfloat16)
    tile = lambda n: pl.BlockSpec((ROW_TILE, n), lambda i: (i, 0))
    return pl.pallas_call(
        _diff_proj_kernel,
        grid=(t // ROW_TILE,),
        in_specs=[tile(d), _resident((1, d)), _resident((1, d)), _resident(w_kv.shape),
                  _layer_resident(w_q, 0), _resident((1, SEG_TILE)), _resident((1, SEG_TILE)),
                  _resident((SEG_TILE, SEG_TILE))],
        out_specs=[tile(nk), tile(nv), tile(nk)],
        out_shape=[jax.ShapeDtypeStruct((t, nk), jnp.bfloat16),
                   jax.ShapeDtypeStruct((t, nv), jnp.bfloat16),
                   jax.ShapeDtypeStruct((t, nk), jnp.bfloat16)],
        compiler_params=_params("parallel"),
        name="diff_proj",
    )(h, g_kv, g_q, w_kv, w_q, k_gain, q_gain, seg_ones)


def _diff_attn_kernel(q_ref, k_ref, v_ref, lam_ref, hg_ref, o_ref,
                      m_scr, l_scr, acc_scr, s_bufs, p_bufs, *, lambda_init):
    qi = pl.program_id(2)
    tq = q_ref.shape[0]
    tk = ATT_TK

    q = q_ref[...]
    lane = lax.broadcasted_iota(jnp.int32, q.shape, 1)
    qcs = [jnp.where((lane // DIFF_DH) == c, q, jnp.zeros_like(q)) for c in range(2)]

    def block(j):
        return pl.ds(pl.multiple_of(j * tk, tk), tk)

    rb = ATT_ROW_BLOCK
    col_minus_row = (lax.broadcasted_iota(jnp.int32, (rb, tk), 1)
                     - lax.broadcasted_iota(jnp.int32, (rb, tk), 0))

    def step(j, buf, row_lo, diag_row0, first=False):
        k = k_ref[block(j), :]
        v = v_ref[block(j), :]
        s_scr = s_bufs.at[buf]
        p_scr = p_bufs.at[buf]
        all_rows = slice(row_lo, tq)
        for c in range(2):
            s_scr[c, all_rows, :] = _dot(qcs[c][all_rows], k, _NT)
        for c in range(2):
            for r0 in range(row_lo, tq, rb):
                rows = slice(r0, r0 + rb)
                masked = diag_row0 is not None and r0 - diag_row0 < tk
                kmax = min(tk, -(-(r0 - diag_row0 + rb) // LANES) * LANES) if masked else tk
                s = s_scr[c, rows, 0:kmax]
                if masked:
                    s = jnp.where(col_minus_row[:, 0:kmax] <= r0 - diag_row0, s, NEG_BIG)
                m_cur = jnp.max(s, axis=-1, keepdims=True)
                if first:
                    m_new = jnp.broadcast_to(m_cur, (rb, LANES))
                else:
                    m_prev = m_scr[c, rows, :]
                    m_new = jnp.maximum(m_prev, m_cur)
                    alpha = jnp.exp2(m_prev - m_new)
                p = jnp.exp2(s - jnp.tile(m_new, (1, kmax // LANES)))
                psum = p[:, 0:LANES]
                for t0 in range(LANES, kmax, LANES):
                    psum = psum + p[:, t0:t0 + LANES]
                if first:
                    l_scr[c, rows, :] = psum
                else:
                    l_scr[c, rows, :] = alpha * l_scr[c, rows, :] + psum
                    acc_scr[c, rows, :] = alpha * acc_scr[c, rows, :]
                m_scr[c, rows, :] = m_new
                p_scr[c, rows, 0:kmax] = p.astype(jnp.bfloat16)
                if kmax < tk:
                    p_scr[c, rows, kmax:tk] = jnp.zeros((rb, tk - kmax), jnp.bfloat16)
            pv = _dot(p_scr[c, all_rows, :], v)
            if first:
                acc_scr[c, all_rows, :] = pv
            else:
                acc_scr[c, all_rows, :] += pv

    n_sub = tq // tk
    step(n_sub * qi, 0, 0, 0, first=True)

    def pair(i, carry):
        step(2 * i, 0, 0, None)
        step(2 * i + 1, 1, 0, None)
        return carry

    lax.fori_loop(0, (n_sub // 2) * qi, pair, 0)
    for t in range(1, n_sub):
        step(n_sub * qi + t, t % 2, t * tk, t * tk)

    lp = lam_ref[...]
    lam = (jnp.exp(jnp.sum(lp[0:1] * lp[1:2], axis=-1, keepdims=True))
           - jnp.exp(jnp.sum(lp[2:3] * lp[3:4], axis=-1, keepdims=True)) + lambda_init)
    l0 = jnp.sum(l_scr[0], axis=-1, keepdims=True)
    l1 = jnp.sum(l_scr[1], axis=-1, keepdims=True)
    o = acc_scr[0] / l0 - lam * (acc_scr[1] / l1)
    on = o * _rms_scale(o) * hg_ref[...]
    o_ref[...] = (on * (1.0 - lambda_init)).astype(o_ref.dtype)


def _diff_attn(q, k, v, lam_params, head_gain, batch, seq, lambda_init):
    t = q.shape[0]
    assert ATT_TQ % (2 * ATT_TK) == 0
    nq = seq // ATT_TQ
    kern = functools.partial(_diff_attn_kernel, lambda_init=lambda_init)
    return pl.pallas_call(
        kern,
        grid=(batch, DIFF_HEADS, nq),
        in_specs=[
            pl.BlockSpec((ATT_TQ, 2 * DIFF_DH), lambda b, h, i: (b * nq + i, h)),
            pl.BlockSpec((seq, 2 * DIFF_DH), lambda b, h, i: (b, h)),
            pl.BlockSpec((seq, DIFF_VD), lambda b, h, i: (b, h)),
            pl.BlockSpec(lam_params.shape, lambda b, h, i: (0, 0)),
            pl.BlockSpec((1, DIFF_VD), lambda b, h, i: (0, 0)),
        ],
        out_specs=pl.BlockSpec((ATT_TQ, DIFF_VD), lambda b, h, i: (b * nq + i, h)),
        out_shape=jax.ShapeDtypeStruct((t, DIFF_HEADS * DIFF_VD), jnp.bfloat16),
        scratch_shapes=[
            pltpu.VMEM((2, ATT_TQ, LANES), jnp.float32),
            pltpu.VMEM((2, ATT_TQ, LANES), jnp.float32),
            pltpu.VMEM((2, ATT_TQ, DIFF_VD), jnp.float32),
            pltpu.VMEM((2, 2, ATT_TQ, ATT_TK), jnp.float32),
            pltpu.VMEM((2, 2, ATT_TQ, ATT_TK), jnp.bfloat16),
        ],
        compiler_params=_params("parallel", "parallel", "arbitrary"),
        name="diff_attn",
    )(q, k, v, lam_params, head_gain)


def kernel(x, a_norm, a_w_in, a_w_gate_up, a_b_gate, a_head_norm, a_w_out, kv_norm, w_kv, k_norm,
           b_norm, b_w_q, b_q_norm, b_lambda, b_head_norm, b_w_out, mlp_norm, mlp_w1, mlp_w2):
    batch, seq, d = x.shape
    bf = jnp.bfloat16
    row = lambda p: p.reshape(1, -1)
    h = x.reshape(batch * seq, d)

    n_main = 2 * GLA_HEADS * GLA_DK + 2 * GLA_HEADS * GLA_DV
    w_z = jnp.pad(a_w_in[0, :, n_main:], ((0, 0), (0, LANES - GLA_RANK))).astype(bf)
    w_gate = jnp.pad(a_w_gate_up[0], ((0, LANES - GLA_RANK), (0, 0))).astype(bf)
    qkvg, z = _gla_inproj(h, row(a_norm[0]), a_w_in, w_z, n_main)
    og = _gla_core(qkvg, z, w_gate, row(a_b_gate[0]), row(a_head_norm[0]), batch, seq)
    h = _outproj_mlp(og, h, a_w_out, row(mlp_norm[0]), mlp_w1, mlp_w2, 0, 0)

    layer = 1
    lambda_init = 0.8 - 0.6 * math.exp(-0.3 * layer)
    reps = SEG_TILE // DIFF_DH
    kn, v, qn = _diff_proj(h, row(kv_norm), row(b_norm[0]), w_kv, b_w_q,
                           row(jnp.tile(k_norm, reps)), row(jnp.tile(b_q_norm[0], reps)))
    oa = _diff_attn(qn, kn, v, b_lambda[0], row(b_head_norm[0]), batch, seq, lambda_init)
    h = _outproj_mlp(oa, h, b_w_out, row(mlp_norm[1]), mlp_w1, mlp_w2, 0, 1)
    return h.reshape(batch, seq, d)
```

```python
import functools
import math

import numpy as np
import jax
import jax.numpy as jnp
from jax import lax
from jax.experimental import pallas as pl
from jax.experimental.pallas import tpu as pltpu

D_MODEL = 1024
GLA_HEADS = 4
GLA_DK = 128
GLA_DV = 256
GLA_RANK = 16
GLA_TAU = 16.0
GLA_CHUNK = 64
DIFF_HEADS = 8
DIFF_DH = 64
DIFF_VD = 128
MLP_HIDDEN = 4 * D_MODEL
NORM_EPS = 1e-6

LANES = 128
VMEM_LIMIT = 56 * 1024 * 1024
NEG_BIG = -1e30

ROW_TILE = 512
PROJ_ROW_TILE = 1024
GLA_ROWS = 1024
SEG_TILE = 256
ATT_TQ = 1024
ATT_TK = 512
ATT_ROW_BLOCK = 32
GLA_LEVELS = (32, 16, 8, 4, 2, 1)
SUBLANES = 8
_GLA_N_VPU_LEVELS = sum(m >= SUBLANES for m in GLA_LEVELS)

_NT = (((1,), (1,)), ((), ()))
_TN = (((0,), (0,)), ((), ()))


def _dot(a, b, dims=None):
    if dims is None:
        return jnp.dot(a, b, preferred_element_type=jnp.float32)
    return lax.dot_general(a, b, dims, preferred_element_type=jnp.float32)


def _params(*sem):
    return pltpu.CompilerParams(dimension_semantics=sem, vmem_limit_bytes=VMEM_LIMIT)


def _resident(shape):
    nd = len(shape)
    return pl.BlockSpec(shape, lambda *_: (0,) * nd, pipeline_mode=pl.Buffered(1))


def _rms_scale(x):
    return lax.rsqrt(jnp.mean(x * x, axis=-1, keepdims=True) + NORM_EPS)


def _gla_inproj_kernel(x_ref, g_ref, w_ref, wz_ref, o_ref, z_ref):
    x = x_ref[...]
    xn = (x * _rms_scale(x) * g_ref[...]).astype(jnp.bfloat16)
    n_out = o_ref.shape[1]
    step = 768
    for n0 in range(0, n_out, step):
        w = w_ref[:, n0:n0 + step].astype(jnp.bfloat16)
        o_ref[:, n0:n0 + step] = _dot(xn, w).astype(o_ref.dtype)
    z_ref[...] = _dot(xn, wz_ref[...])


def _gla_inproj(x, gain, w_in, w_z, n):
    t, d = x.shape
    return pl.pallas_call(
        _gla_inproj_kernel,
        grid=(t // PROJ_ROW_TILE,),
        in_specs=[
            pl.BlockSpec((PROJ_ROW_TILE, d), lambda i: (i, 0)),
            _resident((1, d)),
            _layer_resident(w_in, 0),
            _resident((d, LANES)),
        ],
        out_specs=[
            pl.BlockSpec((PROJ_ROW_TILE, n), lambda i: (i, 0)),
            pl.BlockSpec((PROJ_ROW_TILE, LANES), lambda i: (i, 0)),
        ],
        out_shape=[
            jax.ShapeDtypeStruct((t, n), jnp.bfloat16),
            jax.ShapeDtypeStruct((t, LANES), jnp.float32),
        ],
        compiler_params=_params("parallel"),
        name="gla_inproj",
    )(x, gain, w_in, w_z)


def _gla_constants():
    c = GLA_CHUNK
    t = np.arange(c)
    row, col = t[:, None], t[None, :]
    mats = [col <= row]
    masks = []
    for m in GLA_LEVELS:
        blk = 2 * m
        start = (t // blk) * blk
        mid = (start + m)[:, None]
        second = ((t % blk) >= m)[:, None]
        q_side = second & (col > mid) & (col <= row)
        k_side = (~second) & (col > row) & (col <= mid)
        if m < SUBLANES:
            mats.append(q_side | k_side)
        masks.append((start[:, None] == start[None, :]) & second & (~second).T)
    masks.append(np.eye(c, dtype=bool))
    lm = np.concatenate(mats, axis=0).astype(np.float32)
    mk = np.stack(masks, axis=0).astype(np.float32)
    return lm, mk


def _gla_core_kernel(q_ref, k_ref, v_ref, g_ref, z_ref, wg_ref, bg_ref, hg_ref,
                     lm_ref, mk_ref, o_ref, st_scr):
    c = GLA_CHUNK
    rows = q_ref.shape[0]

    @pl.when(pl.program_id(2) == 0)
    def _():
        st_scr[...] = jnp.zeros_like(st_scr)

    logits = _dot(z_ref[...].astype(jnp.bfloat16), wg_ref[...]) + bg_ref[...]
    log_sig = jnp.minimum(logits, 0.0) - jnp.log1p(jnp.exp(-jnp.abs(logits)))
    la = log_sig * (math.log2(math.e) / GLA_TAU)
    la_hi = la.astype(jnp.bfloat16)
    la_lo = (la - la_hi.astype(jnp.float32)).astype(jnp.bfloat16)

    scale = GLA_DK ** -0.5
    gain = hg_ref[...]
    n = rows // c
    rs = [slice(ci * c, (ci + 1) * c) for ci in range(n)]
    n_lvl = len(GLA_LEVELS)

    sums = []
    for ci in range(n):
        la_stack = jnp.concatenate([la_hi[rs[ci]], la_lo[rs[ci]]], axis=0)
        sums.append(_dot(lm_ref[...], la_stack))
    cum = [s[0:c] for s in sums]

    def level_exponent(ci, li):
        m = GLA_LEVELS[li]
        if m < SUBLANES:
            lo = (1 + li - _GLA_N_VPU_LEVELS) * c
            return sums[ci][lo:lo + c]
        b = cum[ci]
        parts = []
        for start in range(0, c, 2 * m):
            mid = b[start + m:start + m + 1, :]
            parts.append(mid - b[start:start + m])
            parts.append(b[start + m:start + 2 * m] - mid)
        return jnp.concatenate(parts, axis=0)

    qf = [q_ref[r, :].astype(jnp.float32) for r in rs]
    kf = [k_ref[r, :].astype(jnp.float32) for r in rs]
    attn = [mk_ref[n_lvl] * _dot(q_ref[r, :], k_ref[r, :], _NT) for r in rs]
    for li in range(n_lvl):
        for ci in range(n):
            e = jnp.exp2(level_exponent(ci, li))
            ql = (qf[ci] * e).astype(jnp.bfloat16)
            kl = (kf[ci] * e).astype(jnp.bfloat16)
            attn[ci] = attn[ci] + mk_ref[li] * _dot(ql, kl, _NT)

    o_intra, qd, decay, upd = [], [], [], []
    for ci in range(n):
        b = cum[ci]
        b_last = b[c - 1:c, :]
        v = v_ref[rs[ci], :]
        kd = (kf[ci] * jnp.exp2(b_last - b)).astype(jnp.bfloat16)
        qd.append((qf[ci] * jnp.exp2(b)).astype(jnp.bfloat16))
        decay.append(jnp.exp2(b_last))
        upd.append(_dot(v, kd, _TN))
        o_intra.append(_dot(attn[ci].astype(jnp.bfloat16), v))

    st = st_scr[...]
    for ci in range(n):
        o = (o_intra[ci] + _dot(qd[ci], st.astype(jnp.bfloat16), _NT)) * scale
        st = st * decay[ci] + upd[ci]
        on = o * _rms_scale(o) * gain
        gate = g_ref[rs[ci], :].astype(jnp.float32)
        o_ref[rs[ci], :] = (on * (gate * jax.nn.sigmoid(gate))).astype(o_ref.dtype)
    st_scr[...] = st


def _gla_core(qkvg, z, w_gate, b_gate, head_gain, batch, seq):
    t = qkvg.shape[0]
    nblk = seq // GLA_ROWS
    lm, mk = _gla_constants()
    lm = jnp.asarray(np.concatenate([lm, lm], axis=1), jnp.bfloat16)
    mk = jnp.asarray(mk, jnp.float32)
    kq = (GLA_HEADS * GLA_DK) // GLA_DK
    kv = (2 * GLA_HEADS * GLA_DK) // GLA_DV
    kg = kv + GLA_HEADS
    row = lambda b, h, s: b * nblk + s
    return pl.pallas_call(
        _gla_core_kernel,
        grid=(batch, GLA_HEADS, nblk),
        in_specs=[
            pl.BlockSpec((GLA_ROWS, GLA_DK), lambda b, h, s: (row(b, h, s), h)),
            pl.BlockSpec((GLA_ROWS, GLA_DK), lambda b, h, s: (row(b, h, s), kq + h)),
            pl.BlockSpec((GLA_ROWS, GLA_DV), lambda b, h, s: (row(b, h, s), kv + h)),
            pl.BlockSpec((GLA_ROWS, GLA_DV), lambda b, h, s: (row(b, h, s), kg + h)),
            pl.BlockSpec((GLA_ROWS, LANES), lambda b, h, s: (row(b, h, s), 0)),
            pl.BlockSpec((LANES, GLA_DK), lambda b, h, s: (0, h)),
            pl.BlockSpec((1, GLA_DK), lambda b, h, s: (0, h)),
            pl.BlockSpec((1, GLA_DV), lambda b, h, s: (0, 0)),
            pl.BlockSpec(lm.shape, lambda b, h, s: (0, 0)),
            pl.BlockSpec(mk.shape, lambda b, h, s: (0, 0, 0)),
        ],
        out_specs=pl.BlockSpec((GLA_ROWS, GLA_DV), lambda b, h, s: (row(b, h, s), h)),
        out_shape=jax.ShapeDtypeStruct((t, GLA_HEADS * GLA_DV), jnp.bfloat16),
        scratch_shapes=[pltpu.VMEM((GLA_DV, GLA_DK), jnp.float32)],
        compiler_params=_params("parallel", "parallel", "arbitrary"),
        name="gla_core",
    )(qkvg, qkvg, qkvg, qkvg, z, w_gate, b_gate, head_gain, lm, mk)


def _outproj_mlp_kernel(a_ref, h_ref, wo_ref, g_ref, w1_ref, w2_ref, o_ref):
    bf = jnp.bfloat16
    h1 = h_ref[...] + _dot(a_ref[...], wo_ref[...].astype(bf))
    xn = (h1 * _rms_scale(h1) * g_ref[...]).astype(bf)
    acc = h1
    d = h1.shape[1]
    for c0 in range(0, w1_ref.shape[1], d):
        hid = jnp.maximum(_dot(xn, w1_ref[:, c0:c0 + d].astype(bf)), 0.0)
        acc = acc + _dot((hid * hid).astype(bf), w2_ref[c0:c0 + d, :].astype(bf))
    o_ref[...] = acc


def _layer_resident(w, layer):
    return pl.BlockSpec((None,) + w.shape[1:], lambda *_: (layer, 0, 0),
                        pipeline_mode=pl.Buffered(1))


def _outproj_mlp(a, h, w_out, gain, w1, w2, layer_out, layer_mlp):
    t, d = h.shape
    tile = pl.BlockSpec((ROW_TILE, d), lambda i: (i, 0))
    return pl.pallas_call(
        _outproj_mlp_kernel,
        grid=(t // ROW_TILE,),
        in_specs=[tile, tile, _layer_resident(w_out, layer_out), _resident((1, d)),
                  _layer_resident(w1, layer_mlp), _layer_resident(w2, layer_mlp)],
        out_specs=tile,
        out_shape=jax.ShapeDtypeStruct((t, d), jnp.float32),
        compiler_params=_params("parallel"),
        name="outproj_mlp",
    )(a, h, w_out, gain, w1, w2)


def _segment_norm(y, seg_ones, gain):
    outs = []
    for c0 in range(0, y.shape[1], SEG_TILE):
        yc = y[:, c0:c0 + SEG_TILE]
        ss = _dot((yc * yc).astype(jnp.bfloat16), seg_ones)
        outs.append(yc * lax.rsqrt(ss * (1.0 / DIFF_DH) + NORM_EPS) * gain)
    return outs


def _diff_proj_kernel(h_ref, gkv_ref, gq_ref, wkv_ref, wq_ref, kg_ref, qg_ref, so_ref,
                      k_ref, v_ref, q_ref):
    x = h_ref[...]
    xhat = x * _rms_scale(x)
    xkv = (xhat * gkv_ref[...]).astype(jnp.bfloat16)
    xq = (xhat * gq_ref[...]).astype(jnp.bfloat16)
    nk = k_ref.shape[1]
    seg_ones = so_ref[...]
    bf = jnp.bfloat16
    kraw = _dot(xkv, wkv_ref[:, :nk].astype(bf))
    for i, blk in enumerate(_segment_norm(kraw, seg_ones, kg_ref[...])):
        k_ref[:, i * SEG_TILE:(i + 1) * SEG_TILE] = blk.astype(k_ref.dtype)
    v_ref[...] = _dot(xkv, wkv_ref[:, nk:].astype(bf)).astype(v_ref.dtype)
    qraw = _dot(xq, wq_ref[...].astype(bf))
    qscale = DIFF_DH ** -0.5 * math.log2(math.e)
    for i, blk in enumerate(_segment_norm(qraw, seg_ones, qg_ref[...])):
        q_ref[:, i * SEG_TILE:(i + 1) * SEG_TILE] = (blk * qscale).astype(q_ref.dtype)


def _diff_proj(h, g_kv, g_q, w_kv, w_q, k_gain, q_gain):
    t, d = h.shape
    nk = DIFF_HEADS * 2 * DIFF_DH
    nv = DIFF_HEADS * DIFF_VD
    lane = np.arange(SEG_TILE)
    seg_ones = jnp.asarray(lane[:, None] // DIFF_DH == lane[None, :] // DIFF_DH, jnp.bfloat16)
    tile = lambda n: pl.BlockSpec((PROJ_ROW_TILE, n), lambda i: (i, 0))
    return pl.pallas_call(
        _diff_proj_kernel,
        grid=(t // PROJ_ROW_TILE,),
        in_specs=[tile(d), _resident((1, d)), _resident((1, d)), _resident(w_kv.shape),
                  _layer_resident(w_q, 0), _resident((1, SEG_TILE)), _resident((1, SEG_TILE)),
                  _resident((SEG_TILE, SEG_TILE))],
        out_specs=[tile(nk), tile(nv), tile(nk)],
        out_shape=[jax.ShapeDtypeStruct((t, nk), jnp.bfloat16),
                   jax.ShapeDtypeStruct((t, nv), jnp.bfloat16),
                   jax.ShapeDtypeStruct((t, nk), jnp.bfloat16)],
        compiler_params=_params("parallel"),
        name="diff_proj",
    )(h, g_kv, g_q, w_kv, w_q, k_gain, q_gain, seg_ones)


def _diff_attn_kernel(q_ref, k_ref, v_ref, lam_ref, hg_ref, o_ref,
                      m_scr, l_scr, acc_scr, s_bufs, p_bufs, *, lambda_init):
    qi = pl.program_id(2)
    tq = q_ref.shape[0]
    tk = ATT_TK

    m_scr[...] = jnp.full_like(m_scr, NEG_BIG)
    l_scr[...] = jnp.zeros_like(l_scr)
    acc_scr[...] = jnp.zeros_like(acc_scr)

    q = q_ref[...]
    lane = lax.broadcasted_iota(jnp.int32, q.shape, 1)
    qcs = [jnp.where((lane // DIFF_DH) == c, q, jnp.zeros_like(q)) for c in range(2)]

    def block(j):
        return pl.ds(pl.multiple_of(j * tk, tk), tk)

    rb = ATT_ROW_BLOCK
    col_minus_row = (lax.broadcasted_iota(jnp.int32, (rb, tk), 1)
                     - lax.broadcasted_iota(jnp.int32, (rb, tk), 0))

    def step(j, buf, row_lo, diag_row0):
        k = k_ref[block(j), :]
        v = v_ref[block(j), :]
        s_scr = s_bufs.at[buf]
        p_scr = p_bufs.at[buf]
        all_rows = slice(row_lo, tq)
        for c in range(2):
            s_scr[c, all_rows, :] = _dot(qcs[c][all_rows], k, _NT)
        for c in range(2):
            for r0 in range(row_lo, tq, rb):
                rows = slice(r0, r0 + rb)
                masked = diag_row0 is not None and r0 - diag_row0 < tk
                kmax = min(tk, -(-(r0 - diag_row0 + rb) // LANES) * LANES) if masked else tk
                s = s_scr[c, rows, 0:kmax]
                if masked:
                    s = jnp.where(col_minus_row[:, 0:kmax] <= r0 - diag_row0, s, NEG_BIG)
                m_prev = m_scr[c, rows, :]
                m_new = jnp.maximum(m_prev, jnp.max(s, axis=-1, keepdims=True))
                alpha = jnp.exp2(m_prev - m_new)
                p = jnp.exp2(s - jnp.tile(m_new, (1, kmax // LANES)))
                psum = p[:, 0:LANES]
                for t0 in range(LANES, kmax, LANES):
                    psum = psum + p[:, t0:t0 + LANES]
                l_scr[c, rows, :] = alpha * l_scr[c, rows, :] + psum
                m_scr[c, rows, :] = m_new
                acc_scr[c, rows, :] = alpha * acc_scr[c, rows, :]
                p_scr[c, rows, 0:kmax] = p.astype(jnp.bfloat16)
                if kmax < tk:
                    p_scr[c, rows, kmax:tk] = jnp.zeros((rb, tk - kmax), jnp.bfloat16)
            acc_scr[c, all_rows, :] += _dot(p_scr[c, all_rows, :], v)

    n_sub = tq // tk

    def pair(i, carry):
        step(2 * i, 0, 0, None)
        step(2 * i + 1, 1, 0, None)
        return carry

    lax.fori_loop(0, (n_sub // 2) * qi, pair, 0)
    for t in range(n_sub):
        step(n_sub * qi + t, t % 2, t * tk, t * tk)

    lp = lam_ref[...]
    lam = (jnp.exp(jnp.sum(lp[0:1] * lp[1:2], axis=-1, keepdims=True))
           - jnp.exp(jnp.sum(lp[2:3] * lp[3:4], axis=-1, keepdims=True)) + lambda_init)
    l0 = jnp.sum(l_scr[0], axis=-1, keepdims=True)
    l1 = jnp.sum(l_scr[1], axis=-1, keepdims=True)
    o = acc_scr[0] / l0 - lam * (acc_scr[1] / l1)
    on = o * _rms_scale(o) * hg_ref[...]
    o_ref[...] = (on * (1.0 - lambda_init)).astype(o_ref.dtype)


def _diff_attn(q, k, v, lam_params, head_gain, batch, seq, lambda_init):
    t = q.shape[0]
    assert ATT_TQ % (2 * ATT_TK) == 0
    nq = seq // ATT_TQ
    kern = functools.partial(_diff_attn_kernel, lambda_init=lambda_init)
    return pl.pallas_call(
        kern,
        grid=(batch, DIFF_HEADS, nq),
        in_specs=[
            pl.BlockSpec((ATT_TQ, 2 * DIFF_DH), lambda b, h, i: (b * nq + i, h)),
            pl.BlockSpec((seq, 2 * DIFF_DH), lambda b, h, i: (b, h)),
            pl.BlockSpec((seq, DIFF_VD), lambda b, h, i: (b, h)),
            pl.BlockSpec(lam_params.shape, lambda b, h, i: (0, 0)),
            pl.BlockSpec((1, DIFF_VD), lambda b, h, i: (0, 0)),
        ],
        out_specs=pl.BlockSpec((ATT_TQ, DIFF_VD), lambda b, h, i: (b * nq + i, h)),
        out_shape=jax.ShapeDtypeStruct((t, DIFF_HEADS * DIFF_VD), jnp.bfloat16),
        scratch_shapes=[
            pltpu.VMEM((2, ATT_TQ, LANES), jnp.float32),
            pltpu.VMEM((2, ATT_TQ, LANES), jnp.float32),
            pltpu.VMEM((2, ATT_TQ, DIFF_VD), jnp.float32),
            pltpu.VMEM((2, 2, ATT_TQ, ATT_TK), jnp.float32),
            pltpu.VMEM((2, 2, ATT_TQ, ATT_TK), jnp.bfloat16),
        ],
        compiler_params=_params("parallel", "parallel", "arbitrary"),
        name="diff_attn",
    )(q, k, v, lam_params, head_gain)


def kernel(x, a_norm, a_w_in, a_w_gate_up, a_b_gate, a_head_norm, a_w_out, kv_norm, w_kv, k_norm,
           b_norm, b_w_q, b_q_norm, b_lambda, b_head_norm, b_w_out, mlp_norm, mlp_w1, mlp_w2):
    batch, seq, d = x.shape
    bf = jnp.bfloat16
    row = lambda p: p.reshape(1, -1)
    h = x.reshape(batch * seq, d)

    n_main = 2 * GLA_HEADS * GLA_DK + 2 * GLA_HEADS * GLA_DV
    w_main = a_w_in[:, :, :n_main]
    w_z = jnp.pad(a_w_in[0, :, n_main:], ((0, 0), (0, LANES - GLA_RANK))).astype(bf)
    w_gate = jnp.pad(a_w_gate_up[0], ((0, LANES - GLA_RANK), (0, 0))).astype(bf)
    qkvg, z = _gla_inproj(h, row(a_norm[0]), w_main, w_z, n_main)
    og = _gla_core(qkvg, z, w_gate, row(a_b_gate[0]), row(a_head_norm[0]), batch, seq)
    h = _outproj_mlp(og, h, a_w_out, row(mlp_norm[0]), mlp_w1, mlp_w2, 0, 0)

    layer = 1
    lambda_init = 0.8 - 0.6 * math.exp(-0.3 * layer)
    reps = SEG_TILE // DIFF_DH
    kn, v, qn = _diff_proj(h, row(kv_norm), row(b_norm[0]), w_kv, b_w_q,
                           row(jnp.tile(k_norm, reps)), row(jnp.tile(b_q_norm[0], reps)))
    oa = _diff_attn(qn, kn, v, b_lambda[0], row(b_head_norm[0]), batch, seq, lambda_init)
    h = _outproj_mlp(oa, h, b_w_out, row(mlp_norm[1]), mlp_w1, mlp_w2, 0, 1)
    return h.reshape(batch, seq, d)
```

```python
import functools
import math

import numpy as np
import jax
import jax.numpy as jnp
from jax import lax
from jax.experimental import pallas as pl
from jax.experimental.pallas import tpu as pltpu

D_MODEL = 1024
GLA_HEADS = 4
GLA_DK = 128
GLA_DV = 256
GLA_RANK = 16
GLA_TAU = 16.0
GLA_CHUNK = 64
DIFF_HEADS = 8
DIFF_DH = 64
DIFF_VD = 128
MLP_HIDDEN = 4 * D_MODEL
NORM_EPS = 1e-6

LANES = 128
VMEM_LIMIT = 56 * 1024 * 1024
NEG_BIG = -1e30

ROW_TILE = 512
PROJ_ROW_TILE = 1024
GLA_ROWS = 1024
SEG_TILE = 256
ATT_TQ = 1024
ATT_TK = 512
ATT_ROW_BLOCK = 32
GLA_LEVELS = (32, 16, 8, 4, 2, 1)
SUBLANES = 8
_GLA_N_VPU_LEVELS = sum(m >= SUBLANES for m in GLA_LEVELS)

_NT = (((1,), (1,)), ((), ()))
_TN = (((0,), (0,)), ((), ()))


def _dot(a, b, dims=None):
    if dims is None:
        return jnp.dot(a, b, preferred_element_type=jnp.float32)
    return lax.dot_general(a, b, dims, preferred_element_type=jnp.float32)


def _params(*sem):
    return pltpu.CompilerParams(dimension_semantics=sem, vmem_limit_bytes=VMEM_LIMIT)


def _resident(shape):
    nd = len(shape)
    return pl.BlockSpec(shape, lambda *_: (0,) * nd, pipeline_mode=pl.Buffered(1))


def _layer_resident(w, layer):
    return pl.BlockSpec((None,) + w.shape[1:], lambda *_: (layer, 0, 0),
                        pipeline_mode=pl.Buffered(1))


def _rms_scale(x):
    return lax.rsqrt(jnp.mean(x * x, axis=-1, keepdims=True) + NORM_EPS)


def _gla_inproj_kernel(x_ref, g_ref, w_ref, wz_ref, o_ref, z_ref):
    x = x_ref[...]
    xn = (x * _rms_scale(x) * g_ref[...]).astype(jnp.bfloat16)
    n_out = o_ref.shape[1]
    step = 768
    for n0 in range(0, n_out, step):
        w = w_ref[:, n0:n0 + step].astype(jnp.bfloat16)
        o_ref[:, n0:n0 + step] = _dot(xn, w).astype(o_ref.dtype)
    z_ref[...] = _dot(xn, wz_ref[...])


def _gla_inproj(x, gain, w_in, w_z, n):
    t, d = x.shape
    return pl.pallas_call(
        _gla_inproj_kernel,
        grid=(t // PROJ_ROW_TILE,),
        in_specs=[
            pl.BlockSpec((PROJ_ROW_TILE, d), lambda i: (i, 0)),
            _resident((1, d)),
            _layer_resident(w_in, 0),
            _resident((d, LANES)),
        ],
        out_specs=[
            pl.BlockSpec((PROJ_ROW_TILE, n), lambda i: (i, 0)),
            pl.BlockSpec((PROJ_ROW_TILE, LANES), lambda i: (i, 0)),
        ],
        out_shape=[
            jax.ShapeDtypeStruct((t, n), jnp.bfloat16),
            jax.ShapeDtypeStruct((t, LANES), jnp.float32),
        ],
        compiler_params=_params("parallel"),
        name="gla_inproj",
    )(x, gain, w_in, w_z)


def _gla_constants():
    c = GLA_CHUNK
    t = np.arange(c)
    row, col = t[:, None], t[None, :]
    mats = [col <= row]
    masks = []
    for m in GLA_LEVELS:
        blk = 2 * m
        start = (t // blk) * blk
        mid = (start + m)[:, None]
        second = ((t % blk) >= m)[:, None]
        q_side = second & (col > mid) & (col <= row)
        k_side = (~second) & (col > row) & (col <= mid)
        if m < SUBLANES:
            mats.append(q_side | k_side)
        masks.append((start[:, None] == start[None, :]) & second & (~second).T)
    masks.append(np.eye(c, dtype=bool))
    lm = np.concatenate(mats, axis=0).astype(np.float32)
    mk = np.stack(masks, axis=0).astype(np.float32)
    return lm, mk


def _gla_core_kernel(q_ref, k_ref, v_ref, g_ref, z_ref, wg_ref, bg_ref, hg_ref,
                     lm_ref, mk_ref, o_ref, st_scr):
    c = GLA_CHUNK
    rows = q_ref.shape[0]

    @pl.when(pl.program_id(2) == 0)
    def _():
        st_scr[...] = jnp.zeros_like(st_scr)

    logits = _dot(z_ref[...].astype(jnp.bfloat16), wg_ref[...]) + bg_ref[...]
    log_sig = jnp.minimum(logits, 0.0) - jnp.log(1.0 + jnp.exp(-jnp.abs(logits)))
    la = log_sig * (math.log2(math.e) / GLA_TAU)
    la_hi = la.astype(jnp.bfloat16)
    la_lo = (la - la_hi.astype(jnp.float32)).astype(jnp.bfloat16)

    scale = GLA_DK ** -0.5
    gain = hg_ref[...]
    n = rows // c
    rs = [slice(ci * c, (ci + 1) * c) for ci in range(n)]
    n_lvl = len(GLA_LEVELS)

    sums = []
    for ci in range(n):
        la_stack = jnp.concatenate([la_hi[rs[ci]], la_lo[rs[ci]]], axis=0)
        sums.append(_dot(lm_ref[...], la_stack))
    cum = [s[0:c] for s in sums]

    def level_exponent(ci, li):
        m = GLA_LEVELS[li]
        if m < SUBLANES:
            lo = (1 + li - _GLA_N_VPU_LEVELS) * c
            return sums[ci][lo:lo + c]
        b = cum[ci]
        parts = []
        for start in range(0, c, 2 * m):
            mid = b[start + m:start + m + 1, :]
            parts.append(mid - b[start:start + m])
            parts.append(b[start + m:start + 2 * m] - mid)
        return jnp.concatenate(parts, axis=0)

    qf = [q_ref[r, :].astype(jnp.float32) for r in rs]
    kf = [k_ref[r, :].astype(jnp.float32) for r in rs]
    attn = [mk_ref[n_lvl] * _dot(q_ref[r, :], k_ref[r, :], _NT) for r in rs]
    for li in range(n_lvl):
        for ci in range(n):
            e = jnp.exp2(level_exponent(ci, li))
            ql = (qf[ci] * e).astype(jnp.bfloat16)
            kl = (kf[ci] * e).astype(jnp.bfloat16)
            attn[ci] = attn[ci] + mk_ref[li] * _dot(ql, kl, _NT)

    o_intra, qd, decay, upd = [], [], [], []
    for ci in range(n):
        b = cum[ci]
        b_last = b[c - 1:c, :]
        v = v_ref[rs[ci], :]
        kd = (kf[ci] * jnp.exp2(b_last - b)).astype(jnp.bfloat16)
        qd.append((qf[ci] * jnp.exp2(b)).astype(jnp.bfloat16))
        decay.append(jnp.exp2(b_last))
        upd.append(_dot(v, kd, _TN))
        o_intra.append(_dot(attn[ci].astype(jnp.bfloat16), v))

    st = st_scr[...]
    for ci in range(n):
        o = (o_intra[ci] + _dot(qd[ci], st.astype(jnp.bfloat16), _NT)) * scale
        st = st * decay[ci] + upd[ci]
        on = o * _rms_scale(o) * gain
        t = 0.5 * g_ref[rs[ci], :].astype(jnp.float32)
        o_ref[rs[ci], :] = (on * (t + t * jnp.tanh(t))).astype(o_ref.dtype)
    st_scr[...] = st


def _gla_core(qkvg, z, w_gate, b_gate, head_gain, batch, seq):
    t = qkvg.shape[0]
    nblk = seq // GLA_ROWS
    lm, mk = _gla_constants()
    lm = jnp.asarray(np.concatenate([lm, lm], axis=1), jnp.bfloat16)
    mk = jnp.asarray(mk, jnp.float32)
    kq = (GLA_HEADS * GLA_DK) // GLA_DK
    kv = (2 * GLA_HEADS * GLA_DK) // GLA_DV
    kg = kv + GLA_HEADS
    row = lambda b, h, s: b * nblk + s
    return pl.pallas_call(
        _gla_core_kernel,
        grid=(batch, GLA_HEADS, nblk),
        in_specs=[
            pl.BlockSpec((GLA_ROWS, GLA_DK), lambda b, h, s: (row(b, h, s), h)),
            pl.BlockSpec((GLA_ROWS, GLA_DK), lambda b, h, s: (row(b, h, s), kq + h)),
            pl.BlockSpec((GLA_ROWS, GLA_DV), lambda b, h, s: (row(b, h, s), kv + h)),
            pl.BlockSpec((GLA_ROWS, GLA_DV), lambda b, h, s: (row(b, h, s), kg + h)),
            pl.BlockSpec((GLA_ROWS, LANES), lambda b, h, s: (row(b, h, s), 0)),
            pl.BlockSpec((LANES, GLA_DK), lambda b, h, s: (0, h)),
            pl.BlockSpec((1, GLA_DK), lambda b, h, s: (0, h)),
            pl.BlockSpec((1, GLA_DV), lambda b, h, s: (0, 0)),
            pl.BlockSpec(lm.shape, lambda b, h, s: (0, 0)),
            pl.BlockSpec(mk.shape, lambda b, h, s: (0, 0, 0)),
        ],
        out_specs=pl.BlockSpec((GLA_ROWS, GLA_DV), lambda b, h, s: (row(b, h, s), h)),
        out_shape=jax.ShapeDtypeStruct((t, GLA_HEADS * GLA_DV), jnp.bfloat16),
        scratch_shapes=[pltpu.VMEM((GLA_DV, GLA_DK), jnp.float32)],
        compiler_params=_params("parallel", "parallel", "arbitrary"),
        name="gla_core",
    )(qkvg, qkvg, qkvg, qkvg, z, w_gate, b_gate, head_gain, lm, mk)


def _outproj_mlp_kernel(a_ref, h_ref, wo_ref, g_ref, w1_ref, w2_ref, o_ref):
    bf = jnp.bfloat16
    h1 = h_ref[...] + _dot(a_ref[...], wo_ref[...].astype(bf))
    xn = (h1 * _rms_scale(h1) * g_ref[...]).astype(bf)
    acc = h1
    d = h1.shape[1]
    for c0 in range(0, w1_ref.shape[1], d):
        hid = jnp.maximum(_dot(xn, w1_ref[:, c0:c0 + d].astype(bf)), 0.0)
        acc = acc + _dot((hid * hid).astype(bf), w2_ref[c0:c0 + d, :].astype(bf))
    o_ref[...] = acc


def _outproj_mlp(a, h, w_out, gain, w1, w2, layer_out, layer_mlp):
    t, d = h.shape
    tile = pl.BlockSpec((ROW_TILE, d), lambda i: (i, 0))
    return pl.pallas_call(
        _outproj_mlp_kernel,
        grid=(t // ROW_TILE,),
        in_specs=[tile, tile, _layer_resident(w_out, layer_out), _resident((1, d)),
                  _layer_resident(w1, layer_mlp), _layer_resident(w2, layer_mlp)],
        out_specs=tile,
        out_shape=jax.ShapeDtypeStruct((t, d), jnp.float32),
        compiler_params=_params("parallel"),
        name="outproj_mlp",
    )(a, h, w_out, gain, w1, w2)


def _segment_norm(y, seg_ones, gain):
    outs = []
    for c0 in range(0, y.shape[1], SEG_TILE):
        yc = y[:, c0:c0 + SEG_TILE]
        ss = _dot((yc * yc).astype(jnp.bfloat16), seg_ones)
        outs.append(yc * lax.rsqrt(ss * (1.0 / DIFF_DH) + NORM_EPS) * gain)
    return outs


def _diff_proj_kernel(h_ref, gkv_ref, gq_ref, wkv_ref, wq_ref, kg_ref, qg_ref, so_ref,
                      k_ref, v_ref, q_ref):
    x = h_ref[...]
    xhat = x * _rms_scale(x)
    xkv = (xhat * gkv_ref[...]).astype(jnp.bfloat16)
    xq = (xhat * gq_ref[...]).astype(jnp.bfloat16)
    nk = k_ref.shape[1]
    seg_ones = so_ref[...]
    bf = jnp.bfloat16
    kraw = _dot(xkv, wkv_ref[:, :nk].astype(bf))
    for i, blk in enumerate(_segment_norm(kraw, seg_ones, kg_ref[...])):
        k_ref[:, i * SEG_TILE:(i + 1) * SEG_TILE] = blk.astype(k_ref.dtype)
    v_ref[...] = _dot(xkv, wkv_ref[:, nk:].astype(bf)).astype(v_ref.dtype)
    qraw = _dot(xq, wq_ref[...].astype(bf))
    qscale = DIFF_DH ** -0.5 * math.log2(math.e)
    for i, blk in enumerate(_segment_norm(qraw, seg_ones, qg_ref[...])):
        q_ref[:, i * SEG_TILE:(i + 1) * SEG_TILE] = (blk * qscale).astype(q_ref.dtype)


def _diff_proj(h, g_kv, g_q, w_kv, w_q, k_gain, q_gain):
    t, d = h.shape
    nk = DIFF_HEADS * 2 * DIFF_DH
    nv = DIFF_HEADS * DIFF_VD
    lane = np.arange(SEG_TILE)
    seg_ones = jnp.asarray(lane[:, None] // DIFF_DH == lane[None, :] // DIFF_DH, jnp.bfloat16)
    tile = lambda n: pl.BlockSpec((PROJ_ROW_TILE, n), lambda i: (i, 0))
    return pl.pallas_call(
        _diff_proj_kernel,
        grid=(t // PROJ_ROW_TILE,),
        in_specs=[tile(d), _resident((1, d)), _resident((1, d)), _resident(w_kv.shape),
                  _layer_resident(w_q, 0), _resident((1, SEG_TILE)), _resident((1, SEG_TILE)),
                  _resident((SEG_TILE, SEG_TILE))],
        out_specs=[tile(nk), tile(nv), tile(nk)],
        out_shape=[jax.ShapeDtypeStruct((t, nk), jnp.bfloat16),
                   jax.ShapeDtypeStruct((t, nv), jnp.bfloat16),
                   jax.ShapeDtypeStruct((t, nk), jnp.bfloat16)],
        compiler_params=_params("parallel"),
        name="diff_proj",
    )(h, g_kv, g_q, w_kv, w_q, k_gain, q_gain, seg_ones)


def _diff_attn_kernel(q_ref, k_ref, v_ref, lam_ref, hg_ref, o_ref,
                      m_scr, l_scr, acc_scr, s_bufs, p_bufs, *, lambda_init):
    qi = pl.program_id(2)
    tq = q_ref.shape[0]
    tk = ATT_TK

    m_scr[...] = jnp.full_like(m_scr, NEG_BIG)
    l_scr[...] = jnp.zeros_like(l_scr)
    acc_scr[...] = jnp.zeros_like(acc_scr)

    q = q_ref[...]
    lane = lax.broadcasted_iota(jnp.int32, q.shape, 1)
    qcs = [jnp.where((lane // DIFF_DH) == c, q, jnp.zeros_like(q)) for c in range(2)]

    def block(j):
        return pl.ds(pl.multiple_of(j * tk, tk), tk)

    rb = ATT_ROW_BLOCK
    col_minus_row = (lax.broadcasted_iota(jnp.int32, (rb, tk), 1)
                     - lax.broadcasted_iota(jnp.int32, (rb, tk), 0))

    def step(j, buf, row_lo, diag_row0):
        k = k_ref[block(j), :]
        v = v_ref[block(j), :]
        s_scr = s_bufs.at[buf]
        p_scr = p_bufs.at[buf]
        all_rows = slice(row_lo, tq)
        for c in range(2):
            s_scr[c, all_rows, :] = _dot(qcs[c][all_rows], k, _NT)
        for c in range(2):
            for r0 in range(row_lo, tq, rb):
                rows = slice(r0, r0 + rb)
                masked = diag_row0 is not None and r0 - diag_row0 < tk
                kmax = min(tk, -(-(r0 - diag_row0 + rb) // LANES) * LANES) if masked else tk
                s = s_scr[c, rows, 0:kmax]
                if masked:
                    s = jnp.where(col_minus_row[:, 0:kmax] <= r0 - diag_row0, s, NEG_BIG)
                m_prev = m_scr[c, rows, :]
                m_new = jnp.maximum(m_prev, jnp.max(s, axis=-1, keepdims=True))
                alpha = jnp.exp2(m_prev - m_new)
                p = jnp.exp2(s - jnp.tile(m_new, (1, kmax // LANES)))
                psum = p[:, 0:LANES]
                for t0 in range(LANES, kmax, LANES):
                    psum = psum + p[:, t0:t0 + LANES]
                l_scr[c, rows, :] = alpha * l_scr[c, rows, :] + psum
                m_scr[c, rows, :] = m_new
                acc_scr[c, rows, :] = alpha * acc_scr[c, rows, :]
                p_scr[c, rows, 0:kmax] = p.astype(jnp.bfloat16)
                if kmax < tk:
                    p_scr[c, rows, kmax:tk] = jnp.zeros((rb, tk - kmax), jnp.bfloat16)
            acc_scr[c, all_rows, :] += _dot(p_scr[c, all_rows, :], v)

    n_sub = tq // tk

    def pair(i, carry):
        step(2 * i, 0, 0, None)
        step(2 * i + 1, 1, 0, None)
        return carry

    lax.fori_loop(0, (n_sub // 2) * qi, pair, 0)
    for t in range(n_sub):
        step(n_sub * qi + t, t % 2, t * tk, t * tk)

    lp = lam_ref[...]
    lam = (jnp.exp(jnp.sum(lp[0:1] * lp[1:2], axis=-1, keepdims=True))
           - jnp.exp(jnp.sum(lp[2:3] * lp[3:4], axis=-1, keepdims=True)) + lambda_init)
    l0 = jnp.sum(l_scr[0], axis=-1, keepdims=True)
    l1 = jnp.sum(l_scr[1], axis=-1, keepdims=True)
    o = acc_scr[0] / l0 - lam * (acc_scr[1] / l1)
    on = o * _rms_scale(o) * hg_ref[...]
    o_ref[...] = (on * (1.0 - lambda_init)).astype(o_ref.dtype)


def _diff_attn(q, k, v, lam_params, head_gain, batch, seq, lambda_init):
    t = q.shape[0]
    assert ATT_TQ % (2 * ATT_TK) == 0
    nq = seq // ATT_TQ
    kern = functools.partial(_diff_attn_kernel, lambda_init=lambda_init)
    return pl.pallas_call(
        kern,
        grid=(batch, DIFF_HEADS, nq),
        in_specs=[
            pl.BlockSpec((ATT_TQ, 2 * DIFF_DH), lambda b, h, i: (b * nq + i, h)),
            pl.BlockSpec((seq, 2 * DIFF_DH), lambda b, h, i: (b, h)),
            pl.BlockSpec((seq, DIFF_VD), lambda b, h, i: (b, h)),
            pl.BlockSpec(lam_params.shape, lambda b, h, i: (0, 0)),
            pl.BlockSpec((1, DIFF_VD), lambda b, h, i: (0, 0)),
        ],
        out_specs=pl.BlockSpec((ATT_TQ, DIFF_VD), lambda b, h, i: (b * nq + i, h)),
        out_shape=jax.ShapeDtypeStruct((t, DIFF_HEADS * DIFF_VD), jnp.bfloat16),
        scratch_shapes=[
            pltpu.VMEM((2, ATT_TQ, LANES), jnp.float32),
            pltpu.VMEM((2, ATT_TQ, LANES), jnp.float32),
            pltpu.VMEM((2, ATT_TQ, DIFF_VD), jnp.float32),
            pltpu.VMEM((2, 2, ATT_TQ, ATT_TK), jnp.float32),
            pltpu.VMEM((2, 2, ATT_TQ, ATT_TK), jnp.bfloat16),
        ],
        compiler_params=_params("parallel", "parallel", "arbitrary"),
        name="diff_attn",
    )(q, k, v, lam_params, head_gain)


def kernel(x, a_norm, a_w_in, a_w_gate_up, a_b_gate, a_head_norm, a_w_out, kv_norm, w_kv, k_norm,
           b_norm, b_w_q, b_q_norm, b_lambda, b_head_norm, b_w_out, mlp_norm, mlp_w1, mlp_w2):
    batch, seq, d = x.shape
    bf = jnp.bfloat16
    row = lambda p: p.reshape(1, -1)
    h = x.reshape(batch * seq, d)

    n_main = 2 * GLA_HEADS * GLA_DK + 2 * GLA_HEADS * GLA_DV
    w_z = jnp.pad(a_w_in[0, :, n_main:], ((0, 0), (0, LANES - GLA_RANK))).astype(bf)
    w_gate = jnp.pad(a_w_gate_up[0], ((0, LANES - GLA_RANK), (0, 0))).astype(bf)
    qkvg, z = _gla_inproj(h, row(a_norm[0]), a_w_in, w_z, n_main)
    og = _gla_core(qkvg, z, w_gate, row(a_b_gate[0]), row(a_head_norm[0]), batch, seq)
    h = _outproj_mlp(og, h, a_w_out, row(mlp_norm[0]), mlp_w1, mlp_w2, 0, 0)

    layer = 1
    lambda_init = 0.8 - 0.6 * math.exp(-0.3 * layer)
    reps = SEG_TILE // DIFF_DH
    kn, v, qn = _diff_proj(h, row(kv_norm), row(b_norm[0]), w_kv, b_w_q,
                           row(jnp.tile(k_norm, reps)), row(jnp.tile(b_q_norm[0], reps)))
    oa = _diff_attn(qn, kn, v, b_lambda[0], row(b_head_norm[0]), batch, seq, lambda_init)
    h = _outproj_mlp(oa, h, b_w_out, row(mlp_norm[1]), mlp_w1, mlp_w2, 0, 1)
    return h.reshape(batch, seq, d)
```

```python
import functools
import math

import numpy as np
import jax
import jax.numpy as jnp
from jax import lax
from jax.experimental import pallas as pl
from jax.experimental.pallas import tpu as pltpu

D_MODEL = 1024
GLA_HEADS = 4
GLA_DK = 128
GLA_DV = 256
GLA_RANK = 16
GLA_TAU = 16.0
GLA_CHUNK = 64
DIFF_HEADS = 8
DIFF_DH = 64
DIFF_VD = 128
MLP_HIDDEN = 4 * D_MODEL
NORM_EPS = 1e-6

LANES = 128
VMEM_LIMIT = 56 * 1024 * 1024
NEG_BIG = -1e30

ROW_TILE = 512
PROJ_ROW_TILE = 1024
GLA_ROWS = 1024
SEG_TILE = 256
ATT_TQ = 2048
ATT_TK = 512
ATT_ROW_BLOCK = 32
GLA_LEVELS = (32, 16, 8, 4, 2, 1)
SUBLANES = 8
_GLA_N_VPU_LEVELS = sum(m >= SUBLANES for m in GLA_LEVELS)

_NT = (((1,), (1,)), ((), ()))
_TN = (((0,), (0,)), ((), ()))


def _dot(a, b, dims=None):
    if dims is None:
        return jnp.dot(a, b, preferred_element_type=jnp.float32)
    return lax.dot_general(a, b, dims, preferred_element_type=jnp.float32)


def _params(*sem):
    return pltpu.CompilerParams(dimension_semantics=sem, vmem_limit_bytes=VMEM_LIMIT)


def _resident(shape):
    nd = len(shape)
    return pl.BlockSpec(shape, lambda *_: (0,) * nd, pipeline_mode=pl.Buffered(1))


def _layer_resident(w, layer):
    return pl.BlockSpec((None,) + w.shape[1:], lambda *_: (layer, 0, 0),
                        pipeline_mode=pl.Buffered(1))


def _rms_scale(x):
    return lax.rsqrt(jnp.mean(x * x, axis=-1, keepdims=True) + NORM_EPS)


def _gla_inproj_kernel(x_ref, g_ref, w_ref, wz_ref, o_ref, z_ref):
    x = x_ref[...]
    xn = (x * _rms_scale(x) * g_ref[...]).astype(jnp.bfloat16)
    n_out = o_ref.shape[1]
    step = 768
    for n0 in range(0, n_out, step):
        w = w_ref[:, n0:n0 + step].astype(jnp.bfloat16)
        o_ref[:, n0:n0 + step] = _dot(xn, w).astype(o_ref.dtype)
    z_ref[...] = _dot(xn, wz_ref[...])


def _gla_inproj(x, gain, w_in, w_z, n):
    t, d = x.shape
    return pl.pallas_call(
        _gla_inproj_kernel,
        grid=(t // PROJ_ROW_TILE,),
        in_specs=[
            pl.BlockSpec((PROJ_ROW_TILE, d), lambda i: (i, 0)),
            _resident((1, d)),
            _layer_resident(w_in, 0),
            _resident((d, LANES)),
        ],
        out_specs=[
            pl.BlockSpec((PROJ_ROW_TILE, n), lambda i: (i, 0)),
            pl.BlockSpec((PROJ_ROW_TILE, LANES), lambda i: (i, 0)),
        ],
        out_shape=[
            jax.ShapeDtypeStruct((t, n), jnp.bfloat16),
            jax.ShapeDtypeStruct((t, LANES), jnp.float32),
        ],
        compiler_params=_params("parallel"),
        name="gla_inproj",
    )(x, gain, w_in, w_z)


def _gla_constants():
    c = GLA_CHUNK
    t = np.arange(c)
    row, col = t[:, None], t[None, :]
    mats = [col <= row]
    masks = []
    for m in GLA_LEVELS:
        blk = 2 * m
        start = (t // blk) * blk
        mid = (start + m)[:, None]
        second = ((t % blk) >= m)[:, None]
        q_side = second & (col > mid) & (col <= row)
        k_side = (~second) & (col > row) & (col <= mid)
        if m < SUBLANES:
            mats.append(q_side | k_side)
        masks.append((start[:, None] == start[None, :]) & second & (~second).T)
    masks.append(np.eye(c, dtype=bool))
    lm = np.concatenate(mats, axis=0).astype(np.float32)
    mk = np.stack(masks, axis=0).astype(np.float32)
    return lm, mk


def _gla_core_kernel(q_ref, k_ref, v_ref, g_ref, z_ref, wg_ref, bg_ref, hg_ref,
                     lm_ref, mk_ref, o_ref, st_scr):
    c = GLA_CHUNK
    rows = q_ref.shape[0]

    @pl.when(pl.program_id(2) == 0)
    def _():
        st_scr[...] = jnp.zeros_like(st_scr)

    logits = _dot(z_ref[...].astype(jnp.bfloat16), wg_ref[...]) + bg_ref[...]
    log_sig = jnp.minimum(logits, 0.0) - jnp.log(1.0 + jnp.exp(-jnp.abs(logits)))
    la = log_sig * (math.log2(math.e) / GLA_TAU)
    la_hi = la.astype(jnp.bfloat16)
    la_lo = (la - la_hi.astype(jnp.float32)).astype(jnp.bfloat16)

    scale = GLA_DK ** -0.5
    gain = hg_ref[...]
    n = rows // c
    rs = [slice(ci * c, (ci + 1) * c) for ci in range(n)]
    n_lvl = len(GLA_LEVELS)

    sums = []
    for ci in range(n):
        la_stack = jnp.concatenate([la_hi[rs[ci]], la_lo[rs[ci]]], axis=0)
        sums.append(_dot(lm_ref[...], la_stack))
    cum = [s[0:c] for s in sums]

    def level_exponent(ci, li):
        m = GLA_LEVELS[li]
        if m < SUBLANES:
            lo = (1 + li - _GLA_N_VPU_LEVELS) * c
            return sums[ci][lo:lo + c]
        b = cum[ci]
        parts = []
        for start in range(0, c, 2 * m):
            mid = b[start + m:start + m + 1, :]
            parts.append(mid - b[start:start + m])
            parts.append(b[start + m:start + 2 * m] - mid)
        return jnp.concatenate(parts, axis=0)

    qf = [q_ref[r, :].astype(jnp.float32) for r in rs]
    kf = [k_ref[r, :].astype(jnp.float32) for r in rs]
    attn = [mk_ref[n_lvl] * _dot(q_ref[r, :], k_ref[r, :], _NT) for r in rs]
    for li in range(n_lvl):
        for ci in range(n):
            e = jnp.exp2(level_exponent(ci, li))
            ql = (qf[ci] * e).astype(jnp.bfloat16)
            kl = (kf[ci] * e).astype(jnp.bfloat16)
            attn[ci] = attn[ci] + mk_ref[li] * _dot(ql, kl, _NT)

    o_intra, qd, decay, upd = [], [], [], []
    for ci in range(n):
        b = cum[ci]
        b_last = b[c - 1:c, :]
        v = v_ref[rs[ci], :]
        kd = (kf[ci] * jnp.exp2(b_last - b)).astype(jnp.bfloat16)
        qd.append((qf[ci] * jnp.exp2(b)).astype(jnp.bfloat16))
        decay.append(jnp.exp2(b_last))
        upd.append(_dot(v, kd, _TN))
        o_intra.append(_dot(attn[ci].astype(jnp.bfloat16), v))

    st = st_scr[...]
    for ci in range(n):
        o = (o_intra[ci] + _dot(qd[ci], st.astype(jnp.bfloat16), _NT)) * scale
        st = st * decay[ci] + upd[ci]
        on = o * _rms_scale(o) * gain
        t = 0.5 * g_ref[rs[ci], :].astype(jnp.float32)
        o_ref[rs[ci], :] = (on * (t + t * jnp.tanh(t))).astype(o_ref.dtype)
    st_scr[...] = st


def _gla_core(qkvg, z, w_gate, b_gate, head_gain, batch, seq):
    t = qkvg.shape[0]
    nblk = seq // GLA_ROWS
    lm, mk = _gla_constants()
    lm = jnp.asarray(np.concatenate([lm, lm], axis=1), jnp.bfloat16)
    mk = jnp.asarray(mk, jnp.float32)
    kq = (GLA_HEADS * GLA_DK) // GLA_DK
    kv = (2 * GLA_HEADS * GLA_DK) // GLA_DV
    kg = kv + GLA_HEADS
    row = lambda b, h, s: b * nblk + s
    return pl.pallas_call(
        _gla_core_kernel,
        grid=(batch, GLA_HEADS, nblk),
        in_specs=[
            pl.BlockSpec((GLA_ROWS, GLA_DK), lambda b, h, s: (row(b, h, s), h)),
            pl.BlockSpec((GLA_ROWS, GLA_DK), lambda b, h, s: (row(b, h, s), kq + h)),
            pl.BlockSpec((GLA_ROWS, GLA_DV), lambda b, h, s: (row(b, h, s), kv + h)),
            pl.BlockSpec((GLA_ROWS, GLA_DV), lambda b, h, s: (row(b, h, s), kg + h)),
            pl.BlockSpec((GLA_ROWS, LANES), lambda b, h, s: (row(b, h, s), 0)),
            pl.BlockSpec((LANES, GLA_DK), lambda b, h, s: (0, h)),
            pl.BlockSpec((1, GLA_DK), lambda b, h, s: (0, h)),
            pl.BlockSpec((1, GLA_DV), lambda b, h, s: (0, 0)),
            pl.BlockSpec(lm.shape, lambda b, h, s: (0, 0)),
            pl.BlockSpec(mk.shape, lambda b, h, s: (0, 0, 0)),
        ],
        out_specs=pl.BlockSpec((GLA_ROWS, GLA_DV), lambda b, h, s: (row(b, h, s), h)),
        out_shape=jax.ShapeDtypeStruct((t, GLA_HEADS * GLA_DV), jnp.bfloat16),
        scratch_shapes=[pltpu.VMEM((GLA_DV, GLA_DK), jnp.float32)],
        compiler_params=_params("parallel", "parallel", "arbitrary"),
        name="gla_core",
    )(qkvg, qkvg, qkvg, qkvg, z, w_gate, b_gate, head_gain, lm, mk)


def _outproj_mlp_kernel(a_ref, h_ref, wo_ref, g_ref, w1_ref, w2_ref, o_ref):
    bf = jnp.bfloat16
    h1 = h_ref[...] + _dot(a_ref[...], wo_ref[...].astype(bf))
    xn = (h1 * _rms_scale(h1) * g_ref[...]).astype(bf)
    acc = h1
    d = h1.shape[1]
    for c0 in range(0, w1_ref.shape[1], d):
        hid = jnp.maximum(_dot(xn, w1_ref[:, c0:c0 + d].astype(bf)), 0.0)
        acc = acc + _dot((hid * hid).astype(bf), w2_ref[c0:c0 + d, :].astype(bf))
    o_ref[...] = acc


def _outproj_mlp(a, h, w_out, gain, w1, w2, layer_out, layer_mlp):
    t, d = h.shape
    tile = pl.BlockSpec((ROW_TILE, d), lambda i: (i, 0))
    return pl.pallas_call(
        _outproj_mlp_kernel,
        grid=(t // ROW_TILE,),
        in_specs=[tile, tile, _layer_resident(w_out, layer_out), _resident((1, d)),
                  _layer_resident(w1, layer_mlp), _layer_resident(w2, layer_mlp)],
        out_specs=tile,
        out_shape=jax.ShapeDtypeStruct((t, d), jnp.float32),
        compiler_params=_params("parallel"),
        name="outproj_mlp",
    )(a, h, w_out, gain, w1, w2)


def _segment_norm(y, seg_ones, gain):
    outs = []
    for c0 in range(0, y.shape[1], SEG_TILE):
        yc = y[:, c0:c0 + SEG_TILE]
        ss = _dot((yc * yc).astype(jnp.bfloat16), seg_ones)
        outs.append(yc * lax.rsqrt(ss * (1.0 / DIFF_DH) + NORM_EPS) * gain)
    return outs


def _diff_proj_kernel(h_ref, gkv_ref, gq_ref, wkv_ref, wq_ref, kg_ref, qg_ref, so_ref,
                      k_ref, v_ref, q_ref):
    nk = k_ref.shape[1]
    seg_ones = so_ref[...]
    bf = jnp.bfloat16
    qscale = DIFF_DH ** -0.5 * math.log2(math.e)
    x = h_ref[...]
    xhat = x * _rms_scale(x)
    xkv = (xhat * gkv_ref[...]).astype(bf)
    xq = (xhat * gq_ref[...]).astype(bf)
    kraw = _dot(xkv, wkv_ref[:, :nk].astype(bf))
    for i, blk in enumerate(_segment_norm(kraw, seg_ones, kg_ref[...])):
        k_ref[:, i * SEG_TILE:(i + 1) * SEG_TILE] = blk.astype(k_ref.dtype)
    v_ref[...] = _dot(xkv, wkv_ref[:, nk:].astype(bf)).astype(v_ref.dtype)
    qraw = _dot(xq, wq_ref[...].astype(bf))
    for i, blk in enumerate(_segment_norm(qraw, seg_ones, qg_ref[...])):
        q_ref[:, i * SEG_TILE:(i + 1) * SEG_TILE] = (blk * qscale).astype(q_ref.dtype)


def _diff_proj(h, g_kv, g_q, w_kv, w_q, k_gain, q_gain):
    t, d = h.shape
    nk = DIFF_HEADS * 2 * DIFF_DH
    nv = DIFF_HEADS * DIFF_VD
    lane = np.arange(SEG_TILE)
    seg_ones = jnp.asarray(lane[:, None] // DIFF_DH == lane[None, :] // DIFF_DH, jnp.bfloat16)
    tile = lambda n: pl.BlockSpec((PROJ_ROW_TILE, n), lambda i: (i, 0))
    return pl.pallas_call(
        _diff_proj_kernel,
        grid=(t // PROJ_ROW_TILE,),
        in_specs=[tile(d), _resident((1, d)), _resident((1, d)), _resident(w_kv.shape),
                  _layer_resident(w_q, 0), _resident((1, SEG_TILE)), _resident((1, SEG_TILE)),
                  _resident((SEG_TILE, SEG_TILE))],
        out_specs=[tile(nk), tile(nv), tile(nk)],
        out_shape=[jax.ShapeDtypeStruct((t, nk), jnp.bfloat16),
                   jax.ShapeDtypeStruct((t, nv), jnp.bfloat16),
                   jax.ShapeDtypeStruct((t, nk), jnp.bfloat16)],
        compiler_params=_params("parallel"),
        name="diff_proj",
    )(h, g_kv, g_q, w_kv, w_q, k_gain, q_gain, seg_ones)


def _diff_attn_kernel(q_ref, k_ref, v_ref, lam_ref, hg_ref, o_ref,
                      m_scr, l_scr, acc_scr, s_bufs, p_bufs, *, lambda_init):
    qi = pl.program_id(2)
    tq = q_ref.shape[0]
    tk = ATT_TK

    m_scr[...] = jnp.full_like(m_scr, NEG_BIG)
    l_scr[...] = jnp.zeros_like(l_scr)
    acc_scr[...] = jnp.zeros_like(acc_scr)

    q = q_ref[...]
    lane = lax.broadcasted_iota(jnp.int32, q.shape, 1)
    qcs = [jnp.where((lane // DIFF_DH) == c, q, jnp.zeros_like(q)) for c in range(2)]

    def block(j):
        return pl.ds(pl.multiple_of(j * tk, tk), tk)

    rb = ATT_ROW_BLOCK
    col_minus_row = (lax.broadcasted_iota(jnp.int32, (rb, tk), 1)
                     - lax.broadcasted_iota(jnp.int32, (rb, tk), 0))

    def step(j, buf, row_lo, diag_row0):
        k = k_ref[block(j), :]
        v = v_ref[block(j), :]
        s_scr = s_bufs.at[buf]
        p_scr = p_bufs.at[buf]
        all_rows = slice(row_lo, tq)
        for c in range(2):
            s_scr[c, all_rows, :] = _dot(qcs[c][all_rows], k, _NT)
        for c in range(2):
            for r0 in range(row_lo, tq, rb):
                rows = slice(r0, r0 + rb)
                masked = diag_row0 is not None and r0 - diag_row0 < tk
                kmax = min(tk, -(-(r0 - diag_row0 + rb) // LANES) * LANES) if masked else tk
                s = s_scr[c, rows, 0:kmax]
                if masked:
                    s = jnp.where(col_minus_row[:, 0:kmax] <= r0 - diag_row0, s, NEG_BIG)
                m_prev = m_scr[c, rows, :]
                m_new = jnp.maximum(m_prev, jnp.max(s, axis=-1, keepdims=True))
                alpha = jnp.exp2(m_prev - m_new)
                p = jnp.exp2(s - jnp.tile(m_new, (1, kmax // LANES)))
                psum = p[:, 0:LANES]
                for t0 in range(LANES, kmax, LANES):
                    psum = psum + p[:, t0:t0 + LANES]
                l_scr[c, rows, :] = alpha * l_scr[c, rows, :] + psum
                m_scr[c, rows, :] = m_new
                acc_scr[c, rows, :] = alpha * acc_scr[c, rows, :]
                p_scr[c, rows, 0:kmax] = p.astype(jnp.bfloat16)
                if kmax < tk:
                    p_scr[c, rows, kmax:tk] = jnp.zeros((rb, tk - kmax), jnp.bfloat16)
            acc_scr[c, all_rows, :] += _dot(p_scr[c, all_rows, :], v)

    n_sub = tq // tk

    def pair(i, carry):
        step(2 * i, 0, 0, None)
        step(2 * i + 1, 1, 0, None)
        return carry

    lax.fori_loop(0, (n_sub // 2) * qi, pair, 0)
    for t in range(n_sub):
        step(n_sub * qi + t, t % 2, t * tk, t * tk)

    lp = lam_ref[...]
    lam = (jnp.exp(jnp.sum(lp[0:1] * lp[1:2], axis=-1, keepdims=True))
           - jnp.exp(jnp.sum(lp[2:3] * lp[3:4], axis=-1, keepdims=True)) + lambda_init)
    l0 = jnp.sum(l_scr[0], axis=-1, keepdims=True)
    l1 = jnp.sum(l_scr[1], axis=-1, keepdims=True)
    o = acc_scr[0] / l0 - lam * (acc_scr[1] / l1)
    on = o * _rms_scale(o) * hg_ref[...]
    o_ref[...] = (on * (1.0 - lambda_init)).astype(o_ref.dtype)


def _diff_attn(q, k, v, lam_params, head_gain, batch, seq, lambda_init):
    t = q.shape[0]
    assert ATT_TQ % (2 * ATT_TK) == 0
    nq = seq // ATT_TQ
    kern = functools.partial(_diff_attn_kernel, lambda_init=lambda_init)
    return pl.pallas_call(
        kern,
        grid=(batch, DIFF_HEADS, nq),
        in_specs=[
            pl.BlockSpec((ATT_TQ, 2 * DIFF_DH), lambda b, h, i: (b * nq + i, h)),
            pl.BlockSpec((seq, 2 * DIFF_DH), lambda b, h, i: (b, h)),
            pl.BlockSpec((seq, DIFF_VD), lambda b, h, i: (b, h)),
            pl.BlockSpec(lam_params.shape, lambda b, h, i: (0, 0)),
            pl.BlockSpec((1, DIFF_VD), lambda b, h, i: (0, 0)),
        ],
        out_specs=pl.BlockSpec((ATT_TQ, DIFF_VD), lambda b, h, i: (b * nq + i, h)),
        out_shape=jax.ShapeDtypeStruct((t, DIFF_HEADS * DIFF_VD), jnp.bfloat16),
        scratch_shapes=[
            pltpu.VMEM((2, ATT_TQ, LANES), jnp.float32),
            pltpu.VMEM((2, ATT_TQ, LANES), jnp.float32),
            pltpu.VMEM((2, ATT_TQ, DIFF_VD), jnp.float32),
            pltpu.VMEM((2, 2, ATT_TQ, ATT_TK), jnp.float32),
            pltpu.VMEM((2, 2, ATT_TQ, ATT_TK), jnp.bfloat16),
        ],
        compiler_params=_params("parallel", "parallel", "arbitrary"),
        name="diff_attn",
    )(q, k, v, lam_params, head_gain)


def kernel(x, a_norm, a_w_in, a_w_gate_up, a_b_gate, a_head_norm, a_w_out, kv_norm, w_kv, k_norm,
           b_norm, b_w_q, b_q_norm, b_lambda, b_head_norm, b_w_out, mlp_norm, mlp_w1, mlp_w2):
    batch, seq, d = x.shape
    bf = jnp.bfloat16
    row = lambda p: p.reshape(1, -1)
    h = x.reshape(batch * seq, d)

    n_main = 2 * GLA_HEADS * GLA_DK + 2 * GLA_HEADS * GLA_DV
    w_z = jnp.pad(a_w_in[0, :, n_main:], ((0, 0), (0, LANES - GLA_RANK))).astype(bf)
    w_gate = jnp.pad(a_w_gate_up[0], ((0, LANES - GLA_RANK), (0, 0))).astype(bf)
    qkvg, z = _gla_inproj(h, row(a_norm[0]), a_w_in, w_z, n_main)
    og = _gla_core(qkvg, z, w_gate, row(a_b_gate[0]), row(a_head_norm[0]), batch, seq)
    h = _outproj_mlp(og, h, a_w_out, row(mlp_norm[0]), mlp_w1, mlp_w2, 0, 0)

    layer = 1
    lambda_init = 0.8 - 0.6 * math.exp(-0.3 * layer)
    reps = SEG_TILE // DIFF_DH
    kn, v, qn = _diff_proj(h, row(kv_norm), row(b_norm[0]), w_kv, b_w_q,
                           row(jnp.tile(k_norm, reps)), row(jnp.tile(b_q_norm[0], reps)))
    oa = _diff_attn(qn, kn, v, b_lambda[0], row(b_head_norm[0]), batch, seq, lambda_init)
    h = _outproj_mlp(oa, h, b_w_out, row(mlp_norm[1]), mlp_w1, mlp_w2, 0, 1)
    return h.reshape(batch, seq, d)
```

```python
import functools
import math

import numpy as np
import jax
import jax.numpy as jnp
from jax import lax
from jax.experimental import pallas as pl
from jax.experimental.pallas import tpu as pltpu

D_MODEL = 1024
GLA_HEADS = 4
GLA_DK = 128
GLA_DV = 256
GLA_RANK = 16
GLA_TAU = 16.0
GLA_CHUNK = 64
DIFF_HEADS = 8
DIFF_DH = 64
DIFF_VD = 128
MLP_HIDDEN = 4 * D_MODEL
NORM_EPS = 1e-6

LANES = 128
VMEM_LIMIT = 56 * 1024 * 1024
NEG_BIG = -1e30

ROW_TILE = 512
PROJ_ROW_TILE = 1024
PROJ_COL_CHUNK = 768
GLA_ROWS = 2048
SEG_TILE = 256
ATT_TQ = 1024
ATT_TK = 512
ATT_ROW_BLOCK = 32
GLA_LEVELS = (32, 16, 8, 4, 2, 1)
SUBLANES = 8
_GLA_N_VPU_LEVELS = sum(m >= SUBLANES for m in GLA_LEVELS)

_NT = (((1,), (1,)), ((), ()))
_TN = (((0,), (0,)), ((), ()))


def _dot(a, b, dims=None):
    if dims is None:
        return jnp.dot(a, b, preferred_element_type=jnp.float32)
    return lax.dot_general(a, b, dims, preferred_element_type=jnp.float32)


def _params(*sem):
    return pltpu.CompilerParams(dimension_semantics=sem, vmem_limit_bytes=VMEM_LIMIT)


def _resident(shape):
    nd = len(shape)
    return pl.BlockSpec(shape, lambda *_: (0,) * nd, pipeline_mode=pl.Buffered(1))


def _layer_resident(w, layer):
    return pl.BlockSpec((None,) + w.shape[1:], lambda *_: (layer, 0, 0),
                        pipeline_mode=pl.Buffered(1))


def _rms_scale(x):
    return lax.rsqrt(jnp.mean(x * x, axis=-1, keepdims=True) + NORM_EPS)


def _gla_inproj_kernel(x_ref, g_ref, w_ref, wz_ref, o_ref, z_ref):
    x = x_ref[...]
    xn = (x * _rms_scale(x) * g_ref[...]).astype(jnp.bfloat16)
    n_out = o_ref.shape[1]
    step = PROJ_COL_CHUNK
    for n0 in range(0, n_out, step):
        w = w_ref[:, n0:n0 + step].astype(jnp.bfloat16)
        o_ref[:, n0:n0 + step] = _dot(xn, w).astype(o_ref.dtype)
    z_ref[...] = _dot(xn, wz_ref[...])


def _gla_inproj(x, gain, w_in, w_z, n):
    t, d = x.shape
    return pl.pallas_call(
        _gla_inproj_kernel,
        grid=(t // PROJ_ROW_TILE,),
        in_specs=[
            pl.BlockSpec((PROJ_ROW_TILE, d), lambda i: (i, 0)),
            _resident((1, d)),
            _layer_resident(w_in, 0),
            _resident((d, LANES)),
        ],
        out_specs=[
            pl.BlockSpec((PROJ_ROW_TILE, n), lambda i: (i, 0)),
            pl.BlockSpec((PROJ_ROW_TILE, LANES), lambda i: (i, 0)),
        ],
        out_shape=[
            jax.ShapeDtypeStruct((t, n), jnp.bfloat16),
            jax.ShapeDtypeStruct((t, LANES), jnp.float32),
        ],
        compiler_params=_params("parallel"),
        name="gla_inproj",
    )(x, gain, w_in, w_z)


def _gla_constants():
    c = GLA_CHUNK
    t = np.arange(c)
    row, col = t[:, None], t[None, :]
    mats = [col <= row]
    masks = []
    for m in GLA_LEVELS:
        blk = 2 * m
        start = (t // blk) * blk
        mid = (start + m)[:, None]
        second = ((t % blk) >= m)[:, None]
        q_side = second & (col > mid) & (col <= row)
        k_side = (~second) & (col > row) & (col <= mid)
        if m < SUBLANES:
            mats.append(q_side | k_side)
        masks.append((start[:, None] == start[None, :]) & second & (~second).T)
    masks.append(np.eye(c, dtype=bool))
    lm = np.concatenate(mats, axis=0).astype(np.float32)
    mk = np.stack(masks, axis=0).astype(np.float32)
    return lm, mk


def _gla_core_kernel(q_ref, k_ref, v_ref, g_ref, z_ref, wg_ref, bg_ref, hg_ref,
                     lm_ref, mk_ref, o_ref, st_scr):
    c = GLA_CHUNK
    rows = q_ref.shape[0]

    @pl.when(pl.program_id(2) == 0)
    def _():
        st_scr[...] = jnp.zeros_like(st_scr)

    logits = _dot(z_ref[...].astype(jnp.bfloat16), wg_ref[...]) + bg_ref[...]
    log_sig = jnp.minimum(logits, 0.0) - jnp.log(1.0 + jnp.exp(-jnp.abs(logits)))
    la = log_sig * (math.log2(math.e) / GLA_TAU)
    la_hi = la.astype(jnp.bfloat16)
    la_lo = (la - la_hi.astype(jnp.float32)).astype(jnp.bfloat16)

    scale = GLA_DK ** -0.5
    gain = hg_ref[...]
    n = rows // c
    rs = [slice(ci * c, (ci + 1) * c) for ci in range(n)]
    n_lvl = len(GLA_LEVELS)

    sums = []
    for ci in range(n):
        la_stack = jnp.concatenate([la_hi[rs[ci]], la_lo[rs[ci]]], axis=0)
        sums.append(_dot(lm_ref[...], la_stack))
    cum = [s[0:c] for s in sums]

    def level_exponent(ci, li):
        m = GLA_LEVELS[li]
        if m < SUBLANES:
            lo = (1 + li - _GLA_N_VPU_LEVELS) * c
            return sums[ci][lo:lo + c]
        b = cum[ci]
        parts = []
        for start in range(0, c, 2 * m):
            mid = b[start + m:start + m + 1, :]
            parts.append(mid - b[start:start + m])
            parts.append(b[start + m:start + 2 * m] - mid)
        return jnp.concatenate(parts, axis=0)

    qf = [q_ref[r, :].astype(jnp.float32) for r in rs]
    kf = [k_ref[r, :].astype(jnp.float32) for r in rs]
    attn = [mk_ref[n_lvl] * _dot(q_ref[r, :], k_ref[r, :], _NT) for r in rs]
    for li in range(n_lvl):
        for ci in range(n):
            e = jnp.exp2(level_exponent(ci, li))
            ql = (qf[ci] * e).astype(jnp.bfloat16)
            kl = (kf[ci] * e).astype(jnp.bfloat16)
            attn[ci] = attn[ci] + mk_ref[li] * _dot(ql, kl, _NT)

    o_intra, qd, decay, upd = [], [], [], []
    for ci in range(n):
        b = cum[ci]
        b_last = b[c - 1:c, :]
        v = v_ref[rs[ci], :]
        kd = (kf[ci] * jnp.exp2(b_last - b)).astype(jnp.bfloat16)
        qd.append((qf[ci] * jnp.exp2(b)).astype(jnp.bfloat16))
        decay.append(jnp.exp2(b_last))
        upd.append(_dot(v, kd, _TN))
        o_intra.append(_dot(attn[ci].astype(jnp.bfloat16), v))

    st = st_scr[...]
    for ci in range(n):
        o = (o_intra[ci] + _dot(qd[ci], st.astype(jnp.bfloat16), _NT)) * scale
        st = st * decay[ci] + upd[ci]
        on = o * _rms_scale(o) * gain
        t = 0.5 * g_ref[rs[ci], :].astype(jnp.float32)
        o_ref[rs[ci], :] = (on * (t + t * jnp.tanh(t))).astype(o_ref.dtype)
    st_scr[...] = st


def _gla_core(qkvg, z, w_gate, b_gate, head_gain, batch, seq):
    t = qkvg.shape[0]
    nblk = seq // GLA_ROWS
    lm, mk = _gla_constants()
    lm = jnp.asarray(np.concatenate([lm, lm], axis=1), jnp.bfloat16)
    mk = jnp.asarray(mk, jnp.float32)
    kq = (GLA_HEADS * GLA_DK) // GLA_DK
    kv = (2 * GLA_HEADS * GLA_DK) // GLA_DV
    kg = kv + GLA_HEADS
    row = lambda b, h, s: b * nblk + s
    return pl.pallas_call(
        _gla_core_kernel,
        grid=(batch, GLA_HEADS, nblk),
        in_specs=[
            pl.BlockSpec((GLA_ROWS, GLA_DK), lambda b, h, s: (row(b, h, s), h)),
            pl.BlockSpec((GLA_ROWS, GLA_DK), lambda b, h, s: (row(b, h, s), kq + h)),
            pl.BlockSpec((GLA_ROWS, GLA_DV), lambda b, h, s: (row(b, h, s), kv + h)),
            pl.BlockSpec((GLA_ROWS, GLA_DV), lambda b, h, s: (row(b, h, s), kg + h)),
            pl.BlockSpec((GLA_ROWS, LANES), lambda b, h, s: (row(b, h, s), 0)),
            pl.BlockSpec((LANES, GLA_DK), lambda b, h, s: (0, h)),
            pl.BlockSpec((1, GLA_DK), lambda b, h, s: (0, h)),
            pl.BlockSpec((1, GLA_DV), lambda b, h, s: (0, 0)),
            pl.BlockSpec(lm.shape, lambda b, h, s: (0, 0)),
            pl.BlockSpec(mk.shape, lambda b, h, s: (0, 0, 0)),
        ],
        out_specs=pl.BlockSpec((GLA_ROWS, GLA_DV), lambda b, h, s: (row(b, h, s), h)),
        out_shape=jax.ShapeDtypeStruct((t, GLA_HEADS * GLA_DV), jnp.bfloat16),
        scratch_shapes=[pltpu.VMEM((GLA_DV, GLA_DK), jnp.float32)],
        compiler_params=_params("parallel", "parallel", "arbitrary"),
        name="gla_core",
    )(qkvg, qkvg, qkvg, qkvg, z, w_gate, b_gate, head_gain, lm, mk)


def _outproj_mlp_kernel(a_ref, h_ref, wo_ref, g_ref, w1_ref, w2_ref, o_ref):
    bf = jnp.bfloat16
    h1 = h_ref[...] + _dot(a_ref[...], wo_ref[...].astype(bf))
    xn = (h1 * _rms_scale(h1) * g_ref[...]).astype(bf)
    acc = h1
    d = h1.shape[1]
    for c0 in range(0, w1_ref.shape[1], d):
        hid = jnp.maximum(_dot(xn, w1_ref[:, c0:c0 + d].astype(bf)), 0.0)
        acc = acc + _dot((hid * hid).astype(bf), w2_ref[c0:c0 + d, :].astype(bf))
    o_ref[...] = acc


def _outproj_mlp(a, h, w_out, gain, w1, w2, layer_out, layer_mlp):
    t, d = h.shape
    tile = pl.BlockSpec((ROW_TILE, d), lambda i: (i, 0))
    return pl.pallas_call(
        _outproj_mlp_kernel,
        grid=(t // ROW_TILE,),
        in_specs=[tile, tile, _layer_resident(w_out, layer_out), _resident((1, d)),
                  _layer_resident(w1, layer_mlp), _layer_resident(w2, layer_mlp)],
        out_specs=tile,
        out_shape=jax.ShapeDtypeStruct((t, d), jnp.float32),
        compiler_params=_params("parallel"),
        name="outproj_mlp",
    )(a, h, w_out, gain, w1, w2)


def _segment_norm(y, seg_ones, gain):
    outs = []
    for c0 in range(0, y.shape[1], SEG_TILE):
        yc = y[:, c0:c0 + SEG_TILE]
        ss = _dot((yc * yc).astype(jnp.bfloat16), seg_ones)
        outs.append(yc * lax.rsqrt(ss * (1.0 / DIFF_DH) + NORM_EPS) * gain)
    return outs


def _diff_proj_kernel(h_ref, gkv_ref, gq_ref, wkv_ref, wq_ref, kg_ref, qg_ref, so_ref,
                      k_ref, v_ref, q_ref):
    nk = k_ref.shape[1]
    seg_ones = so_ref[...]
    bf = jnp.bfloat16
    qscale = DIFF_DH ** -0.5 * math.log2(math.e)
    x = h_ref[...]
    xhat = x * _rms_scale(x)
    xkv = (xhat * gkv_ref[...]).astype(bf)
    xq = (xhat * gq_ref[...]).astype(bf)
    kraw = _dot(xkv, wkv_ref[:, :nk].astype(bf))
    for i, blk in enumerate(_segment_norm(kraw, seg_ones, kg_ref[...])):
        k_ref[:, i * SEG_TILE:(i + 1) * SEG_TILE] = blk.astype(k_ref.dtype)
    v_ref[...] = _dot(xkv, wkv_ref[:, nk:].astype(bf)).astype(v_ref.dtype)
    qraw = _dot(xq, wq_ref[...].astype(bf))
    for i, blk in enumerate(_segment_norm(qraw, seg_ones, qg_ref[...])):
        q_ref[:, i * SEG_TILE:(i + 1) * SEG_TILE] = (blk * qscale).astype(q_ref.dtype)


def _diff_proj(h, g_kv, g_q, w_kv, w_q, k_gain, q_gain):
    t, d = h.shape
    nk = DIFF_HEADS * 2 * DIFF_DH
    nv = DIFF_HEADS * DIFF_VD
    lane = np.arange(SEG_TILE)
    seg_ones = jnp.asarray(lane[:, None] // DIFF_DH == lane[None, :] // DIFF_DH, jnp.bfloat16)
    tile = lambda n: pl.BlockSpec((PROJ_ROW_TILE, n), lambda i: (i, 0))
    return pl.pallas_call(
        _diff_proj_kernel,
        grid=(t // PROJ_ROW_TILE,),
        in_specs=[tile(d), _resident((1, d)), _resident((1, d)), _resident(w_kv.shape),
                  _layer_resident(w_q, 0), _resident((1, SEG_TILE)), _resident((1, SEG_TILE)),
                  _resident((SEG_TILE, SEG_TILE))],
        out_specs=[tile(nk), tile(nv), tile(nk)],
        out_shape=[jax.ShapeDtypeStruct((t, nk), jnp.bfloat16),
                   jax.ShapeDtypeStruct((t, nv), jnp.bfloat16),
                   jax.ShapeDtypeStruct((t, nk), jnp.bfloat16)],
        compiler_params=_params("parallel"),
        name="diff_proj",
    )(h, g_kv, g_q, w_kv, w_q, k_gain, q_gain, seg_ones)


def _diff_attn_kernel(q_ref, k_ref, v_ref, lam_ref, hg_ref, o_ref,
                      m_scr, l_scr, acc_scr, s_bufs, p_bufs, *, lambda_init):
    qi = pl.program_id(2)
    tq = q_ref.shape[0]
    tk = ATT_TK

    m_scr[...] = jnp.full_like(m_scr, NEG_BIG)
    l_scr[...] = jnp.zeros_like(l_scr)
    acc_scr[...] = jnp.zeros_like(acc_scr)

    q = q_ref[...]
    lane = lax.broadcasted_iota(jnp.int32, q.shape, 1)
    qcs = [jnp.where((lane // DIFF_DH) == c, q, jnp.zeros_like(q)) for c in range(2)]

    def block(j):
        return pl.ds(pl.multiple_of(j * tk, tk), tk)

    rb = ATT_ROW_BLOCK
    col_minus_row = (lax.broadcasted_iota(jnp.int32, (rb, tk), 1)
                     - lax.broadcasted_iota(jnp.int32, (rb, tk), 0))

    def step(j, buf, row_lo, diag_row0):
        k = k_ref[block(j), :]
        v = v_ref[block(j), :]
        s_scr = s_bufs.at[buf]
        p_scr = p_bufs.at[buf]
        all_rows = slice(row_lo, tq)
        for c in range(2):
            s_scr[c, all_rows, :] = _dot(qcs[c][all_rows], k, _NT)
        for c in range(2):
            for r0 in range(row_lo, tq, rb):
                rows = slice(r0, r0 + rb)
                masked = diag_row0 is not None and r0 - diag_row0 < tk
                kmax = min(tk, -(-(r0 - diag_row0 + rb) // LANES) * LANES) if masked else tk
                s = s_scr[c, rows, 0:kmax]
                if masked:
                    s = jnp.where(col_minus_row[:, 0:kmax] <= r0 - diag_row0, s, NEG_BIG)
                m_prev = m_scr[c, rows, :]
                m_new = jnp.maximum(m_prev, jnp.max(s, axis=-1, keepdims=True))
                alpha = jnp.exp2(m_prev - m_new)
                p = jnp.exp2(s - jnp.tile(m_new, (1, kmax // LANES)))
                psum = p[:, 0:LANES]
                for t0 in range(LANES, kmax, LANES):
                    psum = psum + p[:, t0:t0 + LANES]
                l_scr[c, rows, :] = alpha * l_scr[c, rows, :] + psum
                m_scr[c, rows, :] = m_new
                acc_scr[c, rows, :] = alpha * acc_scr[c, rows, :]
                p_scr[c, rows, 0:kmax] = p.astype(jnp.bfloat16)
                if kmax < tk:
                    p_scr[c, rows, kmax:tk] = jnp.zeros((rb, tk - kmax), jnp.bfloat16)
            acc_scr[c, all_rows, :] += _dot(p_scr[c, all_rows, :], v)

    n_sub = tq // tk

    def pair(i, carry):
        step(2 * i, 0, 0, None)
        step(2 * i + 1, 1, 0, None)
        return carry

    lax.fori_loop(0, (n_sub // 2) * qi, pair, 0)
    for t in range(n_sub):
        step(n_sub * qi + t, t % 2, t * tk, t * tk)

    lp = lam_ref[...]
    lam = (jnp.exp(jnp.sum(lp[0:1] * lp[1:2], axis=-1, keepdims=True))
           - jnp.exp(jnp.sum(lp[2:3] * lp[3:4], axis=-1, keepdims=True)) + lambda_init)
    l0 = jnp.sum(l_scr[0], axis=-1, keepdims=True)
    l1 = jnp.sum(l_scr[1], axis=-1, keepdims=True)
    o = acc_scr[0] / l0 - lam * (acc_scr[1] / l1)
    on = o * _rms_scale(o) * hg_ref[...]
    o_ref[...] = (on * (1.0 - lambda_init)).astype(o_ref.dtype)


def _diff_attn(q, k, v, lam_params, head_gain, batch, seq, lambda_init):
    t = q.shape[0]
    assert ATT_TQ % (2 * ATT_TK) == 0
    nq = seq // ATT_TQ
    kern = functools.partial(_diff_attn_kernel, lambda_init=lambda_init)
    return pl.pallas_call(
        kern,
        grid=(batch, DIFF_HEADS, nq),
        in_specs=[
            pl.BlockSpec((ATT_TQ, 2 * DIFF_DH), lambda b, h, i: (b * nq + i, h)),
            pl.BlockSpec((seq, 2 * DIFF_DH), lambda b, h, i: (b, h)),
            pl.BlockSpec((seq, DIFF_VD), lambda b, h, i: (b, h)),
            pl.BlockSpec(lam_params.shape, lambda b, h, i: (0, 0)),
            pl.BlockSpec((1, DIFF_VD), lambda b, h, i: (0, 0)),
        ],
        out_specs=pl.BlockSpec((ATT_TQ, DIFF_VD), lambda b, h, i: (b * nq + i, h)),
        out_shape=jax.ShapeDtypeStruct((t, DIFF_HEADS * DIFF_VD), jnp.bfloat16),
        scratch_shapes=[
            pltpu.VMEM((2, ATT_TQ, LANES), jnp.float32),
            pltpu.VMEM((2, ATT_TQ, LANES), jnp.float32),
            pltpu.VMEM((2, ATT_TQ, DIFF_VD), jnp.float32),
            pltpu.VMEM((2, 2, ATT_TQ, ATT_TK), jnp.float32),
            pltpu.VMEM((2, 2, ATT_TQ, ATT_TK), jnp.bfloat16),
        ],
        compiler_params=_params("parallel", "parallel", "arbitrary"),
        name="diff_attn",
    )(q, k, v, lam_params, head_gain)


def kernel(x, a_norm, a_w_in, a_w_gate_up, a_b_gate, a_head_norm, a_w_out, kv_norm, w_kv, k_norm,
           b_norm, b_w_q, b_q_norm, b_lambda, b_head_norm, b_w_out, mlp_norm, mlp_w1, mlp_w2):
    batch, seq, d = x.shape
    bf = jnp.bfloat16
    row = lambda p: p.reshape(1, -1)
    h = x.reshape(batch * seq, d)

    n_main = 2 * GLA_HEADS * GLA_DK + 2 * GLA_HEADS * GLA_DV
    w_z = jnp.pad(a_w_in[0, :, n_main:], ((0, 0), (0, LANES - GLA_RANK))).astype(bf)
    w_gate = jnp.pad(a_w_gate_up[0], ((0, LANES - GLA_RANK), (0, 0))).astype(bf)
    qkvg, z = _gla_inproj(h, row(a_norm[0]), a_w_in, w_z, n_main)
    og = _gla_core(qkvg, z, w_gate, row(a_b_gate[0]), row(a_head_norm[0]), batch, seq)
    h = _outproj_mlp(og, h, a_w_out, row(mlp_norm[0]), mlp_w1, mlp_w2, 0, 0)

    layer = 1
    lambda_init = 0.8 - 0.6 * math.exp(-0.3 * layer)
    reps = SEG_TILE // DIFF_DH
    kn, v, qn = _diff_proj(h, row(kv_norm), row(b_norm[0]), w_kv, b_w_q,
                           row(jnp.tile(k_norm, reps)), row(jnp.tile(b_q_norm[0], reps)))
    oa = _diff_attn(qn, kn, v, b_lambda[0], row(b_head_norm[0]), batch, seq, lambda_init)
    h = _outproj_mlp(oa, h, b_w_out, row(mlp_norm[1]), mlp_w1, mlp_w2, 0, 1)
    return h.reshape(batch, seq, d)
```

```python
import functools
import math

import numpy as np
import jax
import jax.numpy as jnp
from jax import lax
from jax.experimental import pallas as pl
from jax.experimental.pallas import tpu as pltpu

D_MODEL = 1024
GLA_HEADS = 4
GLA_DK = 128
GLA_DV = 256
GLA_RANK = 16
GLA_TAU = 16.0
GLA_CHUNK = 64
DIFF_HEADS = 8
DIFF_DH = 64
DIFF_VD = 128
MLP_HIDDEN = 4 * D_MODEL
NORM_EPS = 1e-6

LANES = 128
VMEM_LIMIT = 56 * 1024 * 1024
NEG_BIG = -1e30

ROW_TILE = 512
PROJ_ROW_TILE = 1024
PROJ_COL_CHUNK = 768
GLA_ROWS = 2048
SEG_TILE = 256
ATT_TQ = 1024
ATT_TK = 512
ATT_ROW_BLOCK = 32
GLA_LEVELS = (32, 16, 8, 4, 2, 1)
SUBLANES = 8
_GLA_N_VPU_LEVELS = sum(m >= SUBLANES for m in GLA_LEVELS)

_NT = (((1,), (1,)), ((), ()))
_TN = (((0,), (0,)), ((), ()))


def _dot(a, b, dims=None):
    if dims is None:
        return jnp.dot(a, b, preferred_element_type=jnp.float32)
    return lax.dot_general(a, b, dims, preferred_element_type=jnp.float32)


def _params(*sem):
    return pltpu.CompilerParams(dimension_semantics=sem, vmem_limit_bytes=VMEM_LIMIT)


def _resident(shape):
    nd = len(shape)
    return pl.BlockSpec(shape, lambda *_: (0,) * nd, pipeline_mode=pl.Buffered(1))


def _layer_resident(w, layer):
    return pl.BlockSpec((None,) + w.shape[1:], lambda *_: (layer, 0, 0),
                        pipeline_mode=pl.Buffered(1))


def _rms_scale(x):
    return lax.rsqrt(jnp.mean(x * x, axis=-1, keepdims=True) + NORM_EPS)


def _gla_inproj_kernel(x_ref, g_ref, w_ref, wz_ref, o_ref, z_ref):
    x = x_ref[...]
    xn = (x * _rms_scale(x) * g_ref[...]).astype(jnp.bfloat16)
    n_out = o_ref.shape[1]
    step = PROJ_COL_CHUNK
    for n0 in range(0, n_out, step):
        w = w_ref[:, n0:n0 + step].astype(jnp.bfloat16)
        o_ref[:, n0:n0 + step] = _dot(xn, w).astype(o_ref.dtype)
    z_ref[...] = _dot(xn, wz_ref[...])


def _gla_inproj(x, gain, w_in, w_z, n):
    t, d = x.shape
    return pl.pallas_call(
        _gla_inproj_kernel,
        grid=(t // PROJ_ROW_TILE,),
        in_specs=[
            pl.BlockSpec((PROJ_ROW_TILE, d), lambda i: (i, 0)),
            _resident((1, d)),
            _layer_resident(w_in, 0),
            _resident((d, LANES)),
        ],
        out_specs=[
            pl.BlockSpec((PROJ_ROW_TILE, n), lambda i: (i, 0)),
            pl.BlockSpec((PROJ_ROW_TILE, LANES), lambda i: (i, 0)),
        ],
        out_shape=[
            jax.ShapeDtypeStruct((t, n), jnp.bfloat16),
            jax.ShapeDtypeStruct((t, LANES), jnp.float32),
        ],
        compiler_params=_params("parallel"),
        name="gla_inproj",
    )(x, gain, w_in, w_z)


def _gla_constants():
    c = GLA_CHUNK
    t = np.arange(c)
    row, col = t[:, None], t[None, :]
    mats = [col <= row]
    masks = []
    for m in GLA_LEVELS:
        blk = 2 * m
        start = (t // blk) * blk
        mid = (start + m)[:, None]
        second = ((t % blk) >= m)[:, None]
        q_side = second & (col > mid) & (col <= row)
        k_side = (~second) & (col > row) & (col <= mid)
        if m < SUBLANES:
            mats.append(q_side | k_side)
        masks.append((start[:, None] == start[None, :]) & second & (~second).T)
    masks.append(np.eye(c, dtype=bool))
    lm = np.concatenate(mats, axis=0).astype(np.float32)
    mk = np.stack(masks, axis=0).astype(np.float32)
    return lm, mk


def _gla_core_kernel(q_ref, k_ref, v_ref, g_ref, z_ref, wg_ref, bg_ref, hg_ref,
                     lm_ref, mk_ref, o_ref, st_scr):
    c = GLA_CHUNK
    rows = q_ref.shape[0]

    @pl.when(pl.program_id(2) == 0)
    def _():
        st_scr[...] = jnp.zeros_like(st_scr)

    logits = _dot(z_ref[...].astype(jnp.bfloat16), wg_ref[...]) + bg_ref[...]
    log_sig = jnp.minimum(logits, 0.0) - jnp.log(1.0 + jnp.exp(-jnp.abs(logits)))
    la = log_sig * (math.log2(math.e) / GLA_TAU)
    la_hi = la.astype(jnp.bfloat16)
    la_lo = (la - la_hi.astype(jnp.float32)).astype(jnp.bfloat16)

    scale = GLA_DK ** -0.5
    gain = hg_ref[...]
    n = rows // c
    rs = [slice(ci * c, (ci + 1) * c) for ci in range(n)]
    n_lvl = len(GLA_LEVELS)

    sums = []
    for ci in range(n):
        la_stack = jnp.concatenate([la_hi[rs[ci]], la_lo[rs[ci]]], axis=0)
        sums.append(_dot(lm_ref[...], la_stack))
    cum = [s[0:c] for s in sums]

    def level_exponent(ci, li):
        m = GLA_LEVELS[li]
        if m < SUBLANES:
            lo = (1 + li - _GLA_N_VPU_LEVELS) * c
            return sums[ci][lo:lo + c]
        b = cum[ci]
        parts = []
        for start in range(0, c, 2 * m):
            mid = b[start + m:start + m + 1, :]
            parts.append(mid - b[start:start + m])
            parts.append(b[start + m:start + 2 * m] - mid)
        return jnp.concatenate(parts, axis=0)

    qf = [q_ref[r, :].astype(jnp.float32) for r in rs]
    kf = [k_ref[r, :].astype(jnp.float32) for r in rs]
    attn = [mk_ref[n_lvl] * _dot(q_ref[r, :], k_ref[r, :], _NT) for r in rs]
    for li in range(n_lvl):
        for ci in range(n):
            e = jnp.exp2(level_exponent(ci, li))
            ql = (qf[ci] * e).astype(jnp.bfloat16)
            kl = (kf[ci] * e).astype(jnp.bfloat16)
            attn[ci] = attn[ci] + mk_ref[li] * _dot(ql, kl, _NT)

    o_intra, qd, decay, upd = [], [], [], []
    for ci in range(n):
        b = cum[ci]
        b_last = b[c - 1:c, :]
        v = v_ref[rs[ci], :]
        kd = (kf[ci] * jnp.exp2(b_last - b)).astype(jnp.bfloat16)
        qd.append((qf[ci] * jnp.exp2(b)).astype(jnp.bfloat16))
        decay.append(jnp.exp2(b_last))
        upd.append(_dot(v, kd, _TN))
        o_intra.append(_dot(attn[ci].astype(jnp.bfloat16), v))

    st = st_scr[...]
    for ci in range(n):
        o = (o_intra[ci] + _dot(qd[ci], st.astype(jnp.bfloat16), _NT)) * scale
        st = st * decay[ci] + upd[ci]
        on = o * _rms_scale(o) * gain
        t = 0.5 * g_ref[rs[ci], :].astype(jnp.float32)
        o_ref[rs[ci], :] = (on * (t + t * jnp.tanh(t))).astype(o_ref.dtype)
    st_scr[...] = st


def _gla_core(qkvg, z, w_gate, b_gate, head_gain, batch, seq):
    t = qkvg.shape[0]
    nblk = seq // GLA_ROWS
    lm, mk = _gla_constants()
    lm = jnp.asarray(np.concatenate([lm, lm], axis=1), jnp.bfloat16)
    mk = jnp.asarray(mk, jnp.float32)
    kq = (GLA_HEADS * GLA_DK) // GLA_DK
    kv = (2 * GLA_HEADS * GLA_DK) // GLA_DV
    kg = kv + GLA_HEADS
    row = lambda b, h, s: b * nblk + s
    return pl.pallas_call(
        _gla_core_kernel,
        grid=(batch, GLA_HEADS, nblk),
        in_specs=[
            pl.BlockSpec((GLA_ROWS, GLA_DK), lambda b, h, s: (row(b, h, s), h)),
            pl.BlockSpec((GLA_ROWS, GLA_DK), lambda b, h, s: (row(b, h, s), kq + h)),
            pl.BlockSpec((GLA_ROWS, GLA_DV), lambda b, h, s: (row(b, h, s), kv + h)),
            pl.BlockSpec((GLA_ROWS, GLA_DV), lambda b, h, s: (row(b, h, s), kg + h)),
            pl.BlockSpec((GLA_ROWS, LANES), lambda b, h, s: (row(b, h, s), 0)),
            pl.BlockSpec((LANES, GLA_DK), lambda b, h, s: (0, h)),
            pl.BlockSpec((1, GLA_DK), lambda b, h, s: (0, h)),
            pl.BlockSpec((1, GLA_DV), lambda b, h, s: (0, 0)),
            pl.BlockSpec(lm.shape, lambda b, h, s: (0, 0)),
            pl.BlockSpec(mk.shape, lambda b, h, s: (0, 0, 0)),
        ],
        out_specs=pl.BlockSpec((GLA_ROWS, GLA_DV), lambda b, h, s: (row(b, h, s), h)),
        out_shape=jax.ShapeDtypeStruct((t, GLA_HEADS * GLA_DV), jnp.bfloat16),
        scratch_shapes=[pltpu.VMEM((GLA_DV, GLA_DK), jnp.float32)],
        compiler_params=_params("parallel", "parallel", "arbitrary"),
        name="gla_core",
    )(qkvg, qkvg, qkvg, qkvg, z, w_gate, b_gate, head_gain, lm, mk)


def _outproj_mlp_kernel(a_ref, h_ref, wo_ref, g_ref, w1_ref, w2_ref, o_ref):
    bf = jnp.bfloat16
    h1 = h_ref[...] + _dot(a_ref[...], wo_ref[...].astype(bf))
    xn = (h1 * _rms_scale(h1) * g_ref[...]).astype(bf)
    acc = h1
    d = h1.shape[1]
    for c0 in range(0, w1_ref.shape[1], d):
        hid = jnp.maximum(_dot(xn, w1_ref[:, c0:c0 + d].astype(bf)), 0.0)
        acc = acc + _dot((hid * hid).astype(bf), w2_ref[c0:c0 + d, :].astype(bf))
    o_ref[...] = acc


def _outproj_mlp(a, h, w_out, gain, w1, w2, layer_out, layer_mlp):
    t, d = h.shape
    tile = pl.BlockSpec((ROW_TILE, d), lambda i: (i, 0))
    return pl.pallas_call(
        _outproj_mlp_kernel,
        grid=(t // ROW_TILE,),
        in_specs=[tile, tile, _layer_resident(w_out, layer_out), _resident((1, d)),
                  _layer_resident(w1, layer_mlp), _layer_resident(w2, layer_mlp)],
        out_specs=tile,
        out_shape=jax.ShapeDtypeStruct((t, d), jnp.float32),
        compiler_params=_params("parallel"),
        name="outproj_mlp",
    )(a, h, w_out, gain, w1, w2)


def _segment_norm(y, seg_ones, gain):
    outs = []
    for c0 in range(0, y.shape[1], SEG_TILE):
        yc = y[:, c0:c0 + SEG_TILE]
        ss = _dot((yc * yc).astype(jnp.bfloat16), seg_ones)
        outs.append(yc * lax.rsqrt(ss * (1.0 / DIFF_DH) + NORM_EPS) * gain)
    return outs


def _diff_proj_kernel(h_ref, gkv_ref, gq_ref, wkv_ref, wq_ref, kg_ref, qg_ref, so_ref,
                      k_ref, v_ref, q_ref):
    nk = k_ref.shape[1]
    seg_ones = so_ref[...]
    bf = jnp.bfloat16
    qscale = DIFF_DH ** -0.5 * math.log2(math.e)
    x = h_ref[...]
    xhat = x * _rms_scale(x)
    xkv = (xhat * gkv_ref[...]).astype(bf)
    xq = (xhat * gq_ref[...]).astype(bf)
    kraw = _dot(xkv, wkv_ref[:, :nk].astype(bf))
    for i, blk in enumerate(_segment_norm(kraw, seg_ones, kg_ref[...])):
        k_ref[:, i * SEG_TILE:(i + 1) * SEG_TILE] = blk.astype(k_ref.dtype)
    v_ref[...] = _dot(xkv, wkv_ref[:, nk:].astype(bf)).astype(v_ref.dtype)
    qraw = _dot(xq, wq_ref[...].astype(bf))
    for i, blk in enumerate(_segment_norm(qraw, seg_ones, qg_ref[...])):
        q_ref[:, i * SEG_TILE:(i + 1) * SEG_TILE] = (blk * qscale).astype(q_ref.dtype)


def _diff_proj(h, g_kv, g_q, w_kv, w_q, k_gain, q_gain):
    t, d = h.shape
    nk = DIFF_HEADS * 2 * DIFF_DH
    nv = DIFF_HEADS * DIFF_VD
    lane = np.arange(SEG_TILE)
    seg_ones = jnp.asarray(lane[:, None] // DIFF_DH == lane[None, :] // DIFF_DH, jnp.bfloat16)
    tile = lambda n: pl.BlockSpec((PROJ_ROW_TILE, n), lambda i: (i, 0))
    return pl.pallas_call(
        _diff_proj_kernel,
        grid=(t // PROJ_ROW_TILE,),
        in_specs=[tile(d), _resident((1, d)), _resident((1, d)), _resident(w_kv.shape),
                  _layer_resident(w_q, 0), _resident((1, SEG_TILE)), _resident((1, SEG_TILE)),
                  _resident((SEG_TILE, SEG_TILE))],
        out_specs=[tile(nk), tile(nv), tile(nk)],
        out_shape=[jax.ShapeDtypeStruct((t, nk), jnp.bfloat16),
                   jax.ShapeDtypeStruct((t, nv), jnp.bfloat16),
                   jax.ShapeDtypeStruct((t, nk), jnp.bfloat16)],
        compiler_params=_params("parallel"),
        name="diff_proj",
    )(h, g_kv, g_q, w_kv, w_q, k_gain, q_gain, seg_ones)


def _diff_attn_kernel(q_ref, k_ref, v_ref, lam_ref, hg_ref, o_ref,
                      m_scr, l_scr, acc_scr, s_bufs, p_bufs, *, lambda_init):
    tq = ATT_TQ
    tk = ATT_TK
    rb = ATT_ROW_BLOCK
    lane = lax.broadcasted_iota(jnp.int32, (tq, q_ref.shape[1]), 1)
    col_minus_row = (lax.broadcasted_iota(jnp.int32, (rb, tk), 1)
                     - lax.broadcasted_iota(jnp.int32, (rb, tk), 0))

    def block(j):
        return pl.ds(pl.multiple_of(j * tk, tk), tk)

    def q_tile(qi, carry):
        q_rows = pl.ds(pl.multiple_of(qi * tq, tq), tq)
        m_scr[...] = jnp.full_like(m_scr, NEG_BIG)
        l_scr[...] = jnp.zeros_like(l_scr)
        acc_scr[...] = jnp.zeros_like(acc_scr)

        q = q_ref[q_rows, :]
        qcs = [jnp.where((lane // DIFF_DH) == c, q, jnp.zeros_like(q)) for c in range(2)]

        def step(j, buf, row_lo, diag_row0):
            k = k_ref[block(j), :]
            v = v_ref[block(j), :]
            s_scr = s_bufs.at[buf]
            p_scr = p_bufs.at[buf]
            all_rows = slice(row_lo, tq)
            for c in range(2):
                s_scr[c, all_rows, :] = _dot(qcs[c][all_rows], k, _NT)
            for c in range(2):
                for r0 in range(row_lo, tq, rb):
                    rows = slice(r0, r0 + rb)
                    masked = diag_row0 is not None and r0 - diag_row0 < tk
                    kmax = min(tk, -(-(r0 - diag_row0 + rb) // LANES) * LANES) if masked else tk
                    s = s_scr[c, rows, 0:kmax]
                    if masked:
                        s = jnp.where(col_minus_row[:, 0:kmax] <= r0 - diag_row0, s, NEG_BIG)
                    m_prev = m_scr[c, rows, :]
                    m_new = jnp.maximum(m_prev, jnp.max(s, axis=-1, keepdims=True))
                    alpha = jnp.exp2(m_prev - m_new)
                    p = jnp.exp2(s - jnp.tile(m_new, (1, kmax // LANES)))
                    psum = p[:, 0:LANES]
                    for t0 in range(LANES, kmax, LANES):
                        psum = psum + p[:, t0:t0 + LANES]
                    l_scr[c, rows, :] = alpha * l_scr[c, rows, :] + psum
                    m_scr[c, rows, :] = m_new
                    acc_scr[c, rows, :] = alpha * acc_scr[c, rows, :]
                    p_scr[c, rows, 0:kmax] = p.astype(jnp.bfloat16)
                    if kmax < tk:
                        p_scr[c, rows, kmax:tk] = jnp.zeros((rb, tk - kmax), jnp.bfloat16)
                acc_scr[c, all_rows, :] += _dot(p_scr[c, all_rows, :], v)

        n_sub = tq // tk

        def pair(i, carry):
            step(2 * i, 0, 0, None)
            step(2 * i + 1, 1, 0, None)
            return carry

        lax.fori_loop(0, (n_sub // 2) * qi, pair, 0)
        for t in range(n_sub):
            step(n_sub * qi + t, t % 2, t * tk, t * tk)

        lp = lam_ref[...]
        lam = (jnp.exp(jnp.sum(lp[0:1] * lp[1:2], axis=-1, keepdims=True))
               - jnp.exp(jnp.sum(lp[2:3] * lp[3:4], axis=-1, keepdims=True)) + lambda_init)
        l0 = jnp.sum(l_scr[0], axis=-1, keepdims=True)
        l1 = jnp.sum(l_scr[1], axis=-1, keepdims=True)
        o = acc_scr[0] / l0 - lam * (acc_scr[1] / l1)
        on = o * _rms_scale(o) * hg_ref[...]
        o_ref[q_rows, :] = (on * (1.0 - lambda_init)).astype(o_ref.dtype)
        return carry

    lax.fori_loop(0, q_ref.shape[0] // tq, q_tile, 0)


def _diff_attn(q, k, v, lam_params, head_gain, batch, seq, lambda_init):
    t = q.shape[0]
    assert ATT_TQ % (2 * ATT_TK) == 0
    assert seq % ATT_TQ == 0
    kern = functools.partial(_diff_attn_kernel, lambda_init=lambda_init)
    head_block = lambda width: pl.BlockSpec((seq, width), lambda b, h: (b, h))
    return pl.pallas_call(
        kern,
        grid=(batch, DIFF_HEADS),
        in_specs=[
            head_block(2 * DIFF_DH),
            head_block(2 * DIFF_DH),
            head_block(DIFF_VD),
            pl.BlockSpec(lam_params.shape, lambda b, h: (0, 0)),
            pl.BlockSpec((1, DIFF_VD), lambda b, h: (0, 0)),
        ],
        out_specs=head_block(DIFF_VD),
        out_shape=jax.ShapeDtypeStruct((t, DIFF_HEADS * DIFF_VD), jnp.bfloat16),
        scratch_shapes=[
            pltpu.VMEM((2, ATT_TQ, LANES), jnp.float32),
            pltpu.VMEM((2, ATT_TQ, LANES), jnp.float32),
            pltpu.VMEM((2, ATT_TQ, DIFF_VD), jnp.float32),
            pltpu.VMEM((2, 2, ATT_TQ, ATT_TK), jnp.float32),
            pltpu.VMEM((2, 2, ATT_TQ, ATT_TK), jnp.bfloat16),
        ],
        compiler_params=_params("parallel", "parallel"),
        name="diff_attn",
    )(q, k, v, lam_params, head_gain)


def kernel(x, a_norm, a_w_in, a_w_gate_up, a_b_gate, a_head_norm, a_w_out, kv_norm, w_kv, k_norm,
           b_norm, b_w_q, b_q_norm, b_lambda, b_head_norm, b_w_out, mlp_norm, mlp_w1, mlp_w2):
    batch, seq, d = x.shape
    bf = jnp.bfloat16
    row = lambda p: p.reshape(1, -1)
    h = x.reshape(batch * seq, d)

    n_main = 2 * GLA_HEADS * GLA_DK + 2 * GLA_HEADS * GLA_DV
    w_z = jnp.pad(a_w_in[0, :, n_main:], ((0, 0), (0, LANES - GLA_RANK))).astype(bf)
    w_gate = jnp.pad(a_w_gate_up[0], ((0, LANES - GLA_RANK), (0, 0))).astype(bf)
    qkvg, z = _gla_inproj(h, row(a_norm[0]), a_w_in, w_z, n_main)
    og = _gla_core(qkvg, z, w_gate, row(a_b_gate[0]), row(a_head_norm[0]), batch, seq)
    h = _outproj_mlp(og, h, a_w_out, row(mlp_norm[0]), mlp_w1, mlp_w2, 0, 0)

    layer = 1
    lambda_init = 0.8 - 0.6 * math.exp(-0.3 * layer)
    reps = SEG_TILE // DIFF_DH
    kn, v, qn = _diff_proj(h, row(kv_norm), row(b_norm[0]), w_kv, b_w_q,
                           row(jnp.tile(k_norm, reps)), row(jnp.tile(b_q_norm[0], reps)))
    oa = _diff_attn(qn, kn, v, b_lambda[0], row(b_head_norm[0]), batch, seq, lambda_init)
    h = _outproj_mlp(oa, h, b_w_out, row(mlp_norm[1]), mlp_w1, mlp_w2, 0, 1)
    return h.reshape(batch, seq, d)
```

```python
import functools
import math

import numpy as np
import jax
import jax.numpy as jnp
from jax import lax
from jax.experimental import pallas as pl
from jax.experimental.pallas import tpu as pltpu

D_MODEL = 1024
GLA_HEADS = 4
GLA_DK = 128
GLA_DV = 256
GLA_RANK = 16
GLA_TAU = 16.0
GLA_CHUNK = 64
DIFF_HEADS = 8
DIFF_DH = 64
DIFF_VD = 128
MLP_HIDDEN = 4 * D_MODEL
NORM_EPS = 1e-6

LANES = 128
VMEM_LIMIT = 56 * 1024 * 1024
NEG_BIG = -1e30

ROW_TILE = 512
PROJ_ROW_TILE = 1024
PROJ_COL_CHUNK = 768
GLA_ROWS = 2048
SEG_TILE = 256
ATT_TQ = 1024
ATT_TK = 512
ATT_ROW_BLOCK = 32
GLA_LEVELS = (32, 16, 8, 4, 2, 1)
SUBLANES = 8
_GLA_N_VPU_LEVELS = sum(m >= SUBLANES for m in GLA_LEVELS)

_NT = (((1,), (1,)), ((), ()))
_TN = (((0,), (0,)), ((), ()))


def _dot(a, b, dims=None):
    if dims is None:
        return jnp.dot(a, b, preferred_element_type=jnp.float32)
    return lax.dot_general(a, b, dims, preferred_element_type=jnp.float32)


def _params(*sem):
    return pltpu.CompilerParams(dimension_semantics=sem, vmem_limit_bytes=VMEM_LIMIT)


def _resident(shape):
    nd = len(shape)
    return pl.BlockSpec(shape, lambda *_: (0,) * nd, pipeline_mode=pl.Buffered(1))


def _layer_resident(w, layer):
    return pl.BlockSpec((None,) + w.shape[1:], lambda *_: (layer, 0, 0),
                        pipeline_mode=pl.Buffered(1))


def _rms_scale(x):
    return lax.rsqrt(jnp.mean(x * x, axis=-1, keepdims=True) + NORM_EPS)


def _gla_inproj_kernel(x_ref, g_ref, w_ref, wz_ref, o_ref, z_ref):
    x = x_ref[...]
    xn = (x * _rms_scale(x) * g_ref[...]).astype(jnp.bfloat16)
    n_out = o_ref.shape[1]
    step = PROJ_COL_CHUNK
    for n0 in range(0, n_out, step):
        w = w_ref[:, n0:n0 + step].astype(jnp.bfloat16)
        o_ref[:, n0:n0 + step] = _dot(xn, w).astype(o_ref.dtype)
    z_ref[...] = _dot(xn, wz_ref[...])


def _gla_inproj(x, gain, w_in, w_z, n):
    t, d = x.shape
    return pl.pallas_call(
        _gla_inproj_kernel,
        grid=(t // PROJ_ROW_TILE,),
        in_specs=[
            pl.BlockSpec((PROJ_ROW_TILE, d), lambda i: (i, 0)),
            _resident((1, d)),
            _layer_resident(w_in, 0),
            _resident((d, LANES)),
        ],
        out_specs=[
            pl.BlockSpec((PROJ_ROW_TILE, n), lambda i: (i, 0)),
            pl.BlockSpec((PROJ_ROW_TILE, LANES), lambda i: (i, 0)),
        ],
        out_shape=[
            jax.ShapeDtypeStruct((t, n), jnp.bfloat16),
            jax.ShapeDtypeStruct((t, LANES), jnp.float32),
        ],
        compiler_params=_params("parallel"),
        name="gla_inproj",
    )(x, gain, w_in, w_z)


def _gla_constants():
    c = GLA_CHUNK
    t = np.arange(c)
    row, col = t[:, None], t[None, :]
    mats = [col <= row]
    masks = []
    for m in GLA_LEVELS:
        blk = 2 * m
        start = (t // blk) * blk
        mid = (start + m)[:, None]
        second = ((t % blk) >= m)[:, None]
        q_side = second & (col > mid) & (col <= row)
        k_side = (~second) & (col > row) & (col <= mid)
        if m < SUBLANES:
            mats.append(q_side | k_side)
        masks.append((start[:, None] == start[None, :]) & second & (~second).T)
    masks.append(np.eye(c, dtype=bool))
    lm = np.concatenate(mats, axis=0).astype(np.float32)
    mk = np.stack(masks, axis=0).astype(np.float32)
    return lm, mk


def _gla_core_kernel(q_ref, k_ref, v_ref, g_ref, z_ref, wg_ref, bg_ref, hg_ref,
                     lm_ref, mk_ref, o_ref, st_scr):
    c = GLA_CHUNK
    rows = q_ref.shape[0]

    @pl.when(pl.program_id(2) == 0)
    def _():
        st_scr[...] = jnp.zeros_like(st_scr)

    logits = _dot(z_ref[...].astype(jnp.bfloat16), wg_ref[...]) + bg_ref[...]
    log_sig = jnp.minimum(logits, 0.0) - jnp.log(1.0 + jnp.exp(-jnp.abs(logits)))
    la = log_sig * (math.log2(math.e) / GLA_TAU)
    la_hi = la.astype(jnp.bfloat16)
    la_lo = (la - la_hi.astype(jnp.float32)).astype(jnp.bfloat16)

    scale = GLA_DK ** -0.5
    gain = hg_ref[...]
    n = rows // c
    rs = [slice(ci * c, (ci + 1) * c) for ci in range(n)]
    n_lvl = len(GLA_LEVELS)

    sums = []
    for ci in range(n):
        la_stack = jnp.concatenate([la_hi[rs[ci]], la_lo[rs[ci]]], axis=0)
        sums.append(_dot(lm_ref[...], la_stack))
    cum = [s[0:c] for s in sums]

    def level_exponent(ci, li):
        m = GLA_LEVELS[li]
        if m < SUBLANES:
            lo = (1 + li - _GLA_N_VPU_LEVELS) * c
            return sums[ci][lo:lo + c]
        b = cum[ci]
        parts = []
        for start in range(0, c, 2 * m):
            mid = b[start + m:start + m + 1, :]
            parts.append(mid - b[start:start + m])
            parts.append(b[start + m:start + 2 * m] - mid)
        return jnp.concatenate(parts, axis=0)

    qf = [q_ref[r, :].astype(jnp.float32) for r in rs]
    kf = [k_ref[r, :].astype(jnp.float32) for r in rs]
    attn = [mk_ref[n_lvl] * _dot(q_ref[r, :], k_ref[r, :], _NT) for r in rs]
    for li in range(n_lvl):
        for ci in range(n):
            e = jnp.exp2(level_exponent(ci, li))
            ql = (qf[ci] * e).astype(jnp.bfloat16)
            kl = (kf[ci] * e).astype(jnp.bfloat16)
            attn[ci] = attn[ci] + mk_ref[li] * _dot(ql, kl, _NT)

    o_intra, qd, decay, upd = [], [], [], []
    for ci in range(n):
        b = cum[ci]
        b_last = b[c - 1:c, :]
        v = v_ref[rs[ci], :]
        kd = (kf[ci] * jnp.exp2(b_last - b)).astype(jnp.bfloat16)
        qd.append((qf[ci] * jnp.exp2(b)).astype(jnp.bfloat16))
        decay.append(jnp.exp2(b_last))
        upd.append(_dot(v, kd, _TN))
        o_intra.append(_dot(attn[ci].astype(jnp.bfloat16), v))

    st = st_scr[...]
    for ci in range(n):
        o = (o_intra[ci] + _dot(qd[ci], st.astype(jnp.bfloat16), _NT)) * scale
        st = st * decay[ci] + upd[ci]
        on = o * _rms_scale(o) * gain
        t = 0.5 * g_ref[rs[ci], :].astype(jnp.float32)
        o_ref[rs[ci], :] = (on * (t + t * jnp.tanh(t))).astype(o_ref.dtype)
    st_scr[...] = st


def _gla_core(qkvg, z, w_gate, b_gate, head_gain, batch, seq):
    t = qkvg.shape[0]
    nblk = seq // GLA_ROWS
    lm, mk = _gla_constants()
    lm = jnp.asarray(np.concatenate([lm, lm], axis=1), jnp.bfloat16)
    mk = jnp.asarray(mk, jnp.float32)
    kq = (GLA_HEADS * GLA_DK) // GLA_DK
    kv = (2 * GLA_HEADS * GLA_DK) // GLA_DV
    kg = kv + GLA_HEADS
    row = lambda b, h, s: b * nblk + s
    return pl.pallas_call(
        _gla_core_kernel,
        grid=(batch, GLA_HEADS, nblk),
        in_specs=[
            pl.BlockSpec((GLA_ROWS, GLA_DK), lambda b, h, s: (row(b, h, s), h)),
            pl.BlockSpec((GLA_ROWS, GLA_DK), lambda b, h, s: (row(b, h, s), kq + h)),
            pl.BlockSpec((GLA_ROWS, GLA_DV), lambda b, h, s: (row(b, h, s), kv + h)),
            pl.BlockSpec((GLA_ROWS, GLA_DV), lambda b, h, s: (row(b, h, s), kg + h)),
            pl.BlockSpec((GLA_ROWS, LANES), lambda b, h, s: (row(b, h, s), 0)),
            pl.BlockSpec((LANES, GLA_DK), lambda b, h, s: (0, h)),
            pl.BlockSpec((1, GLA_DK), lambda b, h, s: (0, h)),
            pl.BlockSpec((1, GLA_DV), lambda b, h, s: (0, 0)),
            pl.BlockSpec(lm.shape, lambda b, h, s: (0, 0)),
            pl.BlockSpec(mk.shape, lambda b, h, s: (0, 0, 0)),
        ],
        out_specs=pl.BlockSpec((GLA_ROWS, GLA_DV), lambda b, h, s: (row(b, h, s), h)),
        out_shape=jax.ShapeDtypeStruct((t, GLA_HEADS * GLA_DV), jnp.bfloat16),
        scratch_shapes=[pltpu.VMEM((GLA_DV, GLA_DK), jnp.float32)],
        compiler_params=_params("parallel", "parallel", "arbitrary"),
        name="gla_core",
    )(qkvg, qkvg, qkvg, qkvg, z, w_gate, b_gate, head_gain, lm, mk)


def _outproj_mlp_kernel(a_ref, h_ref, wo_ref, g_ref, w1_ref, w2_ref, o_ref):
    bf = jnp.bfloat16
    h1 = h_ref[...] + _dot(a_ref[...], wo_ref[...].astype(bf))
    xn = (h1 * _rms_scale(h1) * g_ref[...]).astype(bf)
    acc = h1
    d = h1.shape[1]
    for c0 in range(0, w1_ref.shape[1], d):
        hid = jnp.maximum(_dot(xn, w1_ref[:, c0:c0 + d].astype(bf)), 0.0)
        acc = acc + _dot((hid * hid).astype(bf), w2_ref[c0:c0 + d, :].astype(bf))
    o_ref[...] = acc


def _outproj_mlp(a, h, w_out, gain, w1, w2, layer_out, layer_mlp):
    t, d = h.shape
    tile = pl.BlockSpec((ROW_TILE, d), lambda i: (i, 0))
    return pl.pallas_call(
        _outproj_mlp_kernel,
        grid=(t // ROW_TILE,),
        in_specs=[tile, tile, _layer_resident(w_out, layer_out), _resident((1, d)),
                  _layer_resident(w1, layer_mlp), _layer_resident(w2, layer_mlp)],
        out_specs=tile,
        out_shape=jax.ShapeDtypeStruct((t, d), jnp.float32),
        compiler_params=_params("parallel"),
        name="outproj_mlp",
    )(a, h, w_out, gain, w1, w2)


def _segment_norm(y, seg_ones, gain):
    ss = _dot((y * y).astype(jnp.bfloat16), seg_ones)
    return y * lax.rsqrt(ss * (1.0 / DIFF_DH) + NORM_EPS) * gain


def _diff_proj_kernel(h_ref, gkv_ref, gq_ref, wkv_ref, wq_ref, kg_ref, qg_ref, so_ref,
                      k_ref, v_ref, q_ref):
    nk = k_ref.shape[1]
    seg_ones = so_ref[...]
    bf = jnp.bfloat16
    qscale = DIFF_DH ** -0.5 * math.log2(math.e)
    x = h_ref[...]
    xhat = x * _rms_scale(x)
    xkv = (xhat * gkv_ref[...]).astype(bf)
    xq = (xhat * gq_ref[...]).astype(bf)
    def norm_store(out_ref, gain_ref, scale, cols, raw):
        y = _segment_norm(raw, seg_ones, gain_ref[...])
        out_ref[:, cols] = (y if scale is None else y * scale).astype(out_ref.dtype)

    pending = None
    for x_in, w_ref, out_ref, gain_ref, scale in ((xkv, wkv_ref, k_ref, kg_ref, None),
                                                 (xq, wq_ref, q_ref, qg_ref, qscale)):
        for c0 in range(0, nk, SEG_TILE):
            cols = slice(c0, c0 + SEG_TILE)
            raw = _dot(x_in, w_ref[:, cols].astype(bf))
            if pending is not None:
                norm_store(*pending)
            pending = (out_ref, gain_ref, scale, cols, raw)
        if out_ref is k_ref:
            v_ref[...] = _dot(xkv, wkv_ref[:, nk:].astype(bf)).astype(v_ref.dtype)
    norm_store(*pending)


def _diff_proj(h, g_kv, g_q, w_kv, w_q, k_gain, q_gain):
    t, d = h.shape
    nk = DIFF_HEADS * 2 * DIFF_DH
    nv = DIFF_HEADS * DIFF_VD
    lane = np.arange(SEG_TILE)
    seg_ones = jnp.asarray(lane[:, None] // DIFF_DH == lane[None, :] // DIFF_DH, jnp.bfloat16)
    tile = lambda n: pl.BlockSpec((PROJ_ROW_TILE, n), lambda i: (i, 0))
    return pl.pallas_call(
        _diff_proj_kernel,
        grid=(t // PROJ_ROW_TILE,),
        in_specs=[tile(d), _resident((1, d)), _resident((1, d)), _resident(w_kv.shape),
                  _layer_resident(w_q, 0), _resident((1, SEG_TILE)), _resident((1, SEG_TILE)),
                  _resident((SEG_TILE, SEG_TILE))],
        out_specs=[tile(nk), tile(nv), tile(nk)],
        out_shape=[jax.ShapeDtypeStruct((t, nk), jnp.bfloat16),
                   jax.ShapeDtypeStruct((t, nv), jnp.bfloat16),
                   jax.ShapeDtypeStruct((t, nk), jnp.bfloat16)],
        compiler_params=_params("parallel"),
        name="diff_proj",
    )(h, g_kv, g_q, w_kv, w_q, k_gain, q_gain, seg_ones)


def _diff_attn_kernel(q_ref, k_ref, v_ref, lam_ref, hg_ref, o_ref,
                      m_scr, l_scr, acc_scr, s_bufs, p_bufs, *, lambda_init):
    tq = ATT_TQ
    tk = ATT_TK
    rb = ATT_ROW_BLOCK
    lane = lax.broadcasted_iota(jnp.int32, (tq, q_ref.shape[1]), 1)
    col_minus_row = (lax.broadcasted_iota(jnp.int32, (rb, tk), 1)
                     - lax.broadcasted_iota(jnp.int32, (rb, tk), 0))

    def block(j):
        return pl.ds(pl.multiple_of(j * tk, tk), tk)

    def q_tile(qi, carry):
        q_rows = pl.ds(pl.multiple_of(qi * tq, tq), tq)
        m_scr[...] = jnp.full_like(m_scr, NEG_BIG)
        l_scr[...] = jnp.zeros_like(l_scr)
        acc_scr[...] = jnp.zeros_like(acc_scr)

        q = q_ref[q_rows, :]
        qcs = [jnp.where((lane // DIFF_DH) == c, q, jnp.zeros_like(q)) for c in range(2)]

        def step(j, buf, row_lo, diag_row0):
            k = k_ref[block(j), :]
            v = v_ref[block(j), :]
            s_scr = s_bufs.at[buf]
            p_scr = p_bufs.at[buf]
            all_rows = slice(row_lo, tq)
            for c in range(2):
                s_scr[c, all_rows, :] = _dot(qcs[c][all_rows], k, _NT)
            for c in range(2):
                for r0 in range(row_lo, tq, rb):
                    rows = slice(r0, r0 + rb)
                    masked = diag_row0 is not None and r0 - diag_row0 < tk
                    kmax = min(tk, -(-(r0 - diag_row0 + rb) // LANES) * LANES) if masked else tk
                    s = s_scr[c, rows, 0:kmax]
                    if masked:
                        s = jnp.where(col_minus_row[:, 0:kmax] <= r0 - diag_row0, s, NEG_BIG)
                    m_prev = m_scr[c, rows, :]
                    m_new = jnp.maximum(m_prev, jnp.max(s, axis=-1, keepdims=True))
                    alpha = jnp.exp2(m_prev - m_new)
                    p = jnp.exp2(s - jnp.tile(m_new, (1, kmax // LANES)))
                    psum = p[:, 0:LANES]
                    for t0 in range(LANES, kmax, LANES):
                        psum = psum + p[:, t0:t0 + LANES]
                    l_scr[c, rows, :] = alpha * l_scr[c, rows, :] + psum
                    m_scr[c, rows, :] = m_new
                    acc_scr[c, rows, :] = alpha * acc_scr[c, rows, :]
                    p_scr[c, rows, 0:kmax] = p.astype(jnp.bfloat16)
                    if kmax < tk:
                        p_scr[c, rows, kmax:tk] = jnp.zeros((rb, tk - kmax), jnp.bfloat16)
                acc_scr[c, all_rows, :] += _dot(p_scr[c, all_rows, :], v)

        n_sub = tq // tk

        def pair(i, carry):
            step(2 * i, 0, 0, None)
            step(2 * i + 1, 1, 0, None)
            return carry

        lax.fori_loop(0, (n_sub // 2) * qi, pair, 0)
        for t in range(n_sub):
            step(n_sub * qi + t, t % 2, t * tk, t * tk)

        lp = lam_ref[...]
        lam = (jnp.exp(jnp.sum(lp[0:1] * lp[1:2], axis=-1, keepdims=True))
               - jnp.exp(jnp.sum(lp[2:3] * lp[3:4], axis=-1, keepdims=True)) + lambda_init)
        l0 = jnp.sum(l_scr[0], axis=-1, keepdims=True)
        l1 = jnp.sum(l_scr[1], axis=-1, keepdims=True)
        o = acc_scr[0] / l0 - lam * (acc_scr[1] / l1)
        on = o * _rms_scale(o) * hg_ref[...]
        o_ref[q_rows, :] = (on * (1.0 - lambda_init)).astype(o_ref.dtype)
        return carry

    lax.fori_loop(0, q_ref.shape[0] // tq, q_tile, 0)


def _diff_attn(q, k, v, lam_params, head_gain, batch, seq, lambda_init):
    t = q.shape[0]
    assert ATT_TQ % (2 * ATT_TK) == 0
    assert seq % ATT_TQ == 0
    kern = functools.partial(_diff_attn_kernel, lambda_init=lambda_init)
    head_block = lambda width: pl.BlockSpec((seq, width), lambda b, h: (b, h))
    return pl.pallas_call(
        kern,
        grid=(batch, DIFF_HEADS),
        in_specs=[
            head_block(2 * DIFF_DH),
            head_block(2 * DIFF_DH),
            head_block(DIFF_VD),
            pl.BlockSpec(lam_params.shape, lambda b, h: (0, 0)),
            pl.BlockSpec((1, DIFF_VD), lambda b, h: (0, 0)),
        ],
        out_specs=head_block(DIFF_VD),
        out_shape=jax.ShapeDtypeStruct((t, DIFF_HEADS * DIFF_VD), jnp.bfloat16),
        scratch_shapes=[
            pltpu.VMEM((2, ATT_TQ, LANES), jnp.float32),
            pltpu.VMEM((2, ATT_TQ, LANES), jnp.float32),
            pltpu.VMEM((2, ATT_TQ, DIFF_VD), jnp.float32),
            pltpu.VMEM((2, 2, ATT_TQ, ATT_TK), jnp.float32),
            pltpu.VMEM((2, 2, ATT_TQ, ATT_TK), jnp.bfloat16),
        ],
        compiler_params=_params("parallel", "parallel"),
        name="diff_attn",
    )(q, k, v, lam_params, head_gain)


def kernel(x, a_norm, a_w_in, a_w_gate_up, a_b_gate, a_head_norm, a_w_out, kv_norm, w_kv, k_norm,
           b_norm, b_w_q, b_q_norm, b_lambda, b_head_norm, b_w_out, mlp_norm, mlp_w1, mlp_w2):
    batch, seq, d = x.shape
    bf = jnp.bfloat16
    row = lambda p: p.reshape(1, -1)
    h = x.reshape(batch * seq, d)

    n_main = 2 * GLA_HEADS * GLA_DK + 2 * GLA_HEADS * GLA_DV
    w_z = jnp.pad(a_w_in[0, :, n_main:], ((0, 0), (0, LANES - GLA_RANK))).astype(bf)
    w_gate = jnp.pad(a_w_gate_up[0], ((0, LANES - GLA_RANK), (0, 0))).astype(bf)
    qkvg, z = _gla_inproj(h, row(a_norm[0]), a_w_in, w_z, n_main)
    og = _gla_core(qkvg, z, w_gate, row(a_b_gate[0]), row(a_head_norm[0]), batch, seq)
    h = _outproj_mlp(og, h, a_w_out, row(mlp_norm[0]), mlp_w1, mlp_w2, 0, 0)

    layer = 1
    lambda_init = 0.8 - 0.6 * math.exp(-0.3 * layer)
    reps = SEG_TILE // DIFF_DH
    kn, v, qn = _diff_proj(h, row(kv_norm), row(b_norm[0]), w_kv, b_w_q,
                           row(jnp.tile(k_norm, reps)), row(jnp.tile(b_q_norm[0], reps)))
    oa = _diff_attn(qn, kn, v, b_lambda[0], row(b_head_norm[0]), batch, seq, lambda_init)
    h = _outproj_mlp(oa, h, b_w_out, row(mlp_norm[1]), mlp_w1, mlp_w2, 0, 1)
    return h.reshape(batch, seq, d)
```

```python
import functools
import math

import numpy as np
import jax
import jax.numpy as jnp
from jax import lax
from jax.experimental import pallas as pl
from jax.experimental.pallas import tpu as pltpu

D_MODEL = 1024
GLA_HEADS = 4
GLA_DK = 128
GLA_DV = 256
GLA_RANK = 16
GLA_TAU = 16.0
GLA_CHUNK = 64
DIFF_HEADS = 8
DIFF_DH = 64
DIFF_VD = 128
MLP_HIDDEN = 4 * D_MODEL
NORM_EPS = 1e-6

LANES = 128
VMEM_LIMIT = 56 * 1024 * 1024
NEG_BIG = -1e30

ROW_TILE = 512
PROJ_ROW_TILE = 1024
PROJ_COL_CHUNK = 768
GLA_ROWS = 2048
SEG_TILE = 256
ATT_TQ = 1024
ATT_TK = 512
ATT_ROW_BLOCK = 32
GLA_LEVELS = (32, 16, 8, 4, 2, 1)
SUBLANES = 8
_GLA_N_VPU_LEVELS = sum(m >= SUBLANES for m in GLA_LEVELS)

_NT = (((1,), (1,)), ((), ()))
_TN = (((0,), (0,)), ((), ()))


def _dot(a, b, dims=None):
    if dims is None:
        return jnp.dot(a, b, preferred_element_type=jnp.float32)
    return lax.dot_general(a, b, dims, preferred_element_type=jnp.float32)


def _params(*sem):
    return pltpu.CompilerParams(dimension_semantics=sem, vmem_limit_bytes=VMEM_LIMIT)


def _resident(shape):
    nd = len(shape)
    return pl.BlockSpec(shape, lambda *_: (0,) * nd, pipeline_mode=pl.Buffered(1))


def _layer_resident(w, layer):
    return pl.BlockSpec((None,) + w.shape[1:], lambda *_: (layer, 0, 0),
                        pipeline_mode=pl.Buffered(1))


def _rms_scale(x):
    return lax.rsqrt(jnp.mean(x * x, axis=-1, keepdims=True) + NORM_EPS)


def _gla_inproj_kernel(x_ref, g_ref, w_ref, wz_ref, o_ref, z_ref):
    x = x_ref[...]
    xn = (x * _rms_scale(x) * g_ref[...]).astype(jnp.bfloat16)
    n_out = o_ref.shape[1]
    step = PROJ_COL_CHUNK
    for n0 in range(0, n_out, step):
        w = w_ref[n0:n0 + step, :].astype(jnp.bfloat16)
        o_ref[:, n0:n0 + step] = _dot(xn, w, _NT).astype(o_ref.dtype)
    z_ref[...] = _dot(xn, wz_ref[...])


def _gla_inproj(x, gain, w_in, w_z, n):
    t, d = x.shape
    return pl.pallas_call(
        _gla_inproj_kernel,
        grid=(t // PROJ_ROW_TILE,),
        in_specs=[
            pl.BlockSpec((PROJ_ROW_TILE, d), lambda i: (i, 0)),
            _resident((1, d)),
            _layer_resident(w_in, 0),
            _resident((d, LANES)),
        ],
        out_specs=[
            pl.BlockSpec((PROJ_ROW_TILE, n), lambda i: (i, 0)),
            pl.BlockSpec((PROJ_ROW_TILE, LANES), lambda i: (i, 0)),
        ],
        out_shape=[
            jax.ShapeDtypeStruct((t, n), jnp.bfloat16),
            jax.ShapeDtypeStruct((t, LANES), jnp.float32),
        ],
        compiler_params=_params("parallel"),
        name="gla_inproj",
    )(x, gain, w_in, w_z)


def _gla_constants():
    c = GLA_CHUNK
    t = np.arange(c)
    row, col = t[:, None], t[None, :]
    mats = [col <= row]
    masks = []
    for m in GLA_LEVELS:
        blk = 2 * m
        start = (t // blk) * blk
        mid = (start + m)[:, None]
        second = ((t % blk) >= m)[:, None]
        q_side = second & (col > mid) & (col <= row)
        k_side = (~second) & (col > row) & (col <= mid)
        if m < SUBLANES:
            mats.append(q_side | k_side)
        masks.append((start[:, None] == start[None, :]) & second & (~second).T)
    masks.append(np.eye(c, dtype=bool))
    lm = np.concatenate(mats, axis=0).astype(np.float32)
    mk = np.stack(masks, axis=0).astype(np.float32)
    return lm, mk


def _gla_core_kernel(q_ref, k_ref, v_ref, g_ref, z_ref, wg_ref, bg_ref, hg_ref,
                     lm_ref, mk_ref, o_ref, st_scr):
    c = GLA_CHUNK
    rows = q_ref.shape[0]

    @pl.when(pl.program_id(2) == 0)
    def _():
        st_scr[...] = jnp.zeros_like(st_scr)

    logits = _dot(z_ref[...].astype(jnp.bfloat16), wg_ref[...]) + bg_ref[...]
    log_sig = jnp.minimum(logits, 0.0) - jnp.log(1.0 + jnp.exp(-jnp.abs(logits)))
    la = log_sig * (math.log2(math.e) / GLA_TAU)
    la_hi = la.astype(jnp.bfloat16)
    la_lo = (la - la_hi.astype(jnp.float32)).astype(jnp.bfloat16)

    scale = GLA_DK ** -0.5
    gain = hg_ref[...]
    n = rows // c
    rs = [slice(ci * c, (ci + 1) * c) for ci in range(n)]
    n_lvl = len(GLA_LEVELS)

    sums = []
    for ci in range(n):
        la_stack = jnp.concatenate([la_hi[rs[ci]], la_lo[rs[ci]]], axis=0)
        sums.append(_dot(lm_ref[...], la_stack))
    cum = [s[0:c] for s in sums]

    def level_exponent(ci, li):
        m = GLA_LEVELS[li]
        if m < SUBLANES:
            lo = (1 + li - _GLA_N_VPU_LEVELS) * c
            return sums[ci][lo:lo + c]
        b = cum[ci]
        parts = []
        for start in range(0, c, 2 * m):
            mid = b[start + m:start + m + 1, :]
            parts.append(mid - b[start:start + m])
            parts.append(b[start + m:start + 2 * m] - mid)
        return jnp.concatenate(parts, axis=0)

    qf = [q_ref[r, :].astype(jnp.float32) for r in rs]
    kf = [k_ref[r, :].astype(jnp.float32) for r in rs]
    attn = [mk_ref[n_lvl] * _dot(q_ref[r, :], k_ref[r, :], _NT) for r in rs]
    for li in range(n_lvl):
        for ci in range(n):
            e = jnp.exp2(level_exponent(ci, li))
            ql = (qf[ci] * e).astype(jnp.bfloat16)
            kl = (kf[ci] * e).astype(jnp.bfloat16)
            attn[ci] = attn[ci] + mk_ref[li] * _dot(ql, kl, _NT)

    o_intra, qd, decay, upd = [], [], [], []
    for ci in range(n):
        b = cum[ci]
        b_last = b[c - 1:c, :]
        v = v_ref[rs[ci], :]
        kd = (kf[ci] * jnp.exp2(b_last - b)).astype(jnp.bfloat16)
        qd.append((qf[ci] * jnp.exp2(b)).astype(jnp.bfloat16))
        decay.append(jnp.exp2(b_last))
        upd.append(_dot(v, kd, _TN))
        o_intra.append(_dot(attn[ci].astype(jnp.bfloat16), v))

    st = st_scr[...]
    for ci in range(n):
        o = (o_intra[ci] + _dot(qd[ci], st.astype(jnp.bfloat16), _NT)) * scale
        st = st * decay[ci] + upd[ci]
        on = o * _rms_scale(o) * gain
        t = 0.5 * g_ref[rs[ci], :].astype(jnp.float32)
        o_ref[rs[ci], :] = (on * (t + t * jnp.tanh(t))).astype(o_ref.dtype)
    st_scr[...] = st


def _gla_core(qkvg, z, w_gate, b_gate, head_gain, batch, seq):
    t = qkvg.shape[0]
    nblk = seq // GLA_ROWS
    lm, mk = _gla_constants()
    lm = jnp.asarray(np.concatenate([lm, lm], axis=1), jnp.bfloat16)
    mk = jnp.asarray(mk, jnp.float32)
    kq = (GLA_HEADS * GLA_DK) // GLA_DK
    kv = (2 * GLA_HEADS * GLA_DK) // GLA_DV
    kg = kv + GLA_HEADS
    row = lambda b, h, s: b * nblk + s
    return pl.pallas_call(
        _gla_core_kernel,
        grid=(batch, GLA_HEADS, nblk),
        in_specs=[
            pl.BlockSpec((GLA_ROWS, GLA_DK), lambda b, h, s: (row(b, h, s), h)),
            pl.BlockSpec((GLA_ROWS, GLA_DK), lambda b, h, s: (row(b, h, s), kq + h)),
            pl.BlockSpec((GLA_ROWS, GLA_DV), lambda b, h, s: (row(b, h, s), kv + h)),
            pl.BlockSpec((GLA_ROWS, GLA_DV), lambda b, h, s: (row(b, h, s), kg + h)),
            pl.BlockSpec((GLA_ROWS, LANES), lambda b, h, s: (row(b, h, s), 0)),
            pl.BlockSpec((LANES, GLA_DK), lambda b, h, s: (0, h)),
            pl.BlockSpec((1, GLA_DK), lambda b, h, s: (0, h)),
            pl.BlockSpec((1, GLA_DV), lambda b, h, s: (0, 0)),
            pl.BlockSpec(lm.shape, lambda b, h, s: (0, 0)),
            pl.BlockSpec(mk.shape, lambda b, h, s: (0, 0, 0)),
        ],
        out_specs=pl.BlockSpec((GLA_ROWS, GLA_DV), lambda b, h, s: (row(b, h, s), h)),
        out_shape=jax.ShapeDtypeStruct((t, GLA_HEADS * GLA_DV), jnp.bfloat16),
        scratch_shapes=[pltpu.VMEM((GLA_DV, GLA_DK), jnp.float32)],
        compiler_params=_params("parallel", "parallel", "arbitrary"),
        name="gla_core",
    )(qkvg, qkvg, qkvg, qkvg, z, w_gate, b_gate, head_gain, lm, mk)


def _outproj_mlp_kernel(a_ref, h_ref, wo_ref, g_ref, w1_ref, w2_ref, o_ref):
    bf = jnp.bfloat16
    h1 = h_ref[...] + _dot(a_ref[...], wo_ref[...].astype(bf))
    xn = (h1 * _rms_scale(h1) * g_ref[...]).astype(bf)
    acc = h1
    d = h1.shape[1]
    for c0 in range(0, w1_ref.shape[1], d):
        hid = jnp.maximum(_dot(xn, w1_ref[:, c0:c0 + d].astype(bf)), 0.0)
        acc = acc + _dot((hid * hid).astype(bf), w2_ref[c0:c0 + d, :].astype(bf))
    o_ref[...] = acc


def _outproj_mlp(a, h, w_out, gain, w1, w2, layer_out, layer_mlp):
    t, d = h.shape
    tile = pl.BlockSpec((ROW_TILE, d), lambda i: (i, 0))
    return pl.pallas_call(
        _outproj_mlp_kernel,
        grid=(t // ROW_TILE,),
        in_specs=[tile, tile, _layer_resident(w_out, layer_out), _resident((1, d)),
                  _layer_resident(w1, layer_mlp), _layer_resident(w2, layer_mlp)],
        out_specs=tile,
        out_shape=jax.ShapeDtypeStruct((t, d), jnp.float32),
        compiler_params=_params("parallel"),
        name="outproj_mlp",
    )(a, h, w_out, gain, w1, w2)


def _segment_norm(y, seg_ones, gain):
    ss = _dot((y * y).astype(jnp.bfloat16), seg_ones)
    return y * lax.rsqrt(ss * (1.0 / DIFF_DH) + NORM_EPS) * gain


def _diff_proj_kernel(h_ref, gkv_ref, gq_ref, wkv_ref, wq_ref, kg_ref, qg_ref, so_ref,
                      k_ref, v_ref, q_ref):
    nk = k_ref.shape[1]
    seg_ones = so_ref[...]
    bf = jnp.bfloat16
    qscale = DIFF_DH ** -0.5 * math.log2(math.e)
    x = h_ref[...]
    xhat = x * _rms_scale(x)
    xkv = (xhat * gkv_ref[...]).astype(bf)
    xq = (xhat * gq_ref[...]).astype(bf)
    def norm_store(out_ref, gain_ref, scale, cols, raw):
        y = _segment_norm(raw, seg_ones, gain_ref[...])
        out_ref[:, cols] = (y if scale is None else y * scale).astype(out_ref.dtype)

    pending = None
    for x_in, w_ref, out_ref, gain_ref, scale in ((xkv, wkv_ref, k_ref, kg_ref, None),
                                                 (xq, wq_ref, q_ref, qg_ref, qscale)):
        for c0 in range(0, nk, SEG_TILE):
            cols = slice(c0, c0 + SEG_TILE)
            raw = _dot(x_in, w_ref[:, cols].astype(bf))
            if pending is not None:
                norm_store(*pending)
            pending = (out_ref, gain_ref, scale, cols, raw)
        if out_ref is k_ref:
            v_ref[...] = _dot(xkv, wkv_ref[:, nk:].astype(bf)).astype(v_ref.dtype)
    norm_store(*pending)


def _diff_proj(h, g_kv, g_q, w_kv, w_q, k_gain, q_gain):
    t, d = h.shape
    nk = DIFF_HEADS * 2 * DIFF_DH
    nv = DIFF_HEADS * DIFF_VD
    lane = np.arange(SEG_TILE)
    seg_ones = jnp.asarray(lane[:, None] // DIFF_DH == lane[None, :] // DIFF_DH, jnp.bfloat16)
    tile = lambda n: pl.BlockSpec((PROJ_ROW_TILE, n), lambda i: (i, 0))
    return pl.pallas_call(
        _diff_proj_kernel,
        grid=(t // PROJ_ROW_TILE,),
        in_specs=[tile(d), _resident((1, d)), _resident((1, d)), _resident(w_kv.shape),
                  _layer_resident(w_q, 0), _resident((1, SEG_TILE)), _resident((1, SEG_TILE)),
                  _resident((SEG_TILE, SEG_TILE))],
        out_specs=[tile(nk), tile(nv), tile(nk)],
        out_shape=[jax.ShapeDtypeStruct((t, nk), jnp.bfloat16),
                   jax.ShapeDtypeStruct((t, nv), jnp.bfloat16),
                   jax.ShapeDtypeStruct((t, nk), jnp.bfloat16)],
        compiler_params=_params("parallel"),
        name="diff_proj",
    )(h, g_kv, g_q, w_kv, w_q, k_gain, q_gain, seg_ones)


def _diff_attn_kernel(q_ref, k_ref, v_ref, lam_ref, hg_ref, o_ref,
                      m_scr, l_scr, acc_scr, s_bufs, p_bufs, *, lambda_init):
    tq = ATT_TQ
    tk = ATT_TK
    rb = ATT_ROW_BLOCK
    lane = lax.broadcasted_iota(jnp.int32, (tq, q_ref.shape[1]), 1)
    col_minus_row = (lax.broadcasted_iota(jnp.int32, (rb, tk), 1)
                     - lax.broadcasted_iota(jnp.int32, (rb, tk), 0))

    def block(j):
        return pl.ds(pl.multiple_of(j * tk, tk), tk)

    def q_tile(qi, carry):
        q_rows = pl.ds(pl.multiple_of(qi * tq, tq), tq)
        m_scr[...] = jnp.full_like(m_scr, NEG_BIG)
        l_scr[...] = jnp.zeros_like(l_scr)
        acc_scr[...] = jnp.zeros_like(acc_scr)

        q = q_ref[q_rows, :]
        qcs = [jnp.where((lane // DIFF_DH) == c, q, jnp.zeros_like(q)) for c in range(2)]

        def step(j, buf, row_lo, diag_row0):
            k = k_ref[block(j), :]
            v = v_ref[block(j), :]
            s_scr = s_bufs.at[buf]
            p_scr = p_bufs.at[buf]
            all_rows = slice(row_lo, tq)
            for c in range(2):
                s_scr[c, all_rows, :] = _dot(qcs[c][all_rows], k, _NT)
            for c in range(2):
                for r0 in range(row_lo, tq, rb):
                    rows = slice(r0, r0 + rb)
                    masked = diag_row0 is not None and r0 - diag_row0 < tk
                    kmax = min(tk, -(-(r0 - diag_row0 + rb) // LANES) * LANES) if masked else tk
                    s = s_scr[c, rows, 0:kmax]
                    if masked:
                        s = jnp.where(col_minus_row[:, 0:kmax] <= r0 - diag_row0, s, NEG_BIG)
                    m_prev = m_scr[c, rows, :]
                    m_new = jnp.maximum(m_prev, jnp.max(s, axis=-1, keepdims=True))
                    alpha = jnp.exp2(m_prev - m_new)
                    p = jnp.exp2(s - jnp.tile(m_new, (1, kmax // LANES)))
                    psum = p[:, 0:LANES]
                    for t0 in range(LANES, kmax, LANES):
                        psum = psum + p[:, t0:t0 + LANES]
                    l_scr[c, rows, :] = alpha * l_scr[c, rows, :] + psum
                    m_scr[c, rows, :] = m_new
                    acc_scr[c, rows, :] = alpha * acc_scr[c, rows, :]
                    p_scr[c, rows, 0:kmax] = p.astype(jnp.bfloat16)
                    if kmax < tk:
                        p_scr[c, rows, kmax:tk] = jnp.zeros((rb, tk - kmax), jnp.bfloat16)
                acc_scr[c, all_rows, :] += _dot(p_scr[c, all_rows, :], v)

        n_sub = tq // tk

        def pair(i, carry):
            step(2 * i, 0, 0, None)
            step(2 * i + 1, 1, 0, None)
            return carry

        lax.fori_loop(0, (n_sub // 2) * qi, pair, 0)
        for t in range(n_sub):
            step(n_sub * qi + t, t % 2, t * tk, t * tk)

        lp = lam_ref[...]
        lam = (jnp.exp(jnp.sum(lp[0:1] * lp[1:2], axis=-1, keepdims=True))
               - jnp.exp(jnp.sum(lp[2:3] * lp[3:4], axis=-1, keepdims=True)) + lambda_init)
        l0 = jnp.sum(l_scr[0], axis=-1, keepdims=True)
        l1 = jnp.sum(l_scr[1], axis=-1, keepdims=True)
        o = acc_scr[0] / l0 - lam * (acc_scr[1] / l1)
        on = o * _rms_scale(o) * hg_ref[...]
        o_ref[q_rows, :] = (on * (1.0 - lambda_init)).astype(o_ref.dtype)
        return carry

    lax.fori_loop(0, q_ref.shape[0] // tq, q_tile, 0)


def _diff_attn(q, k, v, lam_params, head_gain, batch, seq, lambda_init):
    t = q.shape[0]
    assert ATT_TQ % (2 * ATT_TK) == 0
    assert seq % ATT_TQ == 0
    kern = functools.partial(_diff_attn_kernel, lambda_init=lambda_init)
    head_block = lambda width: pl.BlockSpec((seq, width), lambda b, h: (b, h))
    return pl.pallas_call(
        kern,
        grid=(batch, DIFF_HEADS),
        in_specs=[
            head_block(2 * DIFF_DH),
            head_block(2 * DIFF_DH),
            head_block(DIFF_VD),
            pl.BlockSpec(lam_params.shape, lambda b, h: (0, 0)),
            pl.BlockSpec((1, DIFF_VD), lambda b, h: (0, 0)),
        ],
        out_specs=head_block(DIFF_VD),
        out_shape=jax.ShapeDtypeStruct((t, DIFF_HEADS * DIFF_VD), jnp.bfloat16),
        scratch_shapes=[
            pltpu.VMEM((2, ATT_TQ, LANES), jnp.float32),
            pltpu.VMEM((2, ATT_TQ, LANES), jnp.float32),
            pltpu.VMEM((2, ATT_TQ, DIFF_VD), jnp.float32),
            pltpu.VMEM((2, 2, ATT_TQ, ATT_TK), jnp.float32),
            pltpu.VMEM((2, 2, ATT_TQ, ATT_TK), jnp.bfloat16),
        ],
        compiler_params=_params("parallel", "parallel"),
        name="diff_attn",
    )(q, k, v, lam_params, head_gain)


def kernel(x, a_norm, a_w_in, a_w_gate_up, a_b_gate, a_head_norm, a_w_out, kv_norm, w_kv, k_norm,
           b_norm, b_w_q, b_q_norm, b_lambda, b_head_norm, b_w_out, mlp_norm, mlp_w1, mlp_w2):
    batch, seq, d = x.shape
    bf = jnp.bfloat16
    row = lambda p: p.reshape(1, -1)
    h = x.reshape(batch * seq, d)

    n_main = 2 * GLA_HEADS * GLA_DK + 2 * GLA_HEADS * GLA_DV
    w_z = jnp.pad(a_w_in[0, :, n_main:], ((0, 0), (0, LANES - GLA_RANK))).astype(bf)
    w_gate = jnp.pad(a_w_gate_up[0], ((0, LANES - GLA_RANK), (0, 0))).astype(bf)
    qkvg, z = _gla_inproj(h, row(a_norm[0]), jnp.swapaxes(a_w_in, 1, 2), w_z, n_main)
    og = _gla_core(qkvg, z, w_gate, row(a_b_gate[0]), row(a_head_norm[0]), batch, seq)
    h = _outproj_mlp(og, h, a_w_out, row(mlp_norm[0]), mlp_w1, mlp_w2, 0, 0)

    layer = 1
    lambda_init = 0.8 - 0.6 * math.exp(-0.3 * layer)
    reps = SEG_TILE // DIFF_DH
    kn, v, qn = _diff_proj(h, row(kv_norm), row(b_norm[0]), w_kv, b_w_q,
                           row(jnp.tile(k_norm, reps)), row(jnp.tile(b_q_norm[0], reps)))
    oa = _diff_attn(qn, kn, v, b_lambda[0], row(b_head_norm[0]), batch, seq, lambda_init)
    h = _outproj_mlp(oa, h, b_w_out, row(mlp_norm[1]), mlp_w1, mlp_w2, 0, 1)
    return h.reshape(batch, seq, d)
```

```python
import functools
import math

import numpy as np
import jax
import jax.numpy as jnp
from jax import lax
from jax.experimental import pallas as pl
from jax.experimental.pallas import tpu as pltpu

D_MODEL = 1024
GLA_HEADS = 4
GLA_DK = 128
GLA_DV = 256
GLA_RANK = 16
GLA_TAU = 16.0
GLA_CHUNK = 64
DIFF_HEADS = 8
DIFF_DH = 64
DIFF_VD = 128
MLP_HIDDEN = 4 * D_MODEL
NORM_EPS = 1e-6

LANES = 128
VMEM_LIMIT = 56 * 1024 * 1024
NEG_BIG = -1e30

ROW_TILE = 1024
PROJ_ROW_TILE = 1024
PROJ_COL_CHUNK = 768
GLA_ROWS = 2048
SEG_TILE = 256
ATT_TQ = 1024
ATT_TK = 512
ATT_ROW_BLOCK = 32
GLA_LEVELS = (32, 16, 8, 4, 2, 1)
SUBLANES = 8
_GLA_N_VPU_LEVELS = sum(m >= SUBLANES for m in GLA_LEVELS)

_NT = (((1,), (1,)), ((), ()))
_TN = (((0,), (0,)), ((), ()))


def _dot(a, b, dims=None):
    if dims is None:
        return jnp.dot(a, b, preferred_element_type=jnp.float32)
    return lax.dot_general(a, b, dims, preferred_element_type=jnp.float32)


def _params(*sem):
    return pltpu.CompilerParams(dimension_semantics=sem, vmem_limit_bytes=VMEM_LIMIT)


def _resident(shape):
    nd = len(shape)
    return pl.BlockSpec(shape, lambda *_: (0,) * nd, pipeline_mode=pl.Buffered(1))


def _layer_resident(w, layer):
    return pl.BlockSpec((None,) + w.shape[1:], lambda *_: (layer, 0, 0),
                        pipeline_mode=pl.Buffered(1))


def _rounding_job(stacked_weights, n_steps):
    inputs, in_specs, out_specs, out_shapes = [], [], [], []
    for w, layer in stacked_weights:
        rows, cols = w.shape[1] // n_steps, w.shape[2]
        inputs.append(w)
        in_specs.append(pl.BlockSpec((None, rows, cols), lambda i, layer=layer: (layer, i, 0)))
        out_specs.append(pl.BlockSpec((rows, cols), lambda i: (i, 0)))
        out_shapes.append(jax.ShapeDtypeStruct(w.shape[1:], jnp.bfloat16))
    return inputs, in_specs, out_specs, out_shapes


def _round_weights(n_jobs, refs):
    for src, dst in zip(refs[:n_jobs], refs[n_jobs:]):
        dst[...] = src[...].astype(dst.dtype)


def _rms_scale(x):
    return lax.rsqrt(jnp.mean(x * x, axis=-1, keepdims=True) + NORM_EPS)


def _gla_inproj_kernel(x_ref, g_ref, w_ref, wz_ref, *rest, n_jobs):
    o_ref, z_ref = rest[n_jobs:n_jobs + 2]
    _round_weights(n_jobs, rest[:n_jobs] + rest[n_jobs + 2:])
    x = x_ref[...]
    xn = (x * _rms_scale(x) * g_ref[...]).astype(jnp.bfloat16)
    n_out = o_ref.shape[1]
    step = PROJ_COL_CHUNK
    for n0 in range(0, n_out, step):
        w = w_ref[n0:n0 + step, :].astype(jnp.bfloat16)
        o_ref[:, n0:n0 + step] = _dot(xn, w, _NT).astype(o_ref.dtype)
    z_ref[...] = _dot(xn, wz_ref[...])


def _gla_inproj(x, gain, w_in, w_z, n, next_weights):
    t, d = x.shape
    n_steps = t // PROJ_ROW_TILE
    jobs, job_in, job_out, job_shapes = _rounding_job(next_weights, n_steps)
    return pl.pallas_call(
        functools.partial(_gla_inproj_kernel, n_jobs=len(jobs)),
        grid=(n_steps,),
        in_specs=[
            pl.BlockSpec((PROJ_ROW_TILE, d), lambda i: (i, 0)),
            _resident((1, d)),
            _layer_resident(w_in, 0),
            _resident((d, LANES)),
        ] + job_in,
        out_specs=[
            pl.BlockSpec((PROJ_ROW_TILE, n), lambda i: (i, 0)),
            pl.BlockSpec((PROJ_ROW_TILE, LANES), lambda i: (i, 0)),
        ] + job_out,
        out_shape=[
            jax.ShapeDtypeStruct((t, n), jnp.bfloat16),
            jax.ShapeDtypeStruct((t, LANES), jnp.float32),
        ] + job_shapes,
        compiler_params=_params("parallel"),
        name="gla_inproj",
    )(x, gain, w_in, w_z, *jobs)


def _gla_constants():
    c = GLA_CHUNK
    t = np.arange(c)
    row, col = t[:, None], t[None, :]
    mats = [col <= row]
    masks = []
    for m in GLA_LEVELS:
        blk = 2 * m
        start = (t // blk) * blk
        mid = (start + m)[:, None]
        second = ((t % blk) >= m)[:, None]
        q_side = second & (col > mid) & (col <= row)
        k_side = (~second) & (col > row) & (col <= mid)
        if m < SUBLANES:
            mats.append(q_side | k_side)
        masks.append((start[:, None] == start[None, :]) & second & (~second).T)
    masks.append(np.eye(c, dtype=bool))
    lm = np.concatenate(mats, axis=0).astype(np.float32)
    mk = np.stack(masks, axis=0).astype(np.float32)
    return lm, mk


def _gla_core_kernel(q_ref, k_ref, v_ref, g_ref, z_ref, wg_ref, bg_ref, hg_ref,
                     lm_ref, mk_ref, o_ref, st_scr):
    c = GLA_CHUNK
    rows = q_ref.shape[0]

    @pl.when(pl.program_id(2) == 0)
    def _():
        st_scr[...] = jnp.zeros_like(st_scr)

    logits = _dot(z_ref[...].astype(jnp.bfloat16), wg_ref[...]) + bg_ref[...]
    log_sig = jnp.minimum(logits, 0.0) - jnp.log(1.0 + jnp.exp(-jnp.abs(logits)))
    la = log_sig * (math.log2(math.e) / GLA_TAU)
    la_hi = la.astype(jnp.bfloat16)
    la_lo = (la - la_hi.astype(jnp.float32)).astype(jnp.bfloat16)

    scale = GLA_DK ** -0.5
    gain = hg_ref[...]
    n = rows // c
    rs = [slice(ci * c, (ci + 1) * c) for ci in range(n)]
    n_lvl = len(GLA_LEVELS)

    sums = []
    for ci in range(n):
        la_stack = jnp.concatenate([la_hi[rs[ci]], la_lo[rs[ci]]], axis=0)
        sums.append(_dot(lm_ref[...], la_stack))
    cum = [s[0:c] for s in sums]

    def level_exponent(ci, li):
        m = GLA_LEVELS[li]
        if m < SUBLANES:
            lo = (1 + li - _GLA_N_VPU_LEVELS) * c
            return sums[ci][lo:lo + c]
        b = cum[ci]
        parts = []
        for start in range(0, c, 2 * m):
            mid = b[start + m:start + m + 1, :]
            parts.append(mid - b[start:start + m])
            parts.append(b[start + m:start + 2 * m] - mid)
        return jnp.concatenate(parts, axis=0)

    qf = [q_ref[r, :].astype(jnp.float32) for r in rs]
    kf = [k_ref[r, :].astype(jnp.float32) for r in rs]
    attn = [mk_ref[n_lvl] * _dot(q_ref[r, :], k_ref[r, :], _NT) for r in rs]
    for li in range(n_lvl):
        for ci in range(n):
            e = jnp.exp2(level_exponent(ci, li))
            ql = (qf[ci] * e).astype(jnp.bfloat16)
            kl = (kf[ci] * e).astype(jnp.bfloat16)
            attn[ci] = attn[ci] + mk_ref[li] * _dot(ql, kl, _NT)

    o_intra, qd, decay, upd = [], [], [], []
    for ci in range(n):
        b = cum[ci]
        b_last = b[c - 1:c, :]
        v = v_ref[rs[ci], :]
        kd = (kf[ci] * jnp.exp2(b_last - b)).astype(jnp.bfloat16)
        qd.append((qf[ci] * jnp.exp2(b)).astype(jnp.bfloat16))
        decay.append(jnp.exp2(b_last))
        upd.append(_dot(v, kd, _TN))
        o_intra.append(_dot(attn[ci].astype(jnp.bfloat16), v))

    st = st_scr[...]
    for ci in range(n):
        o = (o_intra[ci] + _dot(qd[ci], st.astype(jnp.bfloat16), _NT)) * scale
        st = st * decay[ci] + upd[ci]
        on = o * _rms_scale(o) * gain
        t = 0.5 * g_ref[rs[ci], :].astype(jnp.float32)
        o_ref[rs[ci], :] = (on * (t + t * jnp.tanh(t))).astype(o_ref.dtype)
    st_scr[...] = st


def _gla_core(qkvg, z, w_gate, b_gate, head_gain, batch, seq):
    t = qkvg.shape[0]
    nblk = seq // GLA_ROWS
    lm, mk = _gla_constants()
    lm = jnp.asarray(np.concatenate([lm, lm], axis=1), jnp.bfloat16)
    mk = jnp.asarray(mk, jnp.float32)
    kq = (GLA_HEADS * GLA_DK) // GLA_DK
    kv = (2 * GLA_HEADS * GLA_DK) // GLA_DV
    kg = kv + GLA_HEADS
    row = lambda b, h, s: b * nblk + s
    return pl.pallas_call(
        _gla_core_kernel,
        grid=(batch, GLA_HEADS, nblk),
        in_specs=[
            pl.BlockSpec((GLA_ROWS, GLA_DK), lambda b, h, s: (row(b, h, s), h)),
            pl.BlockSpec((GLA_ROWS, GLA_DK), lambda b, h, s: (row(b, h, s), kq + h)),
            pl.BlockSpec((GLA_ROWS, GLA_DV), lambda b, h, s: (row(b, h, s), kv + h)),
            pl.BlockSpec((GLA_ROWS, GLA_DV), lambda b, h, s: (row(b, h, s), kg + h)),
            pl.BlockSpec((GLA_ROWS, LANES), lambda b, h, s: (row(b, h, s), 0)),
            pl.BlockSpec((LANES, GLA_DK), lambda b, h, s: (0, h)),
            pl.BlockSpec((1, GLA_DK), lambda b, h, s: (0, h)),
            pl.BlockSpec((1, GLA_DV), lambda b, h, s: (0, 0)),
            pl.BlockSpec(lm.shape, lambda b, h, s: (0, 0)),
            pl.BlockSpec(mk.shape, lambda b, h, s: (0, 0, 0)),
        ],
        out_specs=pl.BlockSpec((GLA_ROWS, GLA_DV), lambda b, h, s: (row(b, h, s), h)),
        out_shape=jax.ShapeDtypeStruct((t, GLA_HEADS * GLA_DV), jnp.bfloat16),
        scratch_shapes=[pltpu.VMEM((GLA_DV, GLA_DK), jnp.float32)],
        compiler_params=_params("parallel", "parallel", "arbitrary"),
        name="gla_core",
    )(qkvg, qkvg, qkvg, qkvg, z, w_gate, b_gate, head_gain, lm, mk)


def _outproj_mlp_kernel(a_ref, h_ref, wo_ref, g_ref, w1_ref, w2_ref, o_ref):
    bf = jnp.bfloat16
    h1 = h_ref[...] + _dot(a_ref[...], wo_ref[...])
    xn = (h1 * _rms_scale(h1) * g_ref[...]).astype(bf)
    acc = h1
    d = h1.shape[1]
    for c0 in range(0, w1_ref.shape[1], d):
        hid = jnp.maximum(_dot(xn, w1_ref[:, c0:c0 + d]), 0.0)
        acc = acc + _dot((hid * hid).astype(bf), w2_ref[c0:c0 + d, :])
    o_ref[...] = acc


def _outproj_mlp(a, h, w_out, gain, w1, w2):
    t, d = h.shape
    tile = pl.BlockSpec((ROW_TILE, d), lambda i: (i, 0))
    return pl.pallas_call(
        _outproj_mlp_kernel,
        grid=(t // ROW_TILE,),
        in_specs=[tile, tile, _resident(w_out.shape), _resident((1, d)),
                  _resident(w1.shape), _resident(w2.shape)],
        out_specs=tile,
        out_shape=jax.ShapeDtypeStruct((t, d), jnp.float32),
        compiler_params=_params("parallel"),
        name="outproj_mlp",
    )(a, h, w_out, gain, w1, w2)


def _segment_norm(y, seg_ones, gain):
    ss = _dot((y * y).astype(jnp.bfloat16), seg_ones)
    return y * lax.rsqrt(ss * (1.0 / DIFF_DH) + NORM_EPS) * gain


def _diff_proj_kernel(h_ref, gkv_ref, gq_ref, wkv_ref, wq_ref, kg_ref, qg_ref, so_ref,
                      *rest, n_jobs):
    k_ref, v_ref, q_ref = rest[n_jobs:n_jobs + 3]
    _round_weights(n_jobs, rest[:n_jobs] + rest[n_jobs + 3:])
    nk = k_ref.shape[1]
    seg_ones = so_ref[...]
    bf = jnp.bfloat16
    qscale = DIFF_DH ** -0.5 * math.log2(math.e)
    x = h_ref[...]
    xhat = x * _rms_scale(x)
    xkv = (xhat * gkv_ref[...]).astype(bf)
    xq = (xhat * gq_ref[...]).astype(bf)
    def norm_store(out_ref, gain_ref, scale, cols, raw):
        y = _segment_norm(raw, seg_ones, gain_ref[...])
        out_ref[:, cols] = (y if scale is None else y * scale).astype(out_ref.dtype)

    pending = None
    for x_in, w_ref, out_ref, gain_ref, scale in ((xkv, wkv_ref, k_ref, kg_ref, None),
                                                 (xq, wq_ref, q_ref, qg_ref, qscale)):
        for c0 in range(0, nk, SEG_TILE):
            cols = slice(c0, c0 + SEG_TILE)
            raw = _dot(x_in, w_ref[:, cols].astype(bf))
            if pending is not None:
                norm_store(*pending)
            pending = (out_ref, gain_ref, scale, cols, raw)
        if out_ref is k_ref:
            v_ref[...] = _dot(xkv, wkv_ref[:, nk:].astype(bf)).astype(v_ref.dtype)
    norm_store(*pending)


def _diff_proj(h, g_kv, g_q, w_kv, w_q, k_gain, q_gain, next_weights):
    t, d = h.shape
    n_steps = t // PROJ_ROW_TILE
    jobs, job_in, job_out, job_shapes = _rounding_job(next_weights, n_steps)
    nk = DIFF_HEADS * 2 * DIFF_DH
    nv = DIFF_HEADS * DIFF_VD
    lane = np.arange(SEG_TILE)
    seg_ones = jnp.asarray(lane[:, None] // DIFF_DH == lane[None, :] // DIFF_DH, jnp.bfloat16)
    tile = lambda n: pl.BlockSpec((PROJ_ROW_TILE, n), lambda i: (i, 0))
    return pl.pallas_call(
        functools.partial(_diff_proj_kernel, n_jobs=len(jobs)),
        grid=(n_steps,),
        in_specs=[tile(d), _resident((1, d)), _resident((1, d)), _resident(w_kv.shape),
                  _layer_resident(w_q, 0), _resident((1, SEG_TILE)), _resident((1, SEG_TILE)),
                  _resident((SEG_TILE, SEG_TILE))] + job_in,
        out_specs=[tile(nk), tile(nv), tile(nk)] + job_out,
        out_shape=[jax.ShapeDtypeStruct((t, nk), jnp.bfloat16),
                   jax.ShapeDtypeStruct((t, nv), jnp.bfloat16),
                   jax.ShapeDtypeStruct((t, nk), jnp.bfloat16)] + job_shapes,
        compiler_params=_params("parallel"),
        name="diff_proj",
    )(h, g_kv, g_q, w_kv, w_q, k_gain, q_gain, seg_ones, *jobs)


def _diff_attn_kernel(q_ref, k_ref, v_ref, lam_ref, hg_ref, o_ref,
                      m_scr, l_scr, acc_scr, s_bufs, p_bufs, *, lambda_init):
    tq = ATT_TQ
    tk = ATT_TK
    rb = ATT_ROW_BLOCK
    lane = lax.broadcasted_iota(jnp.int32, (tq, q_ref.shape[1]), 1)
    col_minus_row = (lax.broadcasted_iota(jnp.int32, (rb, tk), 1)
                     - lax.broadcasted_iota(jnp.int32, (rb, tk), 0))

    def block(j):
        return pl.ds(pl.multiple_of(j * tk, tk), tk)

    def q_tile(qi, carry):
        q_rows = pl.ds(pl.multiple_of(qi * tq, tq), tq)
        m_scr[...] = jnp.full_like(m_scr, NEG_BIG)
        l_scr[...] = jnp.zeros_like(l_scr)
        acc_scr[...] = jnp.zeros_like(acc_scr)

        q = q_ref[q_rows, :]
        qcs = [jnp.where((lane // DIFF_DH) == c, q, jnp.zeros_like(q)) for c in range(2)]

        def step(j, buf, row_lo, diag_row0):
            k = k_ref[block(j), :]
            v = v_ref[block(j), :]
            s_scr = s_bufs.at[buf]
            p_scr = p_bufs.at[buf]
            all_rows = slice(row_lo, tq)
            for c in range(2):
                s_scr[c, all_rows, :] = _dot(qcs[c][all_rows], k, _NT)
            for c in range(2):
                for r0 in range(row_lo, tq, rb):
                    rows = slice(r0, r0 + rb)
                    masked = diag_row0 is not None and r0 - diag_row0 < tk
                    kmax = min(tk, -(-(r0 - diag_row0 + rb) // LANES) * LANES) if masked else tk
                    s = s_scr[c, rows, 0:kmax]
                    if masked:
                        s = jnp.where(col_minus_row[:, 0:kmax] <= r0 - diag_row0, s, NEG_BIG)
                    m_prev = m_scr[c, rows, :]
                    m_new = jnp.maximum(m_prev, jnp.max(s, axis=-1, keepdims=True))
                    alpha = jnp.exp2(m_prev - m_new)
                    p = jnp.exp2(s - jnp.tile(m_new, (1, kmax // LANES)))
                    psum = p[:, 0:LANES]
                    for t0 in range(LANES, kmax, LANES):
                        psum = psum + p[:, t0:t0 + LANES]
                    l_scr[c, rows, :] = alpha * l_scr[c, rows, :] + psum
                    m_scr[c, rows, :] = m_new
                    acc_scr[c, rows, :] = alpha * acc_scr[c, rows, :]
                    p_scr[c, rows, 0:kmax] = p.astype(jnp.bfloat16)
                    if kmax < tk:
                        p_scr[c, rows, kmax:tk] = jnp.zeros((rb, tk - kmax), jnp.bfloat16)
                acc_scr[c, all_rows, :] += _dot(p_scr[c, all_rows, :], v)

        n_sub = tq // tk

        def pair(i, carry):
            step(2 * i, 0, 0, None)
            step(2 * i + 1, 1, 0, None)
            return carry

        lax.fori_loop(0, (n_sub // 2) * qi, pair, 0)
        for t in range(n_sub):
            step(n_sub * qi + t, t % 2, t * tk, t * tk)

        lp = lam_ref[...]
        lam = (jnp.exp(jnp.sum(lp[0:1] * lp[1:2], axis=-1, keepdims=True))
               - jnp.exp(jnp.sum(lp[2:3] * lp[3:4], axis=-1, keepdims=True)) + lambda_init)
        l0 = jnp.sum(l_scr[0], axis=-1, keepdims=True)
        l1 = jnp.sum(l_scr[1], axis=-1, keepdims=True)
        o = acc_scr[0] / l0 - lam * (acc_scr[1] / l1)
        on = o * _rms_scale(o) * hg_ref[...]
        o_ref[q_rows, :] = (on * (1.0 - lambda_init)).astype(o_ref.dtype)
        return carry

    lax.fori_loop(0, q_ref.shape[0] // tq, q_tile, 0)


def _diff_attn(q, k, v, lam_params, head_gain, batch, seq, lambda_init):
    t = q.shape[0]
    assert ATT_TQ % (2 * ATT_TK) == 0
    assert seq % ATT_TQ == 0
    kern = functools.partial(_diff_attn_kernel, lambda_init=lambda_init)
    head_block = lambda width: pl.BlockSpec((seq, width), lambda b, h: (b, h))
    return pl.pallas_call(
        kern,
        grid=(batch, DIFF_HEADS),
        in_specs=[
            head_block(2 * DIFF_DH),
            head_block(2 * DIFF_DH),
            head_block(DIFF_VD),
            pl.BlockSpec(lam_params.shape, lambda b, h: (0, 0)),
            pl.BlockSpec((1, DIFF_VD), lambda b, h: (0, 0)),
        ],
        out_specs=head_block(DIFF_VD),
        out_shape=jax.ShapeDtypeStruct((t, DIFF_HEADS * DIFF_VD), jnp.bfloat16),
        scratch_shapes=[
            pltpu.VMEM((2, ATT_TQ, LANES), jnp.float32),
            pltpu.VMEM((2, ATT_TQ, LANES), jnp.float32),
            pltpu.VMEM((2, ATT_TQ, DIFF_VD), jnp.float32),
            pltpu.VMEM((2, 2, ATT_TQ, ATT_TK), jnp.float32),
            pltpu.VMEM((2, 2, ATT_TQ, ATT_TK), jnp.bfloat16),
        ],
        compiler_params=_params("parallel", "parallel"),
        name="diff_attn",
    )(q, k, v, lam_params, head_gain)


def kernel(x, a_norm, a_w_in, a_w_gate_up, a_b_gate, a_head_norm, a_w_out, kv_norm, w_kv, k_norm,
           b_norm, b_w_q, b_q_norm, b_lambda, b_head_norm, b_w_out, mlp_norm, mlp_w1, mlp_w2):
    batch, seq, d = x.shape
    bf = jnp.bfloat16
    row = lambda p: p.reshape(1, -1)
    h = x.reshape(batch * seq, d)

    n_main = 2 * GLA_HEADS * GLA_DK + 2 * GLA_HEADS * GLA_DV
    w_z = jnp.pad(a_w_in[0, :, n_main:], ((0, 0), (0, LANES - GLA_RANK))).astype(bf)
    w_gate = jnp.pad(a_w_gate_up[0], ((0, LANES - GLA_RANK), (0, 0))).astype(bf)
    qkvg, z, wo_bf, w1_bf, w2_bf = _gla_inproj(
        h, row(a_norm[0]), jnp.swapaxes(a_w_in, 1, 2), w_z, n_main,
        next_weights=((a_w_out, 0), (mlp_w1, 0), (mlp_w2, 0)))
    og = _gla_core(qkvg, z, w_gate, row(a_b_gate[0]), row(a_head_norm[0]), batch, seq)
    h = _outproj_mlp(og, h, wo_bf, row(mlp_norm[0]), w1_bf, w2_bf)

    layer = 1
    lambda_init = 0.8 - 0.6 * math.exp(-0.3 * layer)
    reps = SEG_TILE // DIFF_DH
    kn, v, qn, wo_bf, w1_bf, w2_bf = _diff_proj(
        h, row(kv_norm), row(b_norm[0]), w_kv, b_w_q,
        row(jnp.tile(k_norm, reps)), row(jnp.tile(b_q_norm[0], reps)),
        next_weights=((b_w_out, 0), (mlp_w1, 1), (mlp_w2, 1)))
    oa = _diff_attn(qn, kn, v, b_lambda[0], row(b_head_norm[0]), batch, seq, lambda_init)
    h = _outproj_mlp(oa, h, wo_bf, row(mlp_norm[1]), w1_bf, w2_bf)
    return h.reshape(batch, seq, d)
```

```python
import functools
import math

import numpy as np
import jax
import jax.numpy as jnp
from jax import lax
from jax.experimental import pallas as pl
from jax.experimental.pallas import tpu as pltpu

D_MODEL = 1024
GLA_HEADS = 4
GLA_DK = 128
GLA_DV = 256
GLA_RANK = 16
GLA_TAU = 16.0
GLA_CHUNK = 64
DIFF_HEADS = 8
DIFF_DH = 64
DIFF_VD = 128
MLP_HIDDEN = 4 * D_MODEL
NORM_EPS = 1e-6

LANES = 128
VMEM_LIMIT = 56 * 1024 * 1024
NEG_BIG = -1e30

ROW_TILE = 1024
PROJ_ROW_TILE = 1024
PROJ_COL_CHUNK = 768
GLA_ROWS = 2048
SEG_TILE = 256
ATT_TQ = 1024
ATT_TK = 512
ATT_ROW_BLOCK = 32
GLA_LEVELS = (32, 16, 8, 4, 2, 1)
SUBLANES = 8
_GLA_N_VPU_LEVELS = sum(m >= SUBLANES for m in GLA_LEVELS)

_NT = (((1,), (1,)), ((), ()))
_TN = (((0,), (0,)), ((), ()))


def _dot(a, b, dims=None):
    if dims is None:
        return jnp.dot(a, b, preferred_element_type=jnp.float32)
    return lax.dot_general(a, b, dims, preferred_element_type=jnp.float32)


def _params(*sem):
    return pltpu.CompilerParams(dimension_semantics=sem, vmem_limit_bytes=VMEM_LIMIT)


def _resident(shape):
    nd = len(shape)
    return pl.BlockSpec(shape, lambda *_: (0,) * nd, pipeline_mode=pl.Buffered(1))


def _layer_resident(w, layer):
    return pl.BlockSpec((None,) + w.shape[1:], lambda *_: (layer, 0, 0),
                        pipeline_mode=pl.Buffered(1))


def _rounding_job(stacked_weights, n_steps):
    inputs, in_specs, out_specs, out_shapes = [], [], [], []
    for w, layer in stacked_weights:
        rows, cols = w.shape[1] // n_steps, w.shape[2]
        inputs.append(w)
        in_specs.append(pl.BlockSpec((None, rows, cols), lambda i, layer=layer: (layer, i, 0)))
        out_specs.append(pl.BlockSpec((rows, cols), lambda i: (i, 0)))
        out_shapes.append(jax.ShapeDtypeStruct(w.shape[1:], jnp.bfloat16))
    return inputs, in_specs, out_specs, out_shapes


def _round_weights(n_jobs, refs):
    for src, dst in zip(refs[:n_jobs], refs[n_jobs:]):
        dst[...] = src[...].astype(dst.dtype)


def _rms_scale(x):
    return lax.rsqrt(jnp.mean(x * x, axis=-1, keepdims=True) + NORM_EPS)


def _gla_inproj_kernel(x_ref, g_ref, w_ref, wz_ref, *rest, n_jobs):
    o_ref, z_ref = rest[n_jobs:n_jobs + 2]
    _round_weights(n_jobs, rest[:n_jobs] + rest[n_jobs + 2:])
    x = x_ref[...]
    xn = (x * _rms_scale(x) * g_ref[...]).astype(jnp.bfloat16)
    n_out = o_ref.shape[1]
    step = PROJ_COL_CHUNK
    for n0 in range(0, n_out, step):
        w = w_ref[n0:n0 + step, :].astype(jnp.bfloat16)
        o_ref[:, n0:n0 + step] = _dot(xn, w, _NT).astype(o_ref.dtype)
    z_ref[...] = _dot(xn, wz_ref[...])


def _gla_inproj(x, gain, w_in, w_z, n, next_weights):
    t, d = x.shape
    n_steps = t // PROJ_ROW_TILE
    jobs, job_in, job_out, job_shapes = _rounding_job(next_weights, n_steps)
    return pl.pallas_call(
        functools.partial(_gla_inproj_kernel, n_jobs=len(jobs)),
        grid=(n_steps,),
        in_specs=[
            pl.BlockSpec((PROJ_ROW_TILE, d), lambda i: (i, 0)),
            _resident((1, d)),
            _layer_resident(w_in, 0),
            _resident((d, LANES)),
        ] + job_in,
        out_specs=[
            pl.BlockSpec((PROJ_ROW_TILE, n), lambda i: (i, 0)),
            pl.BlockSpec((PROJ_ROW_TILE, LANES), lambda i: (i, 0)),
        ] + job_out,
        out_shape=[
            jax.ShapeDtypeStruct((t, n), jnp.bfloat16),
            jax.ShapeDtypeStruct((t, LANES), jnp.float32),
        ] + job_shapes,
        compiler_params=_params("parallel"),
        name="gla_inproj",
    )(x, gain, w_in, w_z, *jobs)


def _gla_constants():
    c = GLA_CHUNK
    t = np.arange(c)
    row, col = t[:, None], t[None, :]
    mats = [col <= row]
    masks = []
    for m in GLA_LEVELS:
        blk = 2 * m
        start = (t // blk) * blk
        mid = (start + m)[:, None]
        second = ((t % blk) >= m)[:, None]
        q_side = second & (col > mid) & (col <= row)
        k_side = (~second) & (col > row) & (col <= mid)
        if m < SUBLANES:
            mats.append(q_side | k_side)
        masks.append((start[:, None] == start[None, :]) & second & (~second).T)
    masks.append(np.eye(c, dtype=bool))
    lm = np.concatenate(mats, axis=0).astype(np.float32)
    mk = np.stack(masks, axis=0).astype(np.float32)
    return lm, mk


def _gla_core_kernel(q_ref, k_ref, v_ref, g_ref, z_ref, wg_ref, bg_ref, hg_ref,
                     lm_ref, mk_ref, o_ref, st_scr):
    c = GLA_CHUNK
    rows = q_ref.shape[0]

    @pl.when(pl.program_id(2) == 0)
    def _():
        st_scr[...] = jnp.zeros_like(st_scr)

    logits = _dot(z_ref[...].astype(jnp.bfloat16), wg_ref[...]) + bg_ref[...]
    log_sig = jnp.minimum(logits, 0.0) - jnp.log(1.0 + jnp.exp(-jnp.abs(logits)))
    la = log_sig * (math.log2(math.e) / GLA_TAU)
    la_hi = la.astype(jnp.bfloat16)
    la_lo = (la - la_hi.astype(jnp.float32)).astype(jnp.bfloat16)

    scale = GLA_DK ** -0.5
    gain = hg_ref[...]
    n = rows // c
    rs = [slice(ci * c, (ci + 1) * c) for ci in range(n)]
    n_lvl = len(GLA_LEVELS)

    sums = []
    for ci in range(n):
        la_stack = jnp.concatenate([la_hi[rs[ci]], la_lo[rs[ci]]], axis=0)
        sums.append(_dot(lm_ref[...], la_stack))
    cum = [s[0:c] for s in sums]

    def level_exponent(ci, li):
        m = GLA_LEVELS[li]
        if m < SUBLANES:
            lo = (1 + li - _GLA_N_VPU_LEVELS) * c
            return sums[ci][lo:lo + c]
        b = cum[ci]
        parts = []
        for start in range(0, c, 2 * m):
            mid = b[start + m:start + m + 1, :]
            parts.append(mid - b[start:start + m])
            parts.append(b[start + m:start + 2 * m] - mid)
        return jnp.concatenate(parts, axis=0)

    qf = [q_ref[r, :].astype(jnp.float32) for r in rs]
    kf = [k_ref[r, :].astype(jnp.float32) for r in rs]
    attn = [mk_ref[n_lvl] * _dot(q_ref[r, :], k_ref[r, :], _NT) for r in rs]
    row_in_chunk = lax.broadcasted_iota(jnp.int32, (c, GLA_DK), 0)

    def level_operand(ci, li):
        m = GLA_LEVELS[li]
        if m < SUBLANES:
            return jnp.where(row_in_chunk % (2 * m) >= m, qf[ci], kf[ci])
        parts = []
        for start in range(0, c, 2 * m):
            parts.append(kf[ci][start:start + m])
            parts.append(qf[ci][start + m:start + 2 * m])
        return jnp.concatenate(parts, axis=0)

    for li in range(n_lvl):
        for ci in range(n):
            e = jnp.exp2(level_exponent(ci, li))
            x = (level_operand(ci, li) * e).astype(jnp.bfloat16)
            attn[ci] = attn[ci] + mk_ref[li] * _dot(x, x, _NT)

    o_intra, qd, decay, upd = [], [], [], []
    for ci in range(n):
        b = cum[ci]
        b_last = b[c - 1:c, :]
        v = v_ref[rs[ci], :]
        kd = (kf[ci] * jnp.exp2(b_last - b)).astype(jnp.bfloat16)
        qd.append((qf[ci] * jnp.exp2(b)).astype(jnp.bfloat16))
        decay.append(jnp.exp2(b_last))
        upd.append(_dot(v, kd, _TN))
        o_intra.append(_dot(attn[ci].astype(jnp.bfloat16), v))

    st = st_scr[...]
    for ci in range(n):
        o = (o_intra[ci] + _dot(qd[ci], st.astype(jnp.bfloat16), _NT)) * scale
        st = st * decay[ci] + upd[ci]
        on = o * _rms_scale(o) * gain
        t = 0.5 * g_ref[rs[ci], :].astype(jnp.float32)
        o_ref[rs[ci], :] = (on * (t + t * jnp.tanh(t))).astype(o_ref.dtype)
    st_scr[...] = st


def _gla_core(qkvg, z, w_gate, b_gate, head_gain, batch, seq):
    t = qkvg.shape[0]
    nblk = seq // GLA_ROWS
    lm, mk = _gla_constants()
    lm = jnp.asarray(np.concatenate([lm, lm], axis=1), jnp.bfloat16)
    mk = jnp.asarray(mk, jnp.float32)
    kq = (GLA_HEADS * GLA_DK) // GLA_DK
    kv = (2 * GLA_HEADS * GLA_DK) // GLA_DV
    kg = kv + GLA_HEADS
    row = lambda b, h, s: b * nblk + s
    return pl.pallas_call(
        _gla_core_kernel,
        grid=(batch, GLA_HEADS, nblk),
        in_specs=[
            pl.BlockSpec((GLA_ROWS, GLA_DK), lambda b, h, s: (row(b, h, s), h)),
            pl.BlockSpec((GLA_ROWS, GLA_DK), lambda b, h, s: (row(b, h, s), kq + h)),
            pl.BlockSpec((GLA_ROWS, GLA_DV), lambda b, h, s: (row(b, h, s), kv + h)),
            pl.BlockSpec((GLA_ROWS, GLA_DV), lambda b, h, s: (row(b, h, s), kg + h)),
            pl.BlockSpec((GLA_ROWS, LANES), lambda b, h, s: (row(b, h, s), 0)),
            pl.BlockSpec((LANES, GLA_DK), lambda b, h, s: (0, h)),
            pl.BlockSpec((1, GLA_DK), lambda b, h, s: (0, h)),
            pl.BlockSpec((1, GLA_DV), lambda b, h, s: (0, 0)),
            pl.BlockSpec(lm.shape, lambda b, h, s: (0, 0)),
            pl.BlockSpec(mk.shape, lambda b, h, s: (0, 0, 0)),
        ],
        out_specs=pl.BlockSpec((GLA_ROWS, GLA_DV), lambda b, h, s: (row(b, h, s), h)),
        out_shape=jax.ShapeDtypeStruct((t, GLA_HEADS * GLA_DV), jnp.bfloat16),
        scratch_shapes=[pltpu.VMEM((GLA_DV, GLA_DK), jnp.float32)],
        compiler_params=_params("parallel", "parallel", "arbitrary"),
        name="gla_core",
    )(qkvg, qkvg, qkvg, qkvg, z, w_gate, b_gate, head_gain, lm, mk)


def _outproj_mlp_kernel(a_ref, h_ref, wo_ref, g_ref, w1_ref, w2_ref, o_ref):
    bf = jnp.bfloat16
    h1 = h_ref[...] + _dot(a_ref[...], wo_ref[...])
    xn = (h1 * _rms_scale(h1) * g_ref[...]).astype(bf)
    acc = h1
    d = h1.shape[1]
    for c0 in range(0, w1_ref.shape[1], d):
        hid = jnp.maximum(_dot(xn, w1_ref[:, c0:c0 + d]), 0.0)
        acc = acc + _dot((hid * hid).astype(bf), w2_ref[c0:c0 + d, :])
    o_ref[...] = acc


def _outproj_mlp(a, h, w_out, gain, w1, w2):
    t, d = h.shape
    tile = pl.BlockSpec((ROW_TILE, d), lambda i: (i, 0))
    return pl.pallas_call(
        _outproj_mlp_kernel,
        grid=(t // ROW_TILE,),
        in_specs=[tile, tile, _resident(w_out.shape), _resident((1, d)),
                  _resident(w1.shape), _resident(w2.shape)],
        out_specs=tile,
        out_shape=jax.ShapeDtypeStruct((t, d), jnp.float32),
        compiler_params=_params("parallel"),
        name="outproj_mlp",
    )(a, h, w_out, gain, w1, w2)


def _segment_norm(y, seg_ones, gain):
    ss = _dot((y * y).astype(jnp.bfloat16), seg_ones)
    return y * lax.rsqrt(ss * (1.0 / DIFF_DH) + NORM_EPS) * gain


def _diff_proj_kernel(h_ref, gkv_ref, gq_ref, wkv_ref, wq_ref, kg_ref, qg_ref, so_ref,
                      *rest, n_jobs):
    k_ref, v_ref, q_ref = rest[n_jobs:n_jobs + 3]
    _round_weights(n_jobs, rest[:n_jobs] + rest[n_jobs + 3:])
    nk = k_ref.shape[1]
    seg_ones = so_ref[...]
    bf = jnp.bfloat16
    qscale = DIFF_DH ** -0.5 * math.log2(math.e)
    x = h_ref[...]
    xhat = x * _rms_scale(x)
    xkv = (xhat * gkv_ref[...]).astype(bf)
    xq = (xhat * gq_ref[...]).astype(bf)
    def norm_store(out_ref, gain_ref, scale, cols, raw):
        y = _segment_norm(raw, seg_ones, gain_ref[...])
        out_ref[:, cols] = (y if scale is None else y * scale).astype(out_ref.dtype)

    pending = None
    for x_in, w_ref, out_ref, gain_ref, scale in ((xkv, wkv_ref, k_ref, kg_ref, None),
                                                 (xq, wq_ref, q_ref, qg_ref, qscale)):
        for c0 in range(0, nk, SEG_TILE):
            cols = slice(c0, c0 + SEG_TILE)
            raw = _dot(x_in, w_ref[:, cols].astype(bf))
            if pending is not None:
                norm_store(*pending)
            pending = (out_ref, gain_ref, scale, cols, raw)
        if out_ref is k_ref:
            v_ref[...] = _dot(xkv, wkv_ref[:, nk:].astype(bf)).astype(v_ref.dtype)
    norm_store(*pending)


def _diff_proj(h, g_kv, g_q, w_kv, w_q, k_gain, q_gain, next_weights):
    t, d = h.shape
    n_steps = t // PROJ_ROW_TILE
    jobs, job_in, job_out, job_shapes = _rounding_job(next_weights, n_steps)
    nk = DIFF_HEADS * 2 * DIFF_DH
    nv = DIFF_HEADS * DIFF_VD
    lane = np.arange(SEG_TILE)
    seg_ones = jnp.asarray(lane[:, None] // DIFF_DH == lane[None, :] // DIFF_DH, jnp.bfloat16)
    tile = lambda n: pl.BlockSpec((PROJ_ROW_TILE, n), lambda i: (i, 0))
    return pl.pallas_call(
        functools.partial(_diff_proj_kernel, n_jobs=len(jobs)),
        grid=(n_steps,),
        in_specs=[tile(d), _resident((1, d)), _resident((1, d)), _resident(w_kv.shape),
                  _layer_resident(w_q, 0), _resident((1, SEG_TILE)), _resident((1, SEG_TILE)),
                  _resident((SEG_TILE, SEG_TILE))] + job_in,
        out_specs=[tile(nk), tile(nv), tile(nk)] + job_out,
        out_shape=[jax.ShapeDtypeStruct((t, nk), jnp.bfloat16),
                   jax.ShapeDtypeStruct((t, nv), jnp.bfloat16),
                   jax.ShapeDtypeStruct((t, nk), jnp.bfloat16)] + job_shapes,
        compiler_params=_params("parallel"),
        name="diff_proj",
    )(h, g_kv, g_q, w_kv, w_q, k_gain, q_gain, seg_ones, *jobs)


def _diff_attn_kernel(q_ref, k_ref, v_ref, lam_ref, hg_ref, o_ref,
                      m_scr, l_scr, acc_scr, s_bufs, p_bufs, *, lambda_init):
    tq = ATT_TQ
    tk = ATT_TK
    rb = ATT_ROW_BLOCK
    lane = lax.broadcasted_iota(jnp.int32, (tq, q_ref.shape[1]), 1)
    col_minus_row = (lax.broadcasted_iota(jnp.int32, (rb, tk), 1)
                     - lax.broadcasted_iota(jnp.int32, (rb, tk), 0))

    def block(j):
        return pl.ds(pl.multiple_of(j * tk, tk), tk)

    def q_tile(qi, carry):
        q_rows = pl.ds(pl.multiple_of(qi * tq, tq), tq)
        m_scr[...] = jnp.full_like(m_scr, NEG_BIG)
        l_scr[...] = jnp.zeros_like(l_scr)
        acc_scr[...] = jnp.zeros_like(acc_scr)

        q = q_ref[q_rows, :]
        qcs = [jnp.where((lane // DIFF_DH) == c, q, jnp.zeros_like(q)) for c in range(2)]

        def step(j, buf, row_lo, diag_row0):
            k = k_ref[block(j), :]
            v = v_ref[block(j), :]
            s_scr = s_bufs.at[buf]
            p_scr = p_bufs.at[buf]
            all_rows = slice(row_lo, tq)
            for c in range(2):
                s_scr[c, all_rows, :] = _dot(qcs[c][all_rows], k, _NT)
            for c in range(2):
                for r0 in range(row_lo, tq, rb):
                    rows = slice(r0, r0 + rb)
                    masked = diag_row0 is not None and r0 - diag_row0 < tk
                    kmax = min(tk, -(-(r0 - diag_row0 + rb) // LANES) * LANES) if masked else tk
                    s = s_scr[c, rows, 0:kmax]
                    if masked:
                        s = jnp.where(col_minus_row[:, 0:kmax] <= r0 - diag_row0, s, NEG_BIG)
                    m_prev = m_scr[c, rows, :]
                    m_new = jnp.maximum(m_prev, jnp.max(s, axis=-1, keepdims=True))
                    alpha = jnp.exp2(m_prev - m_new)
                    p = jnp.exp2(s - jnp.tile(m_new, (1, kmax // LANES)))
                    psum = p[:, 0:LANES]
                    for t0 in range(LANES, kmax, LANES):
                        psum = psum + p[:, t0:t0 + LANES]
                    l_scr[c, rows, :] = alpha * l_scr[c, rows, :] + psum
                    m_scr[c, rows, :] = m_new
                    acc_scr[c, rows, :] = alpha * acc_scr[c, rows, :]
                    p_scr[c, rows, 0:kmax] = p.astype(jnp.bfloat16)
                    if kmax < tk:
                        p_scr[c, rows, kmax:tk] = jnp.zeros((rb, tk - kmax), jnp.bfloat16)
                acc_scr[c, all_rows, :] += _dot(p_scr[c, all_rows, :], v)

        n_sub = tq // tk

        def pair(i, carry):
            step(2 * i, 0, 0, None)
            step(2 * i + 1, 1, 0, None)
            return carry

        lax.fori_loop(0, (n_sub // 2) * qi, pair, 0)
        for t in range(n_sub):
            step(n_sub * qi + t, t % 2, t * tk, t * tk)

        lp = lam_ref[...]
        lam = (jnp.exp(jnp.sum(lp[0:1] * lp[1:2], axis=-1, keepdims=True))
               - jnp.exp(jnp.sum(lp[2:3] * lp[3:4], axis=-1, keepdims=True)) + lambda_init)
        l0 = jnp.sum(l_scr[0], axis=-1, keepdims=True)
        l1 = jnp.sum(l_scr[1], axis=-1, keepdims=True)
        o = acc_scr[0] / l0 - lam * (acc_scr[1] / l1)
        on = o * _rms_scale(o) * hg_ref[...]
        o_ref[q_rows, :] = (on * (1.0 - lambda_init)).astype(o_ref.dtype)
        return carry

    lax.fori_loop(0, q_ref.shape[0] // tq, q_tile, 0)


def _diff_attn(q, k, v, lam_params, head_gain, batch, seq, lambda_init):
    t = q.shape[0]
    assert ATT_TQ % (2 * ATT_TK) == 0
    assert seq % ATT_TQ == 0
    kern = functools.partial(_diff_attn_kernel, lambda_init=lambda_init)
    head_block = lambda width: pl.BlockSpec((seq, width), lambda b, h: (b, h))
    return pl.pallas_call(
        kern,
        grid=(batch, DIFF_HEADS),
        in_specs=[
            head_block(2 * DIFF_DH),
            head_block(2 * DIFF_DH),
            head_block(DIFF_VD),
            pl.BlockSpec(lam_params.shape, lambda b, h: (0, 0)),
            pl.BlockSpec((1, DIFF_VD), lambda b, h: (0, 0)),
        ],
        out_specs=head_block(DIFF_VD),
        out_shape=jax.ShapeDtypeStruct((t, DIFF_HEADS * DIFF_VD), jnp.bfloat16),
        scratch_shapes=[
            pltpu.VMEM((2, ATT_TQ, LANES), jnp.float32),
            pltpu.VMEM((2, ATT_TQ, LANES), jnp.float32),
            pltpu.VMEM((2, ATT_TQ, DIFF_VD), jnp.float32),
            pltpu.VMEM((2, 2, ATT_TQ, ATT_TK), jnp.float32),
            pltpu.VMEM((2, 2, ATT_TQ, ATT_TK), jnp.bfloat16),
        ],
        compiler_params=_params("parallel", "parallel"),
        name="diff_attn",
    )(q, k, v, lam_params, head_gain)


def kernel(x, a_norm, a_w_in, a_w_gate_up, a_b_gate, a_head_norm, a_w_out, kv_norm, w_kv, k_norm,
           b_norm, b_w_q, b_q_norm, b_lambda, b_head_norm, b_w_out, mlp_norm, mlp_w1, mlp_w2):
    batch, seq, d = x.shape
    bf = jnp.bfloat16
    row = lambda p: p.reshape(1, -1)
    h = x.reshape(batch * seq, d)

    n_main = 2 * GLA_HEADS * GLA_DK + 2 * GLA_HEADS * GLA_DV
    w_z = jnp.pad(a_w_in[0, :, n_main:], ((0, 0), (0, LANES - GLA_RANK))).astype(bf)
    w_gate = jnp.pad(a_w_gate_up[0], ((0, LANES - GLA_RANK), (0, 0))).astype(bf)
    qkvg, z, wo_bf, w1_bf, w2_bf = _gla_inproj(
        h, row(a_norm[0]), jnp.swapaxes(a_w_in, 1, 2), w_z, n_main,
        next_weights=((a_w_out, 0), (mlp_w1, 0), (mlp_w2, 0)))
    og = _gla_core(qkvg, z, w_gate, row(a_b_gate[0]), row(a_head_norm[0]), batch, seq)
    h = _outproj_mlp(og, h, wo_bf, row(mlp_norm[0]), w1_bf, w2_bf)

    layer = 1
    lambda_init = 0.8 - 0.6 * math.exp(-0.3 * layer)
    reps = SEG_TILE // DIFF_DH
    kn, v, qn, wo_bf, w1_bf, w2_bf = _diff_proj(
        h, row(kv_norm), row(b_norm[0]), w_kv, b_w_q,
        row(jnp.tile(k_norm, reps)), row(jnp.tile(b_q_norm[0], reps)),
        next_weights=((b_w_out, 0), (mlp_w1, 1), (mlp_w2, 1)))
    oa = _diff_attn(qn, kn, v, b_lambda[0], row(b_head_norm[0]), batch, seq, lambda_init)
    h = _outproj_mlp(oa, h, wo_bf, row(mlp_norm[1]), w1_bf, w2_bf)
    return h.reshape(batch, seq, d)
```

```python
import functools
import math

import numpy as np
import jax
import jax.numpy as jnp
from jax import lax
from jax.experimental import pallas as pl
from jax.experimental.pallas import tpu as pltpu

D_MODEL = 1024
GLA_HEADS = 4
GLA_DK = 128
GLA_DV = 256
GLA_RANK = 16
GLA_TAU = 16.0
GLA_CHUNK = 64
DIFF_HEADS = 8
DIFF_DH = 64
DIFF_VD = 128
MLP_HIDDEN = 4 * D_MODEL
NORM_EPS = 1e-6

LANES = 128
VMEM_LIMIT = 56 * 1024 * 1024
NEG_BIG = -1e30

ROW_TILE = 1024
PROJ_ROW_TILE = 1024
PROJ_COL_CHUNK = 768
GLA_ROWS = 2048
SEG_TILE = 256
ATT_TQ = 1024
ATT_TK = 512
ATT_ROW_BLOCK = 32
GLA_LEVELS = (32, 16, 8, 4, 2, 1)
SUBLANES = 8
_GLA_N_VPU_LEVELS = sum(m >= SUBLANES for m in GLA_LEVELS)

_NT = (((1,), (1,)), ((), ()))
_TN = (((0,), (0,)), ((), ()))


def _dot(a, b, dims=None):
    if dims is None:
        return jnp.dot(a, b, preferred_element_type=jnp.float32)
    return lax.dot_general(a, b, dims, preferred_element_type=jnp.float32)


def _params(*sem):
    return pltpu.CompilerParams(dimension_semantics=sem, vmem_limit_bytes=VMEM_LIMIT)


def _resident(shape):
    nd = len(shape)
    return pl.BlockSpec(shape, lambda *_: (0,) * nd, pipeline_mode=pl.Buffered(1))


def _layer_resident(w, layer):
    return pl.BlockSpec((None,) + w.shape[1:], lambda *_: (layer, 0, 0),
                        pipeline_mode=pl.Buffered(1))


def _rounding_job(stacked_weights, n_steps):
    inputs, in_specs, out_specs, out_shapes = [], [], [], []
    for w, layer in stacked_weights:
        rows, cols = w.shape[1] // n_steps, w.shape[2]
        inputs.append(w)
        in_specs.append(pl.BlockSpec((None, rows, cols), lambda i, layer=layer: (layer, i, 0)))
        out_specs.append(pl.BlockSpec((rows, cols), lambda i: (i, 0)))
        out_shapes.append(jax.ShapeDtypeStruct(w.shape[1:], jnp.bfloat16))
    return inputs, in_specs, out_specs, out_shapes


def _round_weights(n_jobs, refs):
    for src, dst in zip(refs[:n_jobs], refs[n_jobs:]):
        dst[...] = src[...].astype(dst.dtype)


def _rms_scale(x):
    return lax.rsqrt(jnp.mean(x * x, axis=-1, keepdims=True) + NORM_EPS)


def _gla_inproj_kernel(x_ref, g_ref, w_ref, wz_ref, *rest, n_jobs):
    o_ref, z_ref = rest[n_jobs:n_jobs + 2]
    _round_weights(n_jobs, rest[:n_jobs] + rest[n_jobs + 2:])
    x = x_ref[...]
    xn = (x * _rms_scale(x) * g_ref[...]).astype(jnp.bfloat16)
    n_out = o_ref.shape[1]
    step = PROJ_COL_CHUNK
    for n0 in range(0, n_out, step):
        w = w_ref[n0:n0 + step, :].astype(jnp.bfloat16)
        o_ref[:, n0:n0 + step] = _dot(xn, w, _NT).astype(o_ref.dtype)
    z_ref[...] = _dot(xn, wz_ref[...])


def _gla_inproj(x, gain, w_in, w_z, n, next_weights):
    t, d = x.shape
    n_steps = t // PROJ_ROW_TILE
    jobs, job_in, job_out, job_shapes = _rounding_job(next_weights, n_steps)
    return pl.pallas_call(
        functools.partial(_gla_inproj_kernel, n_jobs=len(jobs)),
        grid=(n_steps,),
        in_specs=[
            pl.BlockSpec((PROJ_ROW_TILE, d), lambda i: (i, 0)),
            _resident((1, d)),
            _layer_resident(w_in, 0),
            _resident((d, LANES)),
        ] + job_in,
        out_specs=[
            pl.BlockSpec((PROJ_ROW_TILE, n), lambda i: (i, 0)),
            pl.BlockSpec((PROJ_ROW_TILE, LANES), lambda i: (i, 0)),
        ] + job_out,
        out_shape=[
            jax.ShapeDtypeStruct((t, n), jnp.bfloat16),
            jax.ShapeDtypeStruct((t, LANES), jnp.float32),
        ] + job_shapes,
        compiler_params=_params("parallel"),
        name="gla_inproj",
    )(x, gain, w_in, w_z, *jobs)


def _gla_constants():
    c = GLA_CHUNK
    t = np.arange(c)
    row, col = t[:, None], t[None, :]
    mats = [col <= row]
    masks = []
    for m in GLA_LEVELS:
        blk = 2 * m
        start = (t // blk) * blk
        mid = (start + m)[:, None]
        second = ((t % blk) >= m)[:, None]
        q_side = second & (col > mid) & (col <= row)
        k_side = (~second) & (col > row) & (col <= mid)
        if m < SUBLANES:
            mats.append(q_side | k_side)
        masks.append((start[:, None] == start[None, :]) & second & (~second).T)
    masks.append(np.eye(c, dtype=bool))
    lm = np.concatenate(mats, axis=0).astype(np.float32)
    mk = np.stack(masks, axis=0).astype(np.float32)
    return lm, mk


def _gla_core_kernel(q_ref, k_ref, v_ref, g_ref, z_ref, wg_ref, bg_ref, hg_ref,
                     lm_ref, mk_ref, o_ref, st_scr):
    c = GLA_CHUNK
    rows = q_ref.shape[0]

    @pl.when(pl.program_id(2) == 0)
    def _():
        st_scr[...] = jnp.zeros_like(st_scr)

    logits = _dot(z_ref[...].astype(jnp.bfloat16), wg_ref[...]) + bg_ref[...]
    log_sig = jnp.minimum(logits, 0.0) - jnp.log(1.0 + jnp.exp(-jnp.abs(logits)))
    la = log_sig * (math.log2(math.e) / GLA_TAU)
    la_hi = la.astype(jnp.bfloat16)
    la_lo = (la - la_hi.astype(jnp.float32)).astype(jnp.bfloat16)

    scale = GLA_DK ** -0.5
    gain = hg_ref[...]
    n = rows // c
    rs = [slice(ci * c, (ci + 1) * c) for ci in range(n)]
    n_lvl = len(GLA_LEVELS)

    sums = []
    for ci in range(n):
        la_stack = jnp.concatenate([la_hi[rs[ci]], la_lo[rs[ci]]], axis=0)
        sums.append(_dot(lm_ref[...], la_stack))
    cum = [s[0:c] for s in sums]

    def level_exponent(ci, li):
        m = GLA_LEVELS[li]
        if m < SUBLANES:
            lo = (1 + li - _GLA_N_VPU_LEVELS) * c
            return sums[ci][lo:lo + c]
        b = cum[ci]
        parts = []
        for start in range(0, c, 2 * m):
            mid = b[start + m:start + m + 1, :]
            parts.append(mid - b[start:start + m])
            parts.append(b[start + m:start + 2 * m] - mid)
        return jnp.concatenate(parts, axis=0)

    qf = [q_ref[r, :].astype(jnp.float32) for r in rs]
    kf = [k_ref[r, :].astype(jnp.float32) for r in rs]
    attn = [mk_ref[n_lvl] * _dot(q_ref[r, :], k_ref[r, :], _NT) for r in rs]
    row_in_chunk = lax.broadcasted_iota(jnp.int32, (c, GLA_DK), 0)

    def level_operand(ci, li):
        m = GLA_LEVELS[li]
        if m < SUBLANES:
            return jnp.where(row_in_chunk % (2 * m) >= m, qf[ci], kf[ci])
        parts = []
        for start in range(0, c, 2 * m):
            parts.append(kf[ci][start:start + m])
            parts.append(qf[ci][start + m:start + 2 * m])
        return jnp.concatenate(parts, axis=0)

    for li in range(n_lvl):
        for ci in range(n):
            e = jnp.exp2(level_exponent(ci, li))
            x = (level_operand(ci, li) * e).astype(jnp.bfloat16)
            attn[ci] = attn[ci] + mk_ref[li] * _dot(x, x, _NT)

    o_intra, qd, decay, upd = [], [], [], []
    for ci in range(n):
        b = cum[ci]
        b_last = b[c - 1:c, :]
        v = v_ref[rs[ci], :]
        kd = (kf[ci] * jnp.exp2(b_last - b)).astype(jnp.bfloat16)
        qd.append((qf[ci] * jnp.exp2(b)).astype(jnp.bfloat16))
        decay.append(jnp.exp2(b_last))
        upd.append(_dot(v, kd, _TN))
        o_intra.append(_dot(attn[ci].astype(jnp.bfloat16), v))

    st = st_scr[...]
    for ci in range(n):
        o = (o_intra[ci] + _dot(qd[ci], st.astype(jnp.bfloat16), _NT)) * scale
        st = st * decay[ci] + upd[ci]
        on = o * _rms_scale(o) * gain
        t = 0.5 * g_ref[rs[ci], :].astype(jnp.float32)
        o_ref[rs[ci], :] = (on * (t + t * jnp.tanh(t))).astype(o_ref.dtype)
    st_scr[...] = st


def _gla_core(qkvg, z, w_gate, b_gate, head_gain, batch, seq):
    t = qkvg.shape[0]
    nblk = seq // GLA_ROWS
    lm, mk = _gla_constants()
    lm = jnp.asarray(np.concatenate([lm, lm], axis=1), jnp.bfloat16)
    mk = jnp.asarray(mk, jnp.float32)
    kq = (GLA_HEADS * GLA_DK) // GLA_DK
    kv = (2 * GLA_HEADS * GLA_DK) // GLA_DV
    kg = kv + GLA_HEADS
    row = lambda b, h, s: b * nblk + s
    return pl.pallas_call(
        _gla_core_kernel,
        grid=(batch, GLA_HEADS, nblk),
        in_specs=[
            pl.BlockSpec((GLA_ROWS, GLA_DK), lambda b, h, s: (row(b, h, s), h)),
            pl.BlockSpec((GLA_ROWS, GLA_DK), lambda b, h, s: (row(b, h, s), kq + h)),
            pl.BlockSpec((GLA_ROWS, GLA_DV), lambda b, h, s: (row(b, h, s), kv + h)),
            pl.BlockSpec((GLA_ROWS, GLA_DV), lambda b, h, s: (row(b, h, s), kg + h)),
            pl.BlockSpec((GLA_ROWS, LANES), lambda b, h, s: (row(b, h, s), 0)),
            pl.BlockSpec((LANES, GLA_DK), lambda b, h, s: (0, h)),
            pl.BlockSpec((1, GLA_DK), lambda b, h, s: (0, h)),
            pl.BlockSpec((1, GLA_DV), lambda b, h, s: (0, 0)),
            pl.BlockSpec(lm.shape, lambda b, h, s: (0, 0)),
            pl.BlockSpec(mk.shape, lambda b, h, s: (0, 0, 0)),
        ],
        out_specs=pl.BlockSpec((GLA_ROWS, GLA_DV), lambda b, h, s: (row(b, h, s), h)),
        out_shape=jax.ShapeDtypeStruct((t, GLA_HEADS * GLA_DV), jnp.bfloat16),
        scratch_shapes=[pltpu.VMEM((GLA_DV, GLA_DK), jnp.float32)],
        compiler_params=_params("parallel", "parallel", "arbitrary"),
        name="gla_core",
    )(qkvg, qkvg, qkvg, qkvg, z, w_gate, b_gate, head_gain, lm, mk)


def _outproj_mlp_kernel(a_ref, h_ref, wo_ref, g_ref, w1_ref, w2_ref, o_ref):
    bf = jnp.bfloat16
    h1 = h_ref[...] + _dot(a_ref[...], wo_ref[...])
    xn = (h1 * _rms_scale(h1) * g_ref[...]).astype(bf)
    acc = h1
    d = h1.shape[1]
    for c0 in range(0, w1_ref.shape[1], d):
        hid = jnp.maximum(_dot(xn, w1_ref[:, c0:c0 + d]), 0.0)
        acc = acc + _dot((hid * hid).astype(bf), w2_ref[c0:c0 + d, :])
    o_ref[...] = acc


def _outproj_mlp(a, h, w_out, gain, w1, w2):
    t, d = h.shape
    tile = pl.BlockSpec((ROW_TILE, d), lambda i: (i, 0))
    return pl.pallas_call(
        _outproj_mlp_kernel,
        grid=(t // ROW_TILE,),
        in_specs=[tile, tile, _resident(w_out.shape), _resident((1, d)),
                  _resident(w1.shape), _resident(w2.shape)],
        out_specs=tile,
        out_shape=jax.ShapeDtypeStruct((t, d), jnp.float32),
        compiler_params=_params("parallel"),
        name="outproj_mlp",
    )(a, h, w_out, gain, w1, w2)


def _segment_norm(y, seg_ones, gain):
    ss = _dot((y * y).astype(jnp.bfloat16), seg_ones)
    return y * lax.rsqrt(ss * (1.0 / DIFF_DH) + NORM_EPS) * gain


def _diff_proj_kernel(h_ref, gkv_ref, gq_ref, wkv_ref, wq_ref, kg_ref, qg_ref, so_ref,
                      *rest, n_jobs):
    k_ref, v_ref, q_ref = rest[n_jobs:n_jobs + 3]
    _round_weights(n_jobs, rest[:n_jobs] + rest[n_jobs + 3:])
    nk = k_ref.shape[1]
    seg_ones = so_ref[...]
    bf = jnp.bfloat16
    qscale = DIFF_DH ** -0.5 * math.log2(math.e)
    x = h_ref[...]
    xhat = x * _rms_scale(x)
    xkv = (xhat * gkv_ref[...]).astype(bf)
    xq = (xhat * gq_ref[...]).astype(bf)
    def norm_store(out_ref, gain_ref, scale, cols, raw):
        y = _segment_norm(raw, seg_ones, gain_ref[...])
        out_ref[:, cols] = (y if scale is None else y * scale).astype(out_ref.dtype)

    pending = None
    for x_in, w_ref, out_ref, gain_ref, scale in ((xkv, wkv_ref, k_ref, kg_ref, None),
                                                 (xq, wq_ref, q_ref, qg_ref, qscale)):
        for c0 in range(0, nk, SEG_TILE):
            cols = slice(c0, c0 + SEG_TILE)
            raw = _dot(x_in, w_ref[:, cols].astype(bf))
            if pending is not None:
                norm_store(*pending)
            pending = (out_ref, gain_ref, scale, cols, raw)
        if out_ref is k_ref:
            v_ref[...] = _dot(xkv, wkv_ref[:, nk:].astype(bf)).astype(v_ref.dtype)
    norm_store(*pending)


def _diff_proj(h, g_kv, g_q, w_kv, w_q, k_gain, q_gain, next_weights):
    t, d = h.shape
    n_steps = t // PROJ_ROW_TILE
    jobs, job_in, job_out, job_shapes = _rounding_job(next_weights, n_steps)
    nk = DIFF_HEADS * 2 * DIFF_DH
    nv = DIFF_HEADS * DIFF_VD
    lane = np.arange(SEG_TILE)
    seg_ones = jnp.asarray(lane[:, None] // DIFF_DH == lane[None, :] // DIFF_DH, jnp.bfloat16)
    tile = lambda n: pl.BlockSpec((PROJ_ROW_TILE, n), lambda i: (i, 0))
    return pl.pallas_call(
        functools.partial(_diff_proj_kernel, n_jobs=len(jobs)),
        grid=(n_steps,),
        in_specs=[tile(d), _resident((1, d)), _resident((1, d)), _resident(w_kv.shape),
                  _layer_resident(w_q, 0), _resident((1, SEG_TILE)), _resident((1, SEG_TILE)),
                  _resident((SEG_TILE, SEG_TILE))] + job_in,
        out_specs=[tile(nk), tile(nv), tile(nk)] + job_out,
        out_shape=[jax.ShapeDtypeStruct((t, nk), jnp.bfloat16),
                   jax.ShapeDtypeStruct((t, nv), jnp.bfloat16),
                   jax.ShapeDtypeStruct((t, nk), jnp.bfloat16)] + job_shapes,
        compiler_params=_params("parallel"),
        name="diff_proj",
    )(h, g_kv, g_q, w_kv, w_q, k_gain, q_gain, seg_ones, *jobs)


def _diff_attn_kernel(q_ref, k_ref, v_ref, lam_ref, hg_ref, o_ref,
                      m_scr, l_scr, acc_scr, s_bufs, p_bufs, *, lambda_init):
    tq = ATT_TQ
    tk = ATT_TK
    rb = ATT_ROW_BLOCK
    lane = lax.broadcasted_iota(jnp.int32, (tq, q_ref.shape[1]), 1)
    col_minus_row = (lax.broadcasted_iota(jnp.int32, (rb, tk), 1)
                     - lax.broadcasted_iota(jnp.int32, (rb, tk), 0))

    def block(j):
        return pl.ds(pl.multiple_of(j * tk, tk), tk)

    def q_tile(qi, carry):
        q_rows = pl.ds(pl.multiple_of(qi * tq, tq), tq)
        m_scr[...] = jnp.full_like(m_scr, NEG_BIG)
        l_scr[...] = jnp.zeros_like(l_scr)
        acc_scr[...] = jnp.zeros_like(acc_scr)

        q = q_ref[q_rows, :]
        qcs = [jnp.where((lane // DIFF_DH) == c, q, jnp.zeros_like(q)) for c in range(2)]

        def step(j, buf, row_lo, diag_row0):
            k = k_ref[block(j), :]
            v = v_ref[block(j), :]
            s_scr = s_bufs.at[buf]
            p_scr = p_bufs.at[buf]
            all_rows = slice(row_lo, tq)
            n_rows = tq - row_lo
            s_both = _dot(jnp.concatenate([qc[all_rows] for qc in qcs], axis=0), k, _NT)
            for c in range(2):
                s_scr[c, all_rows, :] = s_both[c * n_rows:(c + 1) * n_rows]
            for c in range(2):
                for r0 in range(row_lo, tq, rb):
                    rows = slice(r0, r0 + rb)
                    masked = diag_row0 is not None and r0 - diag_row0 < tk
                    kmax = min(tk, -(-(r0 - diag_row0 + rb) // LANES) * LANES) if masked else tk
                    s = s_scr[c, rows, 0:kmax]
                    if masked:
                        s = jnp.where(col_minus_row[:, 0:kmax] <= r0 - diag_row0, s, NEG_BIG)
                    m_prev = m_scr[c, rows, :]
                    m_new = jnp.maximum(m_prev, jnp.max(s, axis=-1, keepdims=True))
                    alpha = jnp.exp2(m_prev - m_new)
                    p = jnp.exp2(s - jnp.tile(m_new, (1, kmax // LANES)))
                    psum = p[:, 0:LANES]
                    for t0 in range(LANES, kmax, LANES):
                        psum = psum + p[:, t0:t0 + LANES]
                    l_scr[c, rows, :] = alpha * l_scr[c, rows, :] + psum
                    m_scr[c, rows, :] = m_new
                    acc_scr[c, rows, :] = alpha * acc_scr[c, rows, :]
                    p_scr[c, rows, 0:kmax] = p.astype(jnp.bfloat16)
                    if kmax < tk:
                        p_scr[c, rows, kmax:tk] = jnp.zeros((rb, tk - kmax), jnp.bfloat16)
                acc_scr[c, all_rows, :] += _dot(p_scr[c, all_rows, :], v)

        n_sub = tq // tk

        def pair(i, carry):
            step(2 * i, 0, 0, None)
            step(2 * i + 1, 1, 0, None)
            return carry

        lax.fori_loop(0, (n_sub // 2) * qi, pair, 0)
        for t in range(n_sub):
            step(n_sub * qi + t, t % 2, t * tk, t * tk)

        lp = lam_ref[...]
        lam = (jnp.exp(jnp.sum(lp[0:1] * lp[1:2], axis=-1, keepdims=True))
               - jnp.exp(jnp.sum(lp[2:3] * lp[3:4], axis=-1, keepdims=True)) + lambda_init)
        l0 = jnp.sum(l_scr[0], axis=-1, keepdims=True)
        l1 = jnp.sum(l_scr[1], axis=-1, keepdims=True)
        o = acc_scr[0] / l0 - lam * (acc_scr[1] / l1)
        on = o * _rms_scale(o) * hg_ref[...]
        o_ref[q_rows, :] = (on * (1.0 - lambda_init)).astype(o_ref.dtype)
        return carry

    lax.fori_loop(0, q_ref.shape[0] // tq, q_tile, 0)


def _diff_attn(q, k, v, lam_params, head_gain, batch, seq, lambda_init):
    t = q.shape[0]
    assert ATT_TQ % (2 * ATT_TK) == 0
    assert seq % ATT_TQ == 0
    kern = functools.partial(_diff_attn_kernel, lambda_init=lambda_init)
    head_block = lambda width: pl.BlockSpec((seq, width), lambda b, h: (b, h))
    return pl.pallas_call(
        kern,
        grid=(batch, DIFF_HEADS),
        in_specs=[
            head_block(2 * DIFF_DH),
            head_block(2 * DIFF_DH),
            head_block(DIFF_VD),
            pl.BlockSpec(lam_params.shape, lambda b, h: (0, 0)),
            pl.BlockSpec((1, DIFF_VD), lambda b, h: (0, 0)),
        ],
        out_specs=head_block(DIFF_VD),
        out_shape=jax.ShapeDtypeStruct((t, DIFF_HEADS * DIFF_VD), jnp.bfloat16),
        scratch_shapes=[
            pltpu.VMEM((2, ATT_TQ, LANES), jnp.float32),
            pltpu.VMEM((2, ATT_TQ, LANES), jnp.float32),
            pltpu.VMEM((2, ATT_TQ, DIFF_VD), jnp.float32),
            pltpu.VMEM((2, 2, ATT_TQ, ATT_TK), jnp.float32),
            pltpu.VMEM((2, 2, ATT_TQ, ATT_TK), jnp.bfloat16),
        ],
        compiler_params=_params("parallel", "parallel"),
        name="diff_attn",
    )(q, k, v, lam_params, head_gain)


def kernel(x, a_norm, a_w_in, a_w_gate_up, a_b_gate, a_head_norm, a_w_out, kv_norm, w_kv, k_norm,
           b_norm, b_w_q, b_q_norm, b_lambda, b_head_norm, b_w_out, mlp_norm, mlp_w1, mlp_w2):
    batch, seq, d = x.shape
    bf = jnp.bfloat16
    row = lambda p: p.reshape(1, -1)
    h = x.reshape(batch * seq, d)

    n_main = 2 * GLA_HEADS * GLA_DK + 2 * GLA_HEADS * GLA_DV
    w_z = jnp.pad(a_w_in[0, :, n_main:], ((0, 0), (0, LANES - GLA_RANK))).astype(bf)
    w_gate = jnp.pad(a_w_gate_up[0], ((0, LANES - GLA_RANK), (0, 0))).astype(bf)
    qkvg, z, wo_bf, w1_bf, w2_bf = _gla_inproj(
        h, row(a_norm[0]), jnp.swapaxes(a_w_in, 1, 2), w_z, n_main,
        next_weights=((a_w_out, 0), (mlp_w1, 0), (mlp_w2, 0)))
    og = _gla_core(qkvg, z, w_gate, row(a_b_gate[0]), row(a_head_norm[0]), batch, seq)
    h = _outproj_mlp(og, h, wo_bf, row(mlp_norm[0]), w1_bf, w2_bf)

    layer = 1
    lambda_init = 0.8 - 0.6 * math.exp(-0.3 * layer)
    reps = SEG_TILE // DIFF_DH
    kn, v, qn, wo_bf, w1_bf, w2_bf = _diff_proj(
        h, row(kv_norm), row(b_norm[0]), w_kv, b_w_q,
        row(jnp.tile(k_norm, reps)), row(jnp.tile(b_q_norm[0], reps)),
        next_weights=((b_w_out, 0), (mlp_w1, 1), (mlp_w2, 1)))
    oa = _diff_attn(qn, kn, v, b_lambda[0], row(b_head_norm[0]), batch, seq, lambda_init)
    h = _outproj_mlp(oa, h, wo_bf, row(mlp_norm[1]), w1_bf, w2_bf)
    return h.reshape(batch, seq, d)
```

```python
import functools
import math

import numpy as np
import jax
import jax.numpy as jnp
from jax import lax
from jax.experimental import pallas as pl
from jax.experimental.pallas import tpu as pltpu

D_MODEL = 1024
GLA_HEADS = 4
GLA_DK = 128
GLA_DV = 256
GLA_RANK = 16
GLA_TAU = 16.0
GLA_CHUNK = 64
DIFF_HEADS = 8
DIFF_DH = 64
DIFF_VD = 128
MLP_HIDDEN = 4 * D_MODEL
NORM_EPS = 1e-6

LANES = 128
VMEM_LIMIT = 56 * 1024 * 1024
NEG_BIG = -1e30

ROW_TILE = 1024
PROJ_ROW_TILE = 1024
PROJ_COL_CHUNK = 768
GLA_ROWS = 2048
SEG_TILE = 256
ATT_TQ = 1024
ATT_TK = 512
ATT_ROW_BLOCK = 32
GLA_LEVELS = (32, 16, 8, 4, 2, 1)
SUBLANES = 8
_GLA_N_VPU_LEVELS = sum(m >= SUBLANES for m in GLA_LEVELS)

_NT = (((1,), (1,)), ((), ()))
_TN = (((0,), (0,)), ((), ()))


def _dot(a, b, dims=None):
    if dims is None:
        return jnp.dot(a, b, preferred_element_type=jnp.float32)
    return lax.dot_general(a, b, dims, preferred_element_type=jnp.float32)


def _params(*sem):
    return pltpu.CompilerParams(dimension_semantics=sem, vmem_limit_bytes=VMEM_LIMIT)


def _resident(shape):
    nd = len(shape)
    return pl.BlockSpec(shape, lambda *_: (0,) * nd, pipeline_mode=pl.Buffered(1))


def _layer_resident(w, layer):
    return pl.BlockSpec((None,) + w.shape[1:], lambda *_: (layer, 0, 0),
                        pipeline_mode=pl.Buffered(1))


def _rounding_job(stacked_weights, n_steps):
    inputs, in_specs, out_specs, out_shapes = [], [], [], []
    for w, layer in stacked_weights:
        rows, cols = w.shape[1] // n_steps, w.shape[2]
        inputs.append(w)
        in_specs.append(pl.BlockSpec((None, rows, cols), lambda i, layer=layer: (layer, i, 0)))
        out_specs.append(pl.BlockSpec((rows, cols), lambda i: (i, 0)))
        out_shapes.append(jax.ShapeDtypeStruct(w.shape[1:], jnp.bfloat16))
    return inputs, in_specs, out_specs, out_shapes


def _round_weights(n_jobs, refs):
    for src, dst in zip(refs[:n_jobs], refs[n_jobs:]):
        dst[...] = src[...].astype(dst.dtype)


def _rms_scale(x):
    return lax.rsqrt(jnp.mean(x * x, axis=-1, keepdims=True) + NORM_EPS)


def _gla_inproj_kernel(x_ref, g_ref, w_ref, wz_ref, *rest, n_jobs):
    o_ref, z_ref = rest[n_jobs:n_jobs + 2]
    _round_weights(n_jobs, rest[:n_jobs] + rest[n_jobs + 2:])
    x = x_ref[...]
    xn = (x * _rms_scale(x) * g_ref[...]).astype(jnp.bfloat16)
    n_out = o_ref.shape[1]
    step = PROJ_COL_CHUNK
    for n0 in range(0, n_out, step):
        w = w_ref[n0:n0 + step, :].astype(jnp.bfloat16)
        o_ref[:, n0:n0 + step] = _dot(xn, w, _NT).astype(o_ref.dtype)
    z_ref[...] = _dot(xn, wz_ref[...])


def _gla_inproj(x, gain, w_in, w_z, n, next_weights):
    t, d = x.shape
    n_steps = t // PROJ_ROW_TILE
    jobs, job_in, job_out, job_shapes = _rounding_job(next_weights, n_steps)
    return pl.pallas_call(
        functools.partial(_gla_inproj_kernel, n_jobs=len(jobs)),
        grid=(n_steps,),
        in_specs=[
            pl.BlockSpec((PROJ_ROW_TILE, d), lambda i: (i, 0)),
            _resident((1, d)),
            _layer_resident(w_in, 0),
            _resident((d, LANES)),
        ] + job_in,
        out_specs=[
            pl.BlockSpec((PROJ_ROW_TILE, n), lambda i: (i, 0)),
            pl.BlockSpec((PROJ_ROW_TILE, LANES), lambda i: (i, 0)),
        ] + job_out,
        out_shape=[
            jax.ShapeDtypeStruct((t, n), jnp.bfloat16),
            jax.ShapeDtypeStruct((t, LANES), jnp.float32),
        ] + job_shapes,
        compiler_params=_params("parallel"),
        name="gla_inproj",
    )(x, gain, w_in, w_z, *jobs)


def _gla_constants():
    c = GLA_CHUNK
    t = np.arange(c)
    row, col = t[:, None], t[None, :]
    mats = [col <= row]
    masks = []
    for m in GLA_LEVELS:
        blk = 2 * m
        start = (t // blk) * blk
        mid = (start + m)[:, None]
        second = ((t % blk) >= m)[:, None]
        q_side = second & (col > mid) & (col <= row)
        k_side = (~second) & (col > row) & (col <= mid)
        if m < SUBLANES:
            mats.append(q_side | k_side)
        masks.append((start[:, None] == start[None, :]) & second & (~second).T)
    masks.append(np.eye(c, dtype=bool))
    lm = np.concatenate(mats, axis=0).astype(np.float32)
    mk = np.stack(masks, axis=0).astype(np.float32)
    return lm, mk


def _gla_core_kernel(q_ref, k_ref, v_ref, g_ref, z_ref, wg_ref, bg_ref, hg_ref,
                     lm_ref, mk_ref, o_ref, st_scr):
    c = GLA_CHUNK
    rows = q_ref.shape[0]

    @pl.when(pl.program_id(2) == 0)
    def _():
        st_scr[...] = jnp.zeros_like(st_scr)

    logits = _dot(z_ref[...].astype(jnp.bfloat16), wg_ref[...]) + bg_ref[...]
    log_sig = jnp.minimum(logits, 0.0) - jnp.log(1.0 + jnp.exp(-jnp.abs(logits)))
    la = log_sig * (math.log2(math.e) / GLA_TAU)
    la_hi = la.astype(jnp.bfloat16)
    la_lo = (la - la_hi.astype(jnp.float32)).astype(jnp.bfloat16)

    gain = hg_ref[...]
    n = rows // c
    rs = [slice(ci * c, (ci + 1) * c) for ci in range(n)]
    n_lvl = len(GLA_LEVELS)

    sums = []
    for ci in range(n):
        la_stack = jnp.concatenate([la_hi[rs[ci]], la_lo[rs[ci]]], axis=0)
        sums.append(_dot(lm_ref[...], la_stack))
    cum = [s[0:c] for s in sums]

    def level_exponent(ci, li):
        m = GLA_LEVELS[li]
        if m < SUBLANES:
            lo = (1 + li - _GLA_N_VPU_LEVELS) * c
            return sums[ci][lo:lo + c]
        b = cum[ci]
        parts = []
        for start in range(0, c, 2 * m):
            mid = b[start + m:start + m + 1, :]
            parts.append(mid - b[start:start + m])
            parts.append(b[start + m:start + 2 * m] - mid)
        return jnp.concatenate(parts, axis=0)

    qf = [q_ref[r, :].astype(jnp.float32) for r in rs]
    kf = [k_ref[r, :].astype(jnp.float32) for r in rs]
    attn = [mk_ref[n_lvl] * _dot(q_ref[r, :], k_ref[r, :], _NT) for r in rs]
    row_in_chunk = lax.broadcasted_iota(jnp.int32, (c, GLA_DK), 0)

    def level_operand(ci, li):
        m = GLA_LEVELS[li]
        if m < SUBLANES:
            return jnp.where(row_in_chunk % (2 * m) >= m, qf[ci], kf[ci])
        parts = []
        for start in range(0, c, 2 * m):
            parts.append(kf[ci][start:start + m])
            parts.append(qf[ci][start + m:start + 2 * m])
        return jnp.concatenate(parts, axis=0)

    for li in range(n_lvl):
        for ci in range(n):
            e = jnp.exp2(level_exponent(ci, li))
            x = (level_operand(ci, li) * e).astype(jnp.bfloat16)
            attn[ci] = attn[ci] + mk_ref[li] * _dot(x, x, _NT)

    o_intra, qd, decay, upd = [], [], [], []
    for ci in range(n):
        b = cum[ci]
        b_last = b[c - 1:c, :]
        v = v_ref[rs[ci], :]
        kd = (kf[ci] * jnp.exp2(b_last - b)).astype(jnp.bfloat16)
        qd.append((qf[ci] * jnp.exp2(b)).astype(jnp.bfloat16))
        decay.append(jnp.exp2(b_last))
        upd.append(_dot(v, kd, _TN))
        o_intra.append(_dot(attn[ci].astype(jnp.bfloat16), v))

    st = st_scr[...]
    for ci in range(n):
        o = o_intra[ci] + _dot(qd[ci], st.astype(jnp.bfloat16), _NT)
        st = st * decay[ci] + upd[ci]
        ms = jnp.mean(o * o, axis=-1, keepdims=True)
        on = o * lax.rsqrt(ms + NORM_EPS * GLA_DK) * gain
        t = 0.5 * g_ref[rs[ci], :].astype(jnp.float32)
        o_ref[rs[ci], :] = (on * (t + t * jnp.tanh(t))).astype(o_ref.dtype)
    st_scr[...] = st


def _gla_core(qkvg, z, w_gate, b_gate, head_gain, batch, seq):
    t = qkvg.shape[0]
    nblk = seq // GLA_ROWS
    lm, mk = _gla_constants()
    lm = jnp.asarray(np.concatenate([lm, lm], axis=1), jnp.bfloat16)
    mk = jnp.asarray(mk, jnp.float32)
    kq = (GLA_HEADS * GLA_DK) // GLA_DK
    kv = (2 * GLA_HEADS * GLA_DK) // GLA_DV
    kg = kv + GLA_HEADS
    row = lambda b, h, s: b * nblk + s
    return pl.pallas_call(
        _gla_core_kernel,
        grid=(batch, GLA_HEADS, nblk),
        in_specs=[
            pl.BlockSpec((GLA_ROWS, GLA_DK), lambda b, h, s: (row(b, h, s), h)),
            pl.BlockSpec((GLA_ROWS, GLA_DK), lambda b, h, s: (row(b, h, s), kq + h)),
            pl.BlockSpec((GLA_ROWS, GLA_DV), lambda b, h, s: (row(b, h, s), kv + h)),
            pl.BlockSpec((GLA_ROWS, GLA_DV), lambda b, h, s: (row(b, h, s), kg + h)),
            pl.BlockSpec((GLA_ROWS, LANES), lambda b, h, s: (row(b, h, s), 0)),
            pl.BlockSpec((LANES, GLA_DK), lambda b, h, s: (0, h)),
            pl.BlockSpec((1, GLA_DK), lambda b, h, s: (0, h)),
            pl.BlockSpec((1, GLA_DV), lambda b, h, s: (0, 0)),
            pl.BlockSpec(lm.shape, lambda b, h, s: (0, 0)),
            pl.BlockSpec(mk.shape, lambda b, h, s: (0, 0, 0)),
        ],
        out_specs=pl.BlockSpec((GLA_ROWS, GLA_DV), lambda b, h, s: (row(b, h, s), h)),
        out_shape=jax.ShapeDtypeStruct((t, GLA_HEADS * GLA_DV), jnp.bfloat16),
        scratch_shapes=[pltpu.VMEM((GLA_DV, GLA_DK), jnp.float32)],
        compiler_params=_params("parallel", "parallel", "arbitrary"),
        name="gla_core",
    )(qkvg, qkvg, qkvg, qkvg, z, w_gate, b_gate, head_gain, lm, mk)


def _outproj_mlp_kernel(a_ref, h_ref, wo_ref, g_ref, w1_ref, w2_ref, o_ref):
    bf = jnp.bfloat16
    h1 = h_ref[...] + _dot(a_ref[...], wo_ref[...])
    xn = (h1 * _rms_scale(h1) * g_ref[...]).astype(bf)
    acc = h1
    d = h1.shape[1]
    for c0 in range(0, w1_ref.shape[1], d):
        hid = jnp.maximum(_dot(xn, w1_ref[:, c0:c0 + d]), 0.0)
        acc = acc + _dot((hid * hid).astype(bf), w2_ref[c0:c0 + d, :])
    o_ref[...] = acc


def _outproj_mlp(a, h, w_out, gain, w1, w2):
    t, d = h.shape
    tile = pl.BlockSpec((ROW_TILE, d), lambda i: (i, 0))
    return pl.pallas_call(
        _outproj_mlp_kernel,
        grid=(t // ROW_TILE,),
        in_specs=[tile, tile, _resident(w_out.shape), _resident((1, d)),
                  _resident(w1.shape), _resident(w2.shape)],
        out_specs=tile,
        out_shape=jax.ShapeDtypeStruct((t, d), jnp.float32),
        compiler_params=_params("parallel"),
        name="outproj_mlp",
    )(a, h, w_out, gain, w1, w2)


def _segment_norm(y, seg_ones, gain):
    ss = _dot((y * y).astype(jnp.bfloat16), seg_ones)
    return y * lax.rsqrt(ss * (1.0 / DIFF_DH) + NORM_EPS) * gain


def _diff_proj_kernel(h_ref, gkv_ref, gq_ref, wkv_ref, wq_ref, kg_ref, qg_ref, so_ref,
                      *rest, n_jobs):
    k_ref, v_ref, q_ref = rest[n_jobs:n_jobs + 3]
    _round_weights(n_jobs, rest[:n_jobs] + rest[n_jobs + 3:])
    nk = k_ref.shape[1]
    seg_ones = so_ref[...]
    bf = jnp.bfloat16
    x = h_ref[...]
    xhat = x * _rms_scale(x)
    xkv = (xhat * gkv_ref[...]).astype(bf)
    xq = (xhat * gq_ref[...]).astype(bf)
    def norm_store(out_ref, gain_ref, cols, raw):
        out_ref[:, cols] = _segment_norm(raw, seg_ones, gain_ref[...]).astype(out_ref.dtype)

    pending = None
    for x_in, w_ref, out_ref, gain_ref in ((xkv, wkv_ref, k_ref, kg_ref),
                                          (xq, wq_ref, q_ref, qg_ref)):
        for c0 in range(0, nk, SEG_TILE):
            cols = slice(c0, c0 + SEG_TILE)
            raw = _dot(x_in, w_ref[:, cols].astype(bf))
            if pending is not None:
                norm_store(*pending)
            pending = (out_ref, gain_ref, cols, raw)
        if out_ref is k_ref:
            v_ref[...] = _dot(xkv, wkv_ref[:, nk:].astype(bf)).astype(v_ref.dtype)
    norm_store(*pending)


def _diff_proj(h, g_kv, g_q, w_kv, w_q, k_gain, q_gain, next_weights):
    t, d = h.shape
    n_steps = t // PROJ_ROW_TILE
    jobs, job_in, job_out, job_shapes = _rounding_job(next_weights, n_steps)
    nk = DIFF_HEADS * 2 * DIFF_DH
    nv = DIFF_HEADS * DIFF_VD
    lane = np.arange(SEG_TILE)
    seg_ones = jnp.asarray(lane[:, None] // DIFF_DH == lane[None, :] // DIFF_DH, jnp.bfloat16)
    tile = lambda n: pl.BlockSpec((PROJ_ROW_TILE, n), lambda i: (i, 0))
    return pl.pallas_call(
        functools.partial(_diff_proj_kernel, n_jobs=len(jobs)),
        grid=(n_steps,),
        in_specs=[tile(d), _resident((1, d)), _resident((1, d)), _resident(w_kv.shape),
                  _layer_resident(w_q, 0), _resident((1, SEG_TILE)), _resident((1, SEG_TILE)),
                  _resident((SEG_TILE, SEG_TILE))] + job_in,
        out_specs=[tile(nk), tile(nv), tile(nk)] + job_out,
        out_shape=[jax.ShapeDtypeStruct((t, nk), jnp.bfloat16),
                   jax.ShapeDtypeStruct((t, nv), jnp.bfloat16),
                   jax.ShapeDtypeStruct((t, nk), jnp.bfloat16)] + job_shapes,
        compiler_params=_params("parallel"),
        name="diff_proj",
    )(h, g_kv, g_q, w_kv, w_q, k_gain, q_gain, seg_ones, *jobs)


def _diff_attn_kernel(q_ref, k_ref, v_ref, lam_ref, hg_ref, o_ref,
                      m_scr, l_scr, acc_scr, s_bufs, p_bufs, *, lambda_init):
    tq = ATT_TQ
    tk = ATT_TK
    rb = ATT_ROW_BLOCK
    lane = lax.broadcasted_iota(jnp.int32, (tq, q_ref.shape[1]), 1)
    col_minus_row = (lax.broadcasted_iota(jnp.int32, (rb, tk), 1)
                     - lax.broadcasted_iota(jnp.int32, (rb, tk), 0))

    def block(j):
        return pl.ds(pl.multiple_of(j * tk, tk), tk)

    def q_tile(qi, carry):
        q_rows = pl.ds(pl.multiple_of(qi * tq, tq), tq)
        m_scr[...] = jnp.full_like(m_scr, NEG_BIG)
        l_scr[...] = jnp.zeros_like(l_scr)
        acc_scr[...] = jnp.zeros_like(acc_scr)

        q = q_ref[q_rows, :]
        qcs = [jnp.where((lane // DIFF_DH) == c, q, jnp.zeros_like(q)) for c in range(2)]

        def step(j, buf, row_lo, diag_row0):
            k = k_ref[block(j), :]
            v = v_ref[block(j), :]
            s_scr = s_bufs.at[buf]
            p_scr = p_bufs.at[buf]
            all_rows = slice(row_lo, tq)
            for c in range(2):
                s_scr[c, all_rows, :] = _dot(qcs[c][all_rows], k, _NT)
            for c in range(2):
                for r0 in range(row_lo, tq, rb):
                    rows = slice(r0, r0 + rb)
                    masked = diag_row0 is not None and r0 - diag_row0 < tk
                    kmax = min(tk, -(-(r0 - diag_row0 + rb) // LANES) * LANES) if masked else tk
                    s = s_scr[c, rows, 0:kmax]
                    if masked:
                        s = jnp.where(col_minus_row[:, 0:kmax] <= r0 - diag_row0, s, NEG_BIG)
                    m_prev = m_scr[c, rows, :]
                    m_new = jnp.maximum(m_prev, jnp.max(s, axis=-1, keepdims=True))
                    alpha = jnp.exp2(m_prev - m_new)
                    p = jnp.exp2(s - jnp.tile(m_new, (1, kmax // LANES)))
                    psum = p[:, 0:LANES]
                    for t0 in range(LANES, kmax, LANES):
                        psum = psum + p[:, t0:t0 + LANES]
                    l_scr[c, rows, :] = alpha * l_scr[c, rows, :] + psum
                    m_scr[c, rows, :] = m_new
                    acc_scr[c, rows, :] = alpha * acc_scr[c, rows, :]
                    p_scr[c, rows, 0:kmax] = p.astype(jnp.bfloat16)
                    if kmax < tk:
                        p_scr[c, rows, kmax:tk] = jnp.zeros((rb, tk - kmax), jnp.bfloat16)
                acc_scr[c, all_rows, :] += _dot(p_scr[c, all_rows, :], v)

        n_sub = tq // tk

        def pair(i, carry):
            step(2 * i, 0, 0, None)
            step(2 * i + 1, 1, 0, None)
            return carry

        lax.fori_loop(0, (n_sub // 2) * qi, pair, 0)
        for t in range(n_sub):
            step(n_sub * qi + t, t % 2, t * tk, t * tk)

        lp = lam_ref[...]
        lam = (jnp.exp(jnp.sum(lp[0:1] * lp[1:2], axis=-1, keepdims=True))
               - jnp.exp(jnp.sum(lp[2:3] * lp[3:4], axis=-1, keepdims=True)) + lambda_init)
        l0 = jnp.sum(l_scr[0], axis=-1, keepdims=True)
        l1 = jnp.sum(l_scr[1], axis=-1, keepdims=True)
        o = acc_scr[0] / l0 - lam * (acc_scr[1] / l1)
        o_ref[q_rows, :] = (o * _rms_scale(o) * hg_ref[...]).astype(o_ref.dtype)
        return carry

    lax.fori_loop(0, q_ref.shape[0] // tq, q_tile, 0)


def _diff_attn(q, k, v, lam_params, head_gain, batch, seq, lambda_init):
    t = q.shape[0]
    assert ATT_TQ % (2 * ATT_TK) == 0
    assert seq % ATT_TQ == 0
    kern = functools.partial(_diff_attn_kernel, lambda_init=lambda_init)
    head_block = lambda width: pl.BlockSpec((seq, width), lambda b, h: (b, h))
    return pl.pallas_call(
        kern,
        grid=(batch, DIFF_HEADS),
        in_specs=[
            head_block(2 * DIFF_DH),
            head_block(2 * DIFF_DH),
            head_block(DIFF_VD),
            pl.BlockSpec(lam_params.shape, lambda b, h: (0, 0)),
            pl.BlockSpec((1, DIFF_VD), lambda b, h: (0, 0)),
        ],
        out_specs=head_block(DIFF_VD),
        out_shape=jax.ShapeDtypeStruct((t, DIFF_HEADS * DIFF_VD), jnp.bfloat16),
        scratch_shapes=[
            pltpu.VMEM((2, ATT_TQ, LANES), jnp.float32),
            pltpu.VMEM((2, ATT_TQ, LANES), jnp.float32),
            pltpu.VMEM((2, ATT_TQ, DIFF_VD), jnp.float32),
            pltpu.VMEM((2, 2, ATT_TQ, ATT_TK), jnp.float32),
            pltpu.VMEM((2, 2, ATT_TQ, ATT_TK), jnp.bfloat16),
        ],
        compiler_params=_params("parallel", "parallel"),
        name="diff_attn",
    )(q, k, v, lam_params, head_gain)


def kernel(x, a_norm, a_w_in, a_w_gate_up, a_b_gate, a_head_norm, a_w_out, kv_norm, w_kv, k_norm,
           b_norm, b_w_q, b_q_norm, b_lambda, b_head_norm, b_w_out, mlp_norm, mlp_w1, mlp_w2):
    batch, seq, d = x.shape
    bf = jnp.bfloat16
    row = lambda p: p.reshape(1, -1)
    h = x.reshape(batch * seq, d)

    n_main = 2 * GLA_HEADS * GLA_DK + 2 * GLA_HEADS * GLA_DV
    w_z = jnp.pad(a_w_in[0, :, n_main:], ((0, 0), (0, LANES - GLA_RANK))).astype(bf)
    w_gate = jnp.pad(a_w_gate_up[0], ((0, LANES - GLA_RANK), (0, 0))).astype(bf)
    qkvg, z, wo_bf, w1_bf, w2_bf = _gla_inproj(
        h, row(a_norm[0]), jnp.swapaxes(a_w_in, 1, 2), w_z, n_main,
        next_weights=((a_w_out, 0), (mlp_w1, 0), (mlp_w2, 0)))
    og = _gla_core(qkvg, z, w_gate, row(a_b_gate[0]), row(a_head_norm[0]), batch, seq)
    h = _outproj_mlp(og, h, wo_bf, row(mlp_norm[0]), w1_bf, w2_bf)

    layer = 1
    lambda_init = 0.8 - 0.6 * math.exp(-0.3 * layer)
    reps = SEG_TILE // DIFF_DH
    qscale = DIFF_DH ** -0.5 * math.log2(math.e)
    kn, v, qn, wo_bf, w1_bf, w2_bf = _diff_proj(
        h, row(kv_norm), row(b_norm[0]), w_kv, b_w_q,
        row(jnp.tile(k_norm, reps)), row(jnp.tile(b_q_norm[0], reps)) * qscale,
        next_weights=((b_w_out, 0), (mlp_w1, 1), (mlp_w2, 1)))
    oa = _diff_attn(qn, kn, v, b_lambda[0], row(b_head_norm[0]) * (1.0 - lambda_init),
                    batch, seq, lambda_init)
    h = _outproj_mlp(oa, h, wo_bf, row(mlp_norm[1]), w1_bf, w2_bf)
    return h.reshape(batch, seq, d)
```

```python
import functools
import math

import numpy as np
import jax
import jax.numpy as jnp
from jax import lax
from jax.experimental import pallas as pl
from jax.experimental.pallas import tpu as pltpu

D_MODEL = 1024
GLA_HEADS = 4
GLA_DK = 128
GLA_DV = 256
GLA_RANK = 16
GLA_TAU = 16.0
GLA_CHUNK = 64
DIFF_HEADS = 8
DIFF_DH = 64
DIFF_VD = 128
MLP_HIDDEN = 4 * D_MODEL
NORM_EPS = 1e-6

LANES = 128
VMEM_LIMIT = 56 * 1024 * 1024
NEG_BIG = -1e30

ROW_TILE = 1024
PROJ_ROW_TILE = 1024
PROJ_COL_CHUNK = 768
GLA_ROWS = 2048
SEG_TILE = 256
ATT_TQ = 1024
ATT_TK = 512
ATT_ROW_BLOCK = 32
GLA_LEVELS = (32, 16, 8, 4, 2, 1)
SUBLANES = 8
_GLA_N_VPU_LEVELS = sum(m >= SUBLANES for m in GLA_LEVELS)

_NT = (((1,), (1,)), ((), ()))
_TN = (((0,), (0,)), ((), ()))


def _dot(a, b, dims=None):
    if dims is None:
        return jnp.dot(a, b, preferred_element_type=jnp.float32)
    return lax.dot_general(a, b, dims, preferred_element_type=jnp.float32)


def _params(*sem):
    return pltpu.CompilerParams(dimension_semantics=sem, vmem_limit_bytes=VMEM_LIMIT)


def _resident(shape):
    nd = len(shape)
    return pl.BlockSpec(shape, lambda *_: (0,) * nd, pipeline_mode=pl.Buffered(1))


def _layer_resident(w, layer):
    return pl.BlockSpec((None,) + w.shape[1:], lambda *_: (layer, 0, 0),
                        pipeline_mode=pl.Buffered(1))


def _rounding_job(stacked_weights, n_steps):
    inputs, in_specs, out_specs, out_shapes = [], [], [], []
    for w, layer in stacked_weights:
        rows, cols = w.shape[1] // n_steps, w.shape[2]
        inputs.append(w)
        in_specs.append(pl.BlockSpec((None, rows, cols), lambda i, layer=layer: (layer, i, 0)))
        out_specs.append(pl.BlockSpec((rows, cols), lambda i: (i, 0)))
        out_shapes.append(jax.ShapeDtypeStruct(w.shape[1:], jnp.bfloat16))
    return inputs, in_specs, out_specs, out_shapes


def _round_weights(n_jobs, refs):
    for src, dst in zip(refs[:n_jobs], refs[n_jobs:]):
        dst[...] = src[...].astype(dst.dtype)


def _rms_scale(x):
    return lax.rsqrt(jnp.mean(x * x, axis=-1, keepdims=True) + NORM_EPS)


def _gla_inproj_kernel(x_ref, g_ref, w_ref, wz_ref, *rest, n_jobs):
    o_ref, z_ref = rest[n_jobs:n_jobs + 2]
    _round_weights(n_jobs, rest[:n_jobs] + rest[n_jobs + 2:])
    x = x_ref[...]
    xn = (x * _rms_scale(x) * g_ref[...]).astype(jnp.bfloat16)
    n_out = o_ref.shape[1]
    step = PROJ_COL_CHUNK
    for n0 in range(0, n_out, step):
        w = w_ref[n0:n0 + step, :].astype(jnp.bfloat16)
        o_ref[:, n0:n0 + step] = _dot(xn, w, _NT).astype(o_ref.dtype)
    z_ref[...] = _dot(xn, wz_ref[...])


def _gla_inproj(x, gain, w_in, w_z, n, next_weights):
    t, d = x.shape
    n_steps = t // PROJ_ROW_TILE
    jobs, job_in, job_out, job_shapes = _rounding_job(next_weights, n_steps)
    return pl.pallas_call(
        functools.partial(_gla_inproj_kernel, n_jobs=len(jobs)),
        grid=(n_steps,),
        in_specs=[
            pl.BlockSpec((PROJ_ROW_TILE, d), lambda i: (i, 0)),
            _resident((1, d)),
            _layer_resident(w_in, 0),
            _resident((d, LANES)),
        ] + job_in,
        out_specs=[
            pl.BlockSpec((PROJ_ROW_TILE, n), lambda i: (i, 0)),
            pl.BlockSpec((PROJ_ROW_TILE, LANES), lambda i: (i, 0)),
        ] + job_out,
        out_shape=[
            jax.ShapeDtypeStruct((t, n), jnp.bfloat16),
            jax.ShapeDtypeStruct((t, LANES), jnp.float32),
        ] + job_shapes,
        compiler_params=_params("parallel"),
        name="gla_inproj",
    )(x, gain, w_in, w_z, *jobs)


def _gla_constants():
    c = GLA_CHUNK
    t = np.arange(c)
    row, col = t[:, None], t[None, :]
    mats = [col <= row]
    masks = []
    for m in GLA_LEVELS:
        blk = 2 * m
        start = (t // blk) * blk
        mid = (start + m)[:, None]
        second = ((t % blk) >= m)[:, None]
        q_side = second & (col > mid) & (col <= row)
        k_side = (~second) & (col > row) & (col <= mid)
        if m < SUBLANES:
            mats.append(q_side | k_side)
        masks.append((start[:, None] == start[None, :]) & second & (~second).T)
    masks.append(np.eye(c, dtype=bool))
    lm = np.concatenate(mats, axis=0).astype(np.float32)
    mk = np.stack(masks, axis=0).astype(np.float32)
    return lm, mk


def _gla_core_kernel(q_ref, k_ref, v_ref, g_ref, z_ref, wg_ref, bg_ref, hg_ref,
                     lm_ref, mk_ref, o_ref, st_scr):
    c = GLA_CHUNK
    rows = q_ref.shape[0]

    @pl.when(pl.program_id(2) == 0)
    def _():
        st_scr[...] = jnp.zeros_like(st_scr)

    logits = _dot(z_ref[...].astype(jnp.bfloat16), wg_ref[...]) + bg_ref[...]
    log_sig = jnp.minimum(logits, 0.0) - jnp.log(1.0 + jnp.exp(-jnp.abs(logits)))
    la = log_sig * (math.log2(math.e) / GLA_TAU)
    la_hi = la.astype(jnp.bfloat16)
    la_lo = (la - la_hi.astype(jnp.float32)).astype(jnp.bfloat16)

    scale = GLA_DK ** -0.5
    gain = hg_ref[...]
    n = rows // c
    rs = [slice(ci * c, (ci + 1) * c) for ci in range(n)]
    n_lvl = len(GLA_LEVELS)

    sums = []
    for ci in range(n):
        la_stack = jnp.concatenate([la_hi[rs[ci]], la_lo[rs[ci]]], axis=0)
        sums.append(_dot(lm_ref[...], la_stack))
    cum = [s[0:c] for s in sums]

    def level_exponent(ci, li):
        m = GLA_LEVELS[li]
        if m < SUBLANES:
            lo = (1 + li - _GLA_N_VPU_LEVELS) * c
            return sums[ci][lo:lo + c]
        b = cum[ci]
        parts = []
        for start in range(0, c, 2 * m):
            mid = b[start + m:start + m + 1, :]
            parts.append(mid - b[start:start + m])
            parts.append(b[start + m:start + 2 * m] - mid)
        return jnp.concatenate(parts, axis=0)

    qf = [q_ref[r, :].astype(jnp.float32) for r in rs]
    kf = [k_ref[r, :].astype(jnp.float32) for r in rs]
    attn = [mk_ref[n_lvl] * _dot(q_ref[r, :], k_ref[r, :], _NT) for r in rs]
    row_in_chunk = lax.broadcasted_iota(jnp.int32, (c, GLA_DK), 0)

    def level_operand(ci, li):
        m = GLA_LEVELS[li]
        if m < SUBLANES:
            return jnp.where(row_in_chunk % (2 * m) >= m, qf[ci], kf[ci])
        parts = []
        for start in range(0, c, 2 * m):
            parts.append(kf[ci][start:start + m])
            parts.append(qf[ci][start + m:start + 2 * m])
        return jnp.concatenate(parts, axis=0)

    for li in range(n_lvl):
        for ci in range(n):
            e = jnp.exp2(level_exponent(ci, li))
            x = (level_operand(ci, li) * e).astype(jnp.bfloat16)
            attn[ci] = attn[ci] + mk_ref[li] * _dot(x, x, _NT)

    o_intra, qd, decay, upd = [], [], [], []
    for ci in range(n):
        b = cum[ci]
        b_last = b[c - 1:c, :]
        v = v_ref[rs[ci], :]
        kd = (kf[ci] * jnp.exp2(b_last - b)).astype(jnp.bfloat16)
        qd.append((qf[ci] * jnp.exp2(b)).astype(jnp.bfloat16))
        decay.append(jnp.exp2(b_last))
        upd.append(_dot(v, kd, _TN))
        o_intra.append(_dot(attn[ci].astype(jnp.bfloat16), v))

    st = st_scr[...]
    for ci in range(n):
        o = (o_intra[ci] + _dot(qd[ci], st.astype(jnp.bfloat16), _NT)) * scale
        st = st * decay[ci] + upd[ci]
        on = o * _rms_scale(o) * gain
        t = 0.5 * g_ref[rs[ci], :].astype(jnp.float32)
        o_ref[rs[ci], :] = (on * (t + t * jnp.tanh(t))).astype(o_ref.dtype)
    st_scr[...] = st


def _gla_core(qkvg, z, w_gate, b_gate, head_gain, batch, seq):
    t = qkvg.shape[0]
    nblk = seq // GLA_ROWS
    lm, mk = _gla_constants()
    lm = jnp.asarray(np.concatenate([lm, lm], axis=1), jnp.bfloat16)
    mk = jnp.asarray(mk, jnp.float32)
    kq = (GLA_HEADS * GLA_DK) // GLA_DK
    kv = (2 * GLA_HEADS * GLA_DK) // GLA_DV
    kg = kv + GLA_HEADS
    row = lambda b, h, s: b * nblk + s
    return pl.pallas_call(
        _gla_core_kernel,
        grid=(batch, GLA_HEADS, nblk),
        in_specs=[
            pl.BlockSpec((GLA_ROWS, GLA_DK), lambda b, h, s: (row(b, h, s), h)),
            pl.BlockSpec((GLA_ROWS, GLA_DK), lambda b, h, s: (row(b, h, s), kq + h)),
            pl.BlockSpec((GLA_ROWS, GLA_DV), lambda b, h, s: (row(b, h, s), kv + h)),
            pl.BlockSpec((GLA_ROWS, GLA_DV), lambda b, h, s: (row(b, h, s), kg + h)),
            pl.BlockSpec((GLA_ROWS, LANES), lambda b, h, s: (row(b, h, s), 0)),
            pl.BlockSpec((LANES, GLA_DK), lambda b, h, s: (0, h)),
            pl.BlockSpec((1, GLA_DK), lambda b, h, s: (0, h)),
            pl.BlockSpec((1, GLA_DV), lambda b, h, s: (0, 0)),
            pl.BlockSpec(lm.shape, lambda b, h, s: (0, 0)),
            pl.BlockSpec(mk.shape, lambda b, h, s: (0, 0, 0)),
        ],
        out_specs=pl.BlockSpec((GLA_ROWS, GLA_DV), lambda b, h, s: (row(b, h, s), h)),
        out_shape=jax.ShapeDtypeStruct((t, GLA_HEADS * GLA_DV), jnp.bfloat16),
        scratch_shapes=[pltpu.VMEM((GLA_DV, GLA_DK), jnp.float32)],
        compiler_params=_params("parallel", "parallel", "arbitrary"),
        name="gla_core",
    )(qkvg, qkvg, qkvg, qkvg, z, w_gate, b_gate, head_gain, lm, mk)


def _outproj_mlp_kernel(a_ref, h_ref, wo_ref, g_ref, w1_ref, w2_ref, o_ref):
    bf = jnp.bfloat16
    h1 = h_ref[...] + _dot(a_ref[...], wo_ref[...])
    xn = (h1 * _rms_scale(h1) * g_ref[...]).astype(bf)
    acc = h1
    d = h1.shape[1]
    for c0 in range(0, w1_ref.shape[1], d):
        hid = jnp.maximum(_dot(xn, w1_ref[:, c0:c0 + d]), 0.0)
        acc = acc + _dot((hid * hid).astype(bf), w2_ref[c0:c0 + d, :])
    o_ref[...] = acc


def _outproj_mlp(a, h, w_out, gain, w1, w2):
    t, d = h.shape
    tile = pl.BlockSpec((ROW_TILE, d), lambda i: (i, 0))
    return pl.pallas_call(
        _outproj_mlp_kernel,
        grid=(t // ROW_TILE,),
        in_specs=[tile, tile, _resident(w_out.shape), _resident((1, d)),
                  _resident(w1.shape), _resident(w2.shape)],
        out_specs=tile,
        out_shape=jax.ShapeDtypeStruct((t, d), jnp.float32),
        compiler_params=_params("parallel"),
        name="outproj_mlp",
    )(a, h, w_out, gain, w1, w2)


def _segment_norm(y, seg_ones, gain):
    ss = _dot((y * y).astype(jnp.bfloat16), seg_ones)
    return y * lax.rsqrt(ss * (1.0 / DIFF_DH) + NORM_EPS) * gain


def _diff_proj_kernel(h_ref, gkv_ref, gq_ref, wkv_ref, wq_ref, kg_ref, qg_ref, so_ref,
                      *rest, n_jobs):
    k_ref, v_ref, q_ref = rest[n_jobs:n_jobs + 3]
    _round_weights(n_jobs, rest[:n_jobs] + rest[n_jobs + 3:])
    nk = k_ref.shape[1]
    seg_ones = so_ref[...]
    bf = jnp.bfloat16
    qscale = DIFF_DH ** -0.5 * math.log2(math.e)
    x = h_ref[...]
    xhat = x * _rms_scale(x)
    xkv = (xhat * gkv_ref[...]).astype(bf)
    xq = (xhat * gq_ref[...]).astype(bf)
    def norm_store(out_ref, gain_ref, scale, cols, raw):
        y = _segment_norm(raw, seg_ones, gain_ref[...])
        out_ref[:, cols] = (y if scale is None else y * scale).astype(out_ref.dtype)

    pending = None
    for x_in, w_ref, out_ref, gain_ref, scale in ((xkv, wkv_ref, k_ref, kg_ref, None),
                                                 (xq, wq_ref, q_ref, qg_ref, qscale)):
        for c0 in range(0, nk, SEG_TILE):
            cols = slice(c0, c0 + SEG_TILE)
            raw = _dot(x_in, w_ref[:, cols].astype(bf))
            if pending is not None:
                norm_store(*pending)
            pending = (out_ref, gain_ref, scale, cols, raw)
        if out_ref is k_ref:
            v_ref[...] = _dot(xkv, wkv_ref[:, nk:].astype(bf)).astype(v_ref.dtype)
    norm_store(*pending)


def _diff_proj(h, g_kv, g_q, w_kv, w_q, k_gain, q_gain, next_weights):
    t, d = h.shape
    n_steps = t // PROJ_ROW_TILE
    jobs, job_in, job_out, job_shapes = _rounding_job(next_weights, n_steps)
    nk = DIFF_HEADS * 2 * DIFF_DH
    nv = DIFF_HEADS * DIFF_VD
    lane = np.arange(SEG_TILE)
    seg_ones = jnp.asarray(lane[:, None] // DIFF_DH == lane[None, :] // DIFF_DH, jnp.bfloat16)
    tile = lambda n: pl.BlockSpec((PROJ_ROW_TILE, n), lambda i: (i, 0))
    return pl.pallas_call(
        functools.partial(_diff_proj_kernel, n_jobs=len(jobs)),
        grid=(n_steps,),
        in_specs=[tile(d), _resident((1, d)), _resident((1, d)), _resident(w_kv.shape),
                  _layer_resident(w_q, 0), _resident((1, SEG_TILE)), _resident((1, SEG_TILE)),
                  _resident((SEG_TILE, SEG_TILE))] + job_in,
        out_specs=[tile(nk), tile(nv), tile(nk)] + job_out,
        out_shape=[jax.ShapeDtypeStruct((t, nk), jnp.bfloat16),
                   jax.ShapeDtypeStruct((t, nv), jnp.bfloat16),
                   jax.ShapeDtypeStruct((t, nk), jnp.bfloat16)] + job_shapes,
        compiler_params=_params("parallel"),
        name="diff_proj",
    )(h, g_kv, g_q, w_kv, w_q, k_gain, q_gain, seg_ones, *jobs)


def _diff_attn_kernel(q_ref, k_ref, v_ref, lam_ref, hg_ref, o_ref,
                      m_scr, acc_scr, s_bufs, p_bufs, *, lambda_init):
    tq = ATT_TQ
    tk = ATT_TK
    rb = ATT_ROW_BLOCK
    lane = lax.broadcasted_iota(jnp.int32, (tq, q_ref.shape[1]), 1)
    col_minus_row = (lax.broadcasted_iota(jnp.int32, (rb, tk), 1)
                     - lax.broadcasted_iota(jnp.int32, (rb, tk), 0))

    ones_block = jnp.ones((tk, DIFF_VD), jnp.bfloat16)

    def block(j):
        return pl.ds(pl.multiple_of(j * tk, tk), tk)

    def q_tile(qi, carry):
        q_rows = pl.ds(pl.multiple_of(qi * tq, tq), tq)
        m_scr[...] = jnp.full_like(m_scr, NEG_BIG)
        acc_scr[...] = jnp.zeros_like(acc_scr)

        q = q_ref[q_rows, :]
        qcs = [jnp.where((lane // DIFF_DH) == c, q, jnp.zeros_like(q)) for c in range(2)]

        def step(j, buf, row_lo, diag_row0):
            k = k_ref[block(j), :]
            v = jnp.concatenate([v_ref[block(j), :], ones_block], axis=1)
            s_scr = s_bufs.at[buf]
            p_scr = p_bufs.at[buf]
            all_rows = slice(row_lo, tq)
            for c in range(2):
                s_scr[c, all_rows, :] = _dot(qcs[c][all_rows], k, _NT)
            for c in range(2):
                for r0 in range(row_lo, tq, rb):
                    rows = slice(r0, r0 + rb)
                    masked = diag_row0 is not None and r0 - diag_row0 < tk
                    kmax = min(tk, -(-(r0 - diag_row0 + rb) // LANES) * LANES) if masked else tk
                    s = s_scr[c, rows, 0:kmax]
                    if masked:
                        s = jnp.where(col_minus_row[:, 0:kmax] <= r0 - diag_row0, s, NEG_BIG)
                    m_prev = m_scr[c, rows, :]
                    m_new = jnp.maximum(m_prev, jnp.max(s, axis=-1, keepdims=True))
                    alpha = jnp.exp2(m_prev - m_new)
                    p = jnp.exp2(s - jnp.tile(m_new, (1, kmax // LANES)))
                    m_scr[c, rows, :] = m_new
                    acc_scr[c, rows, :] = jnp.tile(alpha, (1, 2)) * acc_scr[c, rows, :]
                    p_scr[c, rows, 0:kmax] = p.astype(jnp.bfloat16)
                    if kmax < tk:
                        p_scr[c, rows, kmax:tk] = jnp.zeros((rb, tk - kmax), jnp.bfloat16)
                acc_scr[c, all_rows, :] += _dot(p_scr[c, all_rows, :], v)

        n_sub = tq // tk

        def pair(i, carry):
            step(2 * i, 0, 0, None)
            step(2 * i + 1, 1, 0, None)
            return carry

        lax.fori_loop(0, (n_sub // 2) * qi, pair, 0)
        for t in range(n_sub):
            step(n_sub * qi + t, t % 2, t * tk, t * tk)

        lp = lam_ref[...]
        lam = (jnp.exp(jnp.sum(lp[0:1] * lp[1:2], axis=-1, keepdims=True))
               - jnp.exp(jnp.sum(lp[2:3] * lp[3:4], axis=-1, keepdims=True)) + lambda_init)
        o = (acc_scr[0, :, 0:DIFF_VD] / acc_scr[0, :, DIFF_VD:]
             - lam * (acc_scr[1, :, 0:DIFF_VD] / acc_scr[1, :, DIFF_VD:]))
        on = o * _rms_scale(o) * hg_ref[...]
        o_ref[q_rows, :] = (on * (1.0 - lambda_init)).astype(o_ref.dtype)
        return carry

    lax.fori_loop(0, q_ref.shape[0] // tq, q_tile, 0)


def _diff_attn(q, k, v, lam_params, head_gain, batch, seq, lambda_init):
    t = q.shape[0]
    assert ATT_TQ % (2 * ATT_TK) == 0
    assert seq % ATT_TQ == 0
    kern = functools.partial(_diff_attn_kernel, lambda_init=lambda_init)
    head_block = lambda width: pl.BlockSpec((seq, width), lambda b, h: (b, h))
    return pl.pallas_call(
        kern,
        grid=(batch, DIFF_HEADS),
        in_specs=[
            head_block(2 * DIFF_DH),
            head_block(2 * DIFF_DH),
            head_block(DIFF_VD),
            pl.BlockSpec(lam_params.shape, lambda b, h: (0, 0)),
            pl.BlockSpec((1, DIFF_VD), lambda b, h: (0, 0)),
        ],
        out_specs=head_block(DIFF_VD),
        out_shape=jax.ShapeDtypeStruct((t, DIFF_HEADS * DIFF_VD), jnp.bfloat16),
        scratch_shapes=[
            pltpu.VMEM((2, ATT_TQ, LANES), jnp.float32),
            pltpu.VMEM((2, ATT_TQ, 2 * DIFF_VD), jnp.float32),
            pltpu.VMEM((2, 2, ATT_TQ, ATT_TK), jnp.float32),
            pltpu.VMEM((2, 2, ATT_TQ, ATT_TK), jnp.bfloat16),
        ],
        compiler_params=_params("parallel", "parallel"),
        name="diff_attn",
    )(q, k, v, lam_params, head_gain)


def kernel(x, a_norm, a_w_in, a_w_gate_up, a_b_gate, a_head_norm, a_w_out, kv_norm, w_kv, k_norm,
           b_norm, b_w_q, b_q_norm, b_lambda, b_head_norm, b_w_out, mlp_norm, mlp_w1, mlp_w2):
    batch, seq, d = x.shape
    bf = jnp.bfloat16
    row = lambda p: p.reshape(1, -1)
    h = x.reshape(batch * seq, d)

    n_main = 2 * GLA_HEADS * GLA_DK + 2 * GLA_HEADS * GLA_DV
    w_z = jnp.pad(a_w_in[0, :, n_main:], ((0, 0), (0, LANES - GLA_RANK))).astype(bf)
    w_gate = jnp.pad(a_w_gate_up[0], ((0, LANES - GLA_RANK), (0, 0))).astype(bf)
    qkvg, z, wo_bf, w1_bf, w2_bf = _gla_inproj(
        h, row(a_norm[0]), jnp.swapaxes(a_w_in, 1, 2), w_z, n_main,
        next_weights=((a_w_out, 0), (mlp_w1, 0), (mlp_w2, 0)))
    og = _gla_core(qkvg, z, w_gate, row(a_b_gate[0]), row(a_head_norm[0]), batch, seq)
    h = _outproj_mlp(og, h, wo_bf, row(mlp_norm[0]), w1_bf, w2_bf)

    layer = 1
    lambda_init = 0.8 - 0.6 * math.exp(-0.3 * layer)
    reps = SEG_TILE // DIFF_DH
    kn, v, qn, wo_bf, w1_bf, w2_bf = _diff_proj(
        h, row(kv_norm), row(b_norm[0]), w_kv, b_w_q,
        row(jnp.tile(k_norm, reps)), row(jnp.tile(b_q_norm[0], reps)),
        next_weights=((b_w_out, 0), (mlp_w1, 1), (mlp_w2, 1)))
    oa = _diff_attn(qn, kn, v, b_lambda[0], row(b_head_norm[0]), batch, seq, lambda_init)
    h = _outproj_mlp(oa, h, wo_bf, row(mlp_norm[1]), w1_bf, w2_bf)
    return h.reshape(batch, seq, d)
```

```python
import functools
import math

import numpy as np
import jax
import jax.numpy as jnp
from jax import lax
from jax.experimental import pallas as pl
from jax.experimental.pallas import tpu as pltpu

D_MODEL = 1024
GLA_HEADS = 4
GLA_DK = 128
GLA_DV = 256
GLA_RANK = 16
GLA_TAU = 16.0
GLA_CHUNK = 64
DIFF_HEADS = 8
DIFF_DH = 64
DIFF_VD = 128
MLP_HIDDEN = 4 * D_MODEL
NORM_EPS = 1e-6

LANES = 128
VMEM_LIMIT = 56 * 1024 * 1024
NEG_BIG = -1e30

ROW_TILE = 1024
PROJ_ROW_TILE = 1024
PROJ_COL_CHUNK = 768
GLA_ROWS = 2048
SEG_TILE = 256
ATT_TQ = 1024
ATT_TK = 512
ATT_ROW_BLOCK = 32
GLA_LEVELS = (32, 16, 8, 4, 2, 1)
SUBLANES = 8
_GLA_N_VPU_LEVELS = sum(m >= SUBLANES for m in GLA_LEVELS)

_NT = (((1,), (1,)), ((), ()))
_TN = (((0,), (0,)), ((), ()))


def _dot(a, b, dims=None):
    if dims is None:
        return jnp.dot(a, b, preferred_element_type=jnp.float32)
    return lax.dot_general(a, b, dims, preferred_element_type=jnp.float32)


def _params(*sem):
    return pltpu.CompilerParams(dimension_semantics=sem, vmem_limit_bytes=VMEM_LIMIT)


def _resident(shape):
    nd = len(shape)
    return pl.BlockSpec(shape, lambda *_: (0,) * nd, pipeline_mode=pl.Buffered(1))


def _layer_resident(w, layer):
    return pl.BlockSpec((None,) + w.shape[1:], lambda *_: (layer, 0, 0),
                        pipeline_mode=pl.Buffered(1))


def _rounding_job(stacked_weights, n_steps):
    inputs, in_specs, out_specs, out_shapes = [], [], [], []
    for w, layer in stacked_weights:
        rows, cols = w.shape[1] // n_steps, w.shape[2]
        inputs.append(w)
        in_specs.append(pl.BlockSpec((None, rows, cols), lambda i, layer=layer: (layer, i, 0)))
        out_specs.append(pl.BlockSpec((rows, cols), lambda i: (i, 0)))
        out_shapes.append(jax.ShapeDtypeStruct(w.shape[1:], jnp.bfloat16))
    return inputs, in_specs, out_specs, out_shapes


def _round_weights(n_jobs, refs):
    for src, dst in zip(refs[:n_jobs], refs[n_jobs:]):
        dst[...] = src[...].astype(dst.dtype)


def _rms_scale(x):
    return lax.rsqrt(jnp.mean(x * x, axis=-1, keepdims=True) + NORM_EPS)


def _gla_inproj_kernel(x_ref, g_ref, w_ref, wz_ref, *rest, n_jobs):
    o_ref, z_ref = rest[n_jobs:n_jobs + 2]
    _round_weights(n_jobs, rest[:n_jobs] + rest[n_jobs + 2:])
    x = x_ref[...]
    xg = (x * g_ref[...]).astype(jnp.bfloat16)
    r = _rms_scale(x)
    n_out = o_ref.shape[1]
    step = PROJ_COL_CHUNK
    for n0 in range(0, n_out, step):
        w = w_ref[n0:n0 + step, :].astype(jnp.bfloat16)
        o_ref[:, n0:n0 + step] = (_dot(xg, w, _NT) * r).astype(o_ref.dtype)
    z_ref[...] = _dot(xg, wz_ref[...]) * r


def _gla_inproj(x, gain, w_in, w_z, n, next_weights):
    t, d = x.shape
    n_steps = t // PROJ_ROW_TILE
    jobs, job_in, job_out, job_shapes = _rounding_job(next_weights, n_steps)
    return pl.pallas_call(
        functools.partial(_gla_inproj_kernel, n_jobs=len(jobs)),
        grid=(n_steps,),
        in_specs=[
            pl.BlockSpec((PROJ_ROW_TILE, d), lambda i: (i, 0)),
            _resident((1, d)),
            _layer_resident(w_in, 0),
            _resident((d, LANES)),
        ] + job_in,
        out_specs=[
            pl.BlockSpec((PROJ_ROW_TILE, n), lambda i: (i, 0)),
            pl.BlockSpec((PROJ_ROW_TILE, LANES), lambda i: (i, 0)),
        ] + job_out,
        out_shape=[
            jax.ShapeDtypeStruct((t, n), jnp.bfloat16),
            jax.ShapeDtypeStruct((t, LANES), jnp.float32),
        ] + job_shapes,
        compiler_params=_params("parallel"),
        name="gla_inproj",
    )(x, gain, w_in, w_z, *jobs)


def _gla_constants():
    c = GLA_CHUNK
    t = np.arange(c)
    row, col = t[:, None], t[None, :]
    mats = [col <= row]
    masks = []
    for m in GLA_LEVELS:
        blk = 2 * m
        start = (t // blk) * blk
        mid = (start + m)[:, None]
        second = ((t % blk) >= m)[:, None]
        q_side = second & (col > mid) & (col <= row)
        k_side = (~second) & (col > row) & (col <= mid)
        if m < SUBLANES:
            mats.append(q_side | k_side)
        masks.append((start[:, None] == start[None, :]) & second & (~second).T)
    masks.append(np.eye(c, dtype=bool))
    lm = np.concatenate(mats, axis=0).astype(np.float32)
    mk = np.stack(masks, axis=0).astype(np.float32)
    return lm, mk


def _gla_core_kernel(q_ref, k_ref, v_ref, g_ref, z_ref, wg_ref, bg_ref, hg_ref,
                     lm_ref, mk_ref, o_ref, st_scr):
    c = GLA_CHUNK
    rows = q_ref.shape[0]

    @pl.when(pl.program_id(2) == 0)
    def _():
        st_scr[...] = jnp.zeros_like(st_scr)

    logits = _dot(z_ref[...].astype(jnp.bfloat16), wg_ref[...]) + bg_ref[...]
    log_sig = jnp.minimum(logits, 0.0) - jnp.log(1.0 + jnp.exp(-jnp.abs(logits)))
    la = log_sig * (math.log2(math.e) / GLA_TAU)
    la_hi = la.astype(jnp.bfloat16)
    la_lo = (la - la_hi.astype(jnp.float32)).astype(jnp.bfloat16)

    scale = GLA_DK ** -0.5
    gain = hg_ref[...]
    n = rows // c
    rs = [slice(ci * c, (ci + 1) * c) for ci in range(n)]
    n_lvl = len(GLA_LEVELS)

    sums = []
    for ci in range(n):
        la_stack = jnp.concatenate([la_hi[rs[ci]], la_lo[rs[ci]]], axis=0)
        sums.append(_dot(lm_ref[...], la_stack))
    cum = [s[0:c] for s in sums]

    def level_exponent(ci, li):
        m = GLA_LEVELS[li]
        if m < SUBLANES:
            lo = (1 + li - _GLA_N_VPU_LEVELS) * c
            return sums[ci][lo:lo + c]
        b = cum[ci]
        parts = []
        for start in range(0, c, 2 * m):
            mid = b[start + m:start + m + 1, :]
            parts.append(mid - b[start:start + m])
            parts.append(b[start + m:start + 2 * m] - mid)
        return jnp.concatenate(parts, axis=0)

    qf = [q_ref[r, :].astype(jnp.float32) for r in rs]
    kf = [k_ref[r, :].astype(jnp.float32) for r in rs]
    attn = [mk_ref[n_lvl] * _dot(q_ref[r, :], k_ref[r, :], _NT) for r in rs]
    row_in_chunk = lax.broadcasted_iota(jnp.int32, (c, GLA_DK), 0)

    def level_operand(ci, li):
        m = GLA_LEVELS[li]
        if m < SUBLANES:
            return jnp.where(row_in_chunk % (2 * m) >= m, qf[ci], kf[ci])
        parts = []
        for start in range(0, c, 2 * m):
            parts.append(kf[ci][start:start + m])
            parts.append(qf[ci][start + m:start + 2 * m])
        return jnp.concatenate(parts, axis=0)

    for li in range(n_lvl):
        for ci in range(n):
            e = jnp.exp2(level_exponent(ci, li))
            x = (level_operand(ci, li) * e).astype(jnp.bfloat16)
            attn[ci] = attn[ci] + mk_ref[li] * _dot(x, x, _NT)

    o_intra, qd, decay, upd = [], [], [], []
    for ci in range(n):
        b = cum[ci]
        b_last = b[c - 1:c, :]
        v = v_ref[rs[ci], :]
        kd = (kf[ci] * jnp.exp2(b_last - b)).astype(jnp.bfloat16)
        qd.append((qf[ci] * jnp.exp2(b)).astype(jnp.bfloat16))
        decay.append(jnp.exp2(b_last))
        upd.append(_dot(v, kd, _TN))
        o_intra.append(_dot(attn[ci].astype(jnp.bfloat16), v))

    st = st_scr[...]
    for ci in range(n):
        o = (o_intra[ci] + _dot(qd[ci], st.astype(jnp.bfloat16), _NT)) * scale
        st = st * decay[ci] + upd[ci]
        on = o * _rms_scale(o) * gain
        t = 0.5 * g_ref[rs[ci], :].astype(jnp.float32)
        o_ref[rs[ci], :] = (on * (t + t * jnp.tanh(t))).astype(o_ref.dtype)
    st_scr[...] = st


def _gla_core(qkvg, z, w_gate, b_gate, head_gain, batch, seq):
    t = qkvg.shape[0]
    nblk = seq // GLA_ROWS
    lm, mk = _gla_constants()
    lm = jnp.asarray(np.concatenate([lm, lm], axis=1), jnp.bfloat16)
    mk = jnp.asarray(mk, jnp.float32)
    kq = (GLA_HEADS * GLA_DK) // GLA_DK
    kv = (2 * GLA_HEADS * GLA_DK) // GLA_DV
    kg = kv + GLA_HEADS
    row = lambda b, h, s: b * nblk + s
    return pl.pallas_call(
        _gla_core_kernel,
        grid=(batch, GLA_HEADS, nblk),
        in_specs=[
            pl.BlockSpec((GLA_ROWS, GLA_DK), lambda b, h, s: (row(b, h, s), h)),
            pl.BlockSpec((GLA_ROWS, GLA_DK), lambda b, h, s: (row(b, h, s), kq + h)),
            pl.BlockSpec((GLA_ROWS, GLA_DV), lambda b, h, s: (row(b, h, s), kv + h)),
            pl.BlockSpec((GLA_ROWS, GLA_DV), lambda b, h, s: (row(b, h, s), kg + h)),
            pl.BlockSpec((GLA_ROWS, LANES), lambda b, h, s: (row(b, h, s), 0)),
            pl.BlockSpec((LANES, GLA_DK), lambda b, h, s: (0, h)),
            pl.BlockSpec((1, GLA_DK), lambda b, h, s: (0, h)),
            pl.BlockSpec((1, GLA_DV), lambda b, h, s: (0, 0)),
            pl.BlockSpec(lm.shape, lambda b, h, s: (0, 0)),
            pl.BlockSpec(mk.shape, lambda b, h, s: (0, 0, 0)),
        ],
        out_specs=pl.BlockSpec((GLA_ROWS, GLA_DV), lambda b, h, s: (row(b, h, s), h)),
        out_shape=jax.ShapeDtypeStruct((t, GLA_HEADS * GLA_DV), jnp.bfloat16),
        scratch_shapes=[pltpu.VMEM((GLA_DV, GLA_DK), jnp.float32)],
        compiler_params=_params("parallel", "parallel", "arbitrary"),
        name="gla_core",
    )(qkvg, qkvg, qkvg, qkvg, z, w_gate, b_gate, head_gain, lm, mk)


def _outproj_mlp_kernel(a_ref, h_ref, wo_ref, g_ref, w1_ref, w2_ref, o_ref):
    bf = jnp.bfloat16
    h1 = h_ref[...] + _dot(a_ref[...], wo_ref[...])
    xg = (h1 * g_ref[...]).astype(bf)
    ms = jnp.mean(h1 * h1, axis=-1, keepdims=True)
    d = h1.shape[1]
    acc = None
    for c0 in range(0, w1_ref.shape[1], d):
        hid = jnp.maximum(_dot(xg, w1_ref[:, c0:c0 + d]), 0.0)
        part = _dot((hid * hid).astype(bf), w2_ref[c0:c0 + d, :])
        acc = part if acc is None else acc + part
    o_ref[...] = h1 + acc * (1.0 / (ms + NORM_EPS))


def _outproj_mlp(a, h, w_out, gain, w1, w2):
    t, d = h.shape
    tile = pl.BlockSpec((ROW_TILE, d), lambda i: (i, 0))
    return pl.pallas_call(
        _outproj_mlp_kernel,
        grid=(t // ROW_TILE,),
        in_specs=[tile, tile, _resident(w_out.shape), _resident((1, d)),
                  _resident(w1.shape), _resident(w2.shape)],
        out_specs=tile,
        out_shape=jax.ShapeDtypeStruct((t, d), jnp.float32),
        compiler_params=_params("parallel"),
        name="outproj_mlp",
    )(a, h, w_out, gain, w1, w2)


def _segment_norm(y, seg_ones, gain):
    ss = _dot((y * y).astype(jnp.bfloat16), seg_ones)
    return y * lax.rsqrt(ss * (1.0 / DIFF_DH) + NORM_EPS) * gain


def _diff_proj_kernel(h_ref, gkv_ref, gq_ref, wkv_ref, wq_ref, kg_ref, qg_ref, so_ref,
                      *rest, n_jobs):
    k_ref, v_ref, q_ref = rest[n_jobs:n_jobs + 3]
    _round_weights(n_jobs, rest[:n_jobs] + rest[n_jobs + 3:])
    nk = k_ref.shape[1]
    seg_ones = so_ref[...]
    bf = jnp.bfloat16
    qscale = DIFF_DH ** -0.5 * math.log2(math.e)
    x = h_ref[...]
    xkv = (x * gkv_ref[...]).astype(bf)
    xq = (x * gq_ref[...]).astype(bf)
    r = _rms_scale(x)
    def norm_store(out_ref, gain_ref, scale, cols, raw):
        y = _segment_norm(raw * r, seg_ones, gain_ref[...])
        out_ref[:, cols] = (y if scale is None else y * scale).astype(out_ref.dtype)

    pending = None
    for x_in, w_ref, out_ref, gain_ref, scale in ((xkv, wkv_ref, k_ref, kg_ref, None),
                                                 (xq, wq_ref, q_ref, qg_ref, qscale)):
        for c0 in range(0, nk, SEG_TILE):
            cols = slice(c0, c0 + SEG_TILE)
            raw = _dot(x_in, w_ref[:, cols].astype(bf))
            if pending is not None:
                norm_store(*pending)
            pending = (out_ref, gain_ref, scale, cols, raw)
        if out_ref is k_ref:
            v_ref[...] = (_dot(xkv, wkv_ref[:, nk:].astype(bf)) * r).astype(v_ref.dtype)
    norm_store(*pending)


def _diff_proj(h, g_kv, g_q, w_kv, w_q, k_gain, q_gain, next_weights):
    t, d = h.shape
    n_steps = t // PROJ_ROW_TILE
    jobs, job_in, job_out, job_shapes = _rounding_job(next_weights, n_steps)
    nk = DIFF_HEADS * 2 * DIFF_DH
    nv = DIFF_HEADS * DIFF_VD
    lane = np.arange(SEG_TILE)
    seg_ones = jnp.asarray(lane[:, None] // DIFF_DH == lane[None, :] // DIFF_DH, jnp.bfloat16)
    tile = lambda n: pl.BlockSpec((PROJ_ROW_TILE, n), lambda i: (i, 0))
    return pl.pallas_call(
        functools.partial(_diff_proj_kernel, n_jobs=len(jobs)),
        grid=(n_steps,),
        in_specs=[tile(d), _resident((1, d)), _resident((1, d)), _resident(w_kv.shape),
                  _layer_resident(w_q, 0), _resident((1, SEG_TILE)), _resident((1, SEG_TILE)),
                  _resident((SEG_TILE, SEG_TILE))] + job_in,
        out_specs=[tile(nk), tile(nv), tile(nk)] + job_out,
        out_shape=[jax.ShapeDtypeStruct((t, nk), jnp.bfloat16),
                   jax.ShapeDtypeStruct((t, nv), jnp.bfloat16),
                   jax.ShapeDtypeStruct((t, nk), jnp.bfloat16)] + job_shapes,
        compiler_params=_params("parallel"),
        name="diff_proj",
    )(h, g_kv, g_q, w_kv, w_q, k_gain, q_gain, seg_ones, *jobs)


def _diff_attn_kernel(q_ref, k_ref, v_ref, lam_ref, hg_ref, o_ref,
                      m_scr, l_scr, acc_scr, s_bufs, p_bufs, *, lambda_init):
    tq = ATT_TQ
    tk = ATT_TK
    rb = ATT_ROW_BLOCK
    lane = lax.broadcasted_iota(jnp.int32, (tq, q_ref.shape[1]), 1)
    col_minus_row = (lax.broadcasted_iota(jnp.int32, (rb, tk), 1)
                     - lax.broadcasted_iota(jnp.int32, (rb, tk), 0))

    def block(j):
        return pl.ds(pl.multiple_of(j * tk, tk), tk)

    def q_tile(qi, carry):
        q_rows = pl.ds(pl.multiple_of(qi * tq, tq), tq)
        m_scr[...] = jnp.full_like(m_scr, NEG_BIG)
        l_scr[...] = jnp.zeros_like(l_scr)
        acc_scr[...] = jnp.zeros_like(acc_scr)

        q = q_ref[q_rows, :]
        qcs = [jnp.where((lane // DIFF_DH) == c, q, jnp.zeros_like(q)) for c in range(2)]

        def step(j, buf, row_lo, diag_row0):
            k = k_ref[block(j), :]
            v = v_ref[block(j), :]
            s_scr = s_bufs.at[buf]
            p_scr = p_bufs.at[buf]
            all_rows = slice(row_lo, tq)
            for c in range(2):
                s_scr[c, all_rows, :] = _dot(qcs[c][all_rows], k, _NT)
            for c in range(2):
                for r0 in range(row_lo, tq, rb):
                    rows = slice(r0, r0 + rb)
                    masked = diag_row0 is not None and r0 - diag_row0 < tk
                    kmax = min(tk, -(-(r0 - diag_row0 + rb) // LANES) * LANES) if masked else tk
                    s = s_scr[c, rows, 0:kmax]
                    if masked:
                        s = jnp.where(col_minus_row[:, 0:kmax] <= r0 - diag_row0, s, NEG_BIG)
                    m_prev = m_scr[c, rows, :]
                    m_new = jnp.maximum(m_prev, jnp.max(s, axis=-1, keepdims=True))
                    alpha = jnp.exp2(m_prev - m_new)
                    p = jnp.exp2(s - jnp.tile(m_new, (1, kmax // LANES)))
                    psum = p[:, 0:LANES]
                    for t0 in range(LANES, kmax, LANES):
                        psum = psum + p[:, t0:t0 + LANES]
                    l_scr[c, rows, :] = alpha * l_scr[c, rows, :] + psum
                    m_scr[c, rows, :] = m_new
                    acc_scr[c, rows, :] = alpha * acc_scr[c, rows, :]
                    p_scr[c, rows, 0:kmax] = p.astype(jnp.bfloat16)
                    if kmax < tk:
                        p_scr[c, rows, kmax:tk] = jnp.zeros((rb, tk - kmax), jnp.bfloat16)
                acc_scr[c, all_rows, :] += _dot(p_scr[c, all_rows, :], v)

        n_sub = tq // tk

        def pair(i, carry):
            step(2 * i, 0, 0, None)
            step(2 * i + 1, 1, 0, None)
            return carry

        lax.fori_loop(0, (n_sub // 2) * qi, pair, 0)
        for t in range(n_sub):
            step(n_sub * qi + t, t % 2, t * tk, t * tk)

        lp = lam_ref[...]
        lam = (jnp.exp(jnp.sum(lp[0:1] * lp[1:2], axis=-1, keepdims=True))
               - jnp.exp(jnp.sum(lp[2:3] * lp[3:4], axis=-1, keepdims=True)) + lambda_init)
        l0 = jnp.sum(l_scr[0], axis=-1, keepdims=True)
        l1 = jnp.sum(l_scr[1], axis=-1, keepdims=True)
        o = acc_scr[0] / l0 - lam * (acc_scr[1] / l1)
        on = o * _rms_scale(o) * hg_ref[...]
        o_ref[q_rows, :] = (on * (1.0 - lambda_init)).astype(o_ref.dtype)
        return carry

    lax.fori_loop(0, q_ref.shape[0] // tq, q_tile, 0)


def _diff_attn(q, k, v, lam_params, head_gain, batch, seq, lambda_init):
    t = q.shape[0]
    assert ATT_TQ % (2 * ATT_TK) == 0
    assert seq % ATT_TQ == 0
    kern = functools.partial(_diff_attn_kernel, lambda_init=lambda_init)
    head_block = lambda width: pl.BlockSpec((seq, width), lambda b, h: (b, h))
    return pl.pallas_call(
        kern,
        grid=(batch, DIFF_HEADS),
        in_specs=[
            head_block(2 * DIFF_DH),
            head_block(2 * DIFF_DH),
            head_block(DIFF_VD),
            pl.BlockSpec(lam_params.shape, lambda b, h: (0, 0)),
            pl.BlockSpec((1, DIFF_VD), lambda b, h: (0, 0)),
        ],
        out_specs=head_block(DIFF_VD),
        out_shape=jax.ShapeDtypeStruct((t, DIFF_HEADS * DIFF_VD), jnp.bfloat16),
        scratch_shapes=[
            pltpu.VMEM((2, ATT_TQ, LANES), jnp.float32),
            pltpu.VMEM((2, ATT_TQ, LANES), jnp.float32),
            pltpu.VMEM((2, ATT_TQ, DIFF_VD), jnp.float32),
            pltpu.VMEM((2, 2, ATT_TQ, ATT_TK), jnp.float32),
            pltpu.VMEM((2, 2, ATT_TQ, ATT_TK), jnp.bfloat16),
        ],
        compiler_params=_params("parallel", "parallel"),
        name="diff_attn",
    )(q, k, v, lam_params, head_gain)


def kernel(x, a_norm, a_w_in, a_w_gate_up, a_b_gate, a_head_norm, a_w_out, kv_norm, w_kv, k_norm,
           b_norm, b_w_q, b_q_norm, b_lambda, b_head_norm, b_w_out, mlp_norm, mlp_w1, mlp_w2):
    batch, seq, d = x.shape
    bf = jnp.bfloat16
    row = lambda p: p.reshape(1, -1)
    h = x.reshape(batch * seq, d)

    n_main = 2 * GLA_HEADS * GLA_DK + 2 * GLA_HEADS * GLA_DV
    w_z = jnp.pad(a_w_in[0, :, n_main:], ((0, 0), (0, LANES - GLA_RANK))).astype(bf)
    w_gate = jnp.pad(a_w_gate_up[0], ((0, LANES - GLA_RANK), (0, 0))).astype(bf)
    qkvg, z, wo_bf, w1_bf, w2_bf = _gla_inproj(
        h, row(a_norm[0]), jnp.swapaxes(a_w_in, 1, 2), w_z, n_main,
        next_weights=((a_w_out, 0), (mlp_w1, 0), (mlp_w2, 0)))
    og = _gla_core(qkvg, z, w_gate, row(a_b_gate[0]), row(a_head_norm[0]), batch, seq)
    h = _outproj_mlp(og, h, wo_bf, row(mlp_norm[0]), w1_bf, w2_bf)

    layer = 1
    lambda_init = 0.8 - 0.6 * math.exp(-0.3 * layer)
    reps = SEG_TILE // DIFF_DH
    kn, v, qn, wo_bf, w1_bf, w2_bf = _diff_proj(
        h, row(kv_norm), row(b_norm[0]), w_kv, b_w_q,
        row(jnp.tile(k_norm, reps)), row(jnp.tile(b_q_norm[0], reps)),
        next_weights=((b_w_out, 0), (mlp_w1, 1), (mlp_w2, 1)))
    oa = _diff_attn(qn, kn, v, b_lambda[0], row(b_head_norm[0]), batch, seq, lambda_init)
    h = _outproj_mlp(oa, h, wo_bf, row(mlp_norm[1]), w1_bf, w2_bf)
    return h.reshape(batch, seq, d)
```

```python
import functools
import math

import numpy as np
import jax
import jax.numpy as jnp
from jax import lax
from jax.experimental import pallas as pl
from jax.experimental.pallas import tpu as pltpu

D_MODEL = 1024
GLA_HEADS = 4
GLA_DK = 128
GLA_DV = 256
GLA_RANK = 16
GLA_TAU = 16.0
GLA_CHUNK = 64
DIFF_HEADS = 8
DIFF_DH = 64
DIFF_VD = 128
MLP_HIDDEN = 4 * D_MODEL
NORM_EPS = 1e-6

LANES = 128
VMEM_LIMIT = 56 * 1024 * 1024
NEG_BIG = -1e30

ROW_TILE = 1024
PROJ_ROW_TILE = 1024
PROJ_COL_CHUNK = 768
GLA_ROWS = 2048
SEG_TILE = 256
ATT_TQ = 1024
ATT_TK = 512
ATT_MXU_TILE = 256
ATT_ROW_BLOCK = 32
GLA_LEVELS = (32, 16, 8, 4, 2, 1)
SUBLANES = 8
_GLA_N_VPU_LEVELS = sum(m >= SUBLANES for m in GLA_LEVELS)

_NT = (((1,), (1,)), ((), ()))
_TN = (((0,), (0,)), ((), ()))


def _dot(a, b, dims=None):
    if dims is None:
        return jnp.dot(a, b, preferred_element_type=jnp.float32)
    return lax.dot_general(a, b, dims, preferred_element_type=jnp.float32)


def _params(*sem):
    return pltpu.CompilerParams(dimension_semantics=sem, vmem_limit_bytes=VMEM_LIMIT)


def _resident(shape):
    nd = len(shape)
    return pl.BlockSpec(shape, lambda *_: (0,) * nd, pipeline_mode=pl.Buffered(1))


def _layer_resident(w, layer):
    return pl.BlockSpec((None,) + w.shape[1:], lambda *_: (layer, 0, 0),
                        pipeline_mode=pl.Buffered(1))


def _rounding_job(stacked_weights, n_steps):
    inputs, in_specs, out_specs, out_shapes = [], [], [], []
    for w, layer in stacked_weights:
        rows, cols = w.shape[1] // n_steps, w.shape[2]
        inputs.append(w)
        in_specs.append(pl.BlockSpec((None, rows, cols), lambda i, layer=layer: (layer, i, 0)))
        out_specs.append(pl.BlockSpec((rows, cols), lambda i: (i, 0)))
        out_shapes.append(jax.ShapeDtypeStruct(w.shape[1:], jnp.bfloat16))
    return inputs, in_specs, out_specs, out_shapes


def _round_weights(n_jobs, refs):
    for src, dst in zip(refs[:n_jobs], refs[n_jobs:]):
        dst[...] = src[...].astype(dst.dtype)


def _rms_scale(x):
    return lax.rsqrt(jnp.mean(x * x, axis=-1, keepdims=True) + NORM_EPS)


def _gla_inproj_kernel(x_ref, g_ref, w_ref, wz_ref, *rest, n_jobs):
    o_ref, z_ref = rest[n_jobs:n_jobs + 2]
    _round_weights(n_jobs, rest[:n_jobs] + rest[n_jobs + 2:])
    x = x_ref[...]
    xn = (x * _rms_scale(x) * g_ref[...]).astype(jnp.bfloat16)
    n_out = o_ref.shape[1]
    step = PROJ_COL_CHUNK
    for n0 in range(0, n_out, step):
        w = w_ref[n0:n0 + step, :].astype(jnp.bfloat16)
        o_ref[:, n0:n0 + step] = _dot(xn, w, _NT).astype(o_ref.dtype)
    z_ref[...] = _dot(xn, wz_ref[...])


def _gla_inproj(x, gain, w_in, w_z, n, next_weights):
    t, d = x.shape
    n_steps = t // PROJ_ROW_TILE
    jobs, job_in, job_out, job_shapes = _rounding_job(next_weights, n_steps)
    return pl.pallas_call(
        functools.partial(_gla_inproj_kernel, n_jobs=len(jobs)),
        grid=(n_steps,),
        in_specs=[
            pl.BlockSpec((PROJ_ROW_TILE, d), lambda i: (i, 0)),
            _resident((1, d)),
            _layer_resident(w_in, 0),
            _resident((d, LANES)),
        ] + job_in,
        out_specs=[
            pl.BlockSpec((PROJ_ROW_TILE, n), lambda i: (i, 0)),
            pl.BlockSpec((PROJ_ROW_TILE, LANES), lambda i: (i, 0)),
        ] + job_out,
        out_shape=[
            jax.ShapeDtypeStruct((t, n), jnp.bfloat16),
            jax.ShapeDtypeStruct((t, LANES), jnp.float32),
        ] + job_shapes,
        compiler_params=_params("parallel"),
        name="gla_inproj",
    )(x, gain, w_in, w_z, *jobs)


def _gla_constants():
    c = GLA_CHUNK
    t = np.arange(c)
    row, col = t[:, None], t[None, :]
    mats = [col <= row]
    masks = []
    for m in GLA_LEVELS:
        blk = 2 * m
        start = (t // blk) * blk
        mid = (start + m)[:, None]
        second = ((t % blk) >= m)[:, None]
        q_side = second & (col > mid) & (col <= row)
        k_side = (~second) & (col > row) & (col <= mid)
        if m < SUBLANES:
            mats.append(q_side | k_side)
        masks.append((start[:, None] == start[None, :]) & second & (~second).T)
    masks.append(np.eye(c, dtype=bool))
    lm = np.concatenate(mats, axis=0).astype(np.float32)
    mk = np.stack(masks, axis=0).astype(np.float32)
    return lm, mk


def _gla_core_kernel(q_ref, k_ref, v_ref, g_ref, z_ref, wg_ref, bg_ref, hg_ref,
                     lm_ref, mk_ref, o_ref, st_scr):
    c = GLA_CHUNK
    rows = q_ref.shape[0]

    @pl.when(pl.program_id(2) == 0)
    def _():
        st_scr[...] = jnp.zeros_like(st_scr)

    logits = _dot(z_ref[...].astype(jnp.bfloat16), wg_ref[...]) + bg_ref[...]
    log_sig = jnp.minimum(logits, 0.0) - jnp.log(1.0 + jnp.exp(-jnp.abs(logits)))
    la = log_sig * (math.log2(math.e) / GLA_TAU)
    la_hi = la.astype(jnp.bfloat16)
    la_lo = (la - la_hi.astype(jnp.float32)).astype(jnp.bfloat16)

    scale = GLA_DK ** -0.5
    gain = hg_ref[...]
    n = rows // c
    rs = [slice(ci * c, (ci + 1) * c) for ci in range(n)]
    n_lvl = len(GLA_LEVELS)

    sums = []
    for ci in range(n):
        la_stack = jnp.concatenate([la_hi[rs[ci]], la_lo[rs[ci]]], axis=0)
        sums.append(_dot(lm_ref[...], la_stack))
    cum = [s[0:c] for s in sums]

    def level_exponent(ci, li):
        m = GLA_LEVELS[li]
        if m < SUBLANES:
            lo = (1 + li - _GLA_N_VPU_LEVELS) * c
            return sums[ci][lo:lo + c]
        b = cum[ci]
        parts = []
        for start in range(0, c, 2 * m):
            mid = b[start + m:start + m + 1, :]
            parts.append(mid - b[start:start + m])
            parts.append(b[start + m:start + 2 * m] - mid)
        return jnp.concatenate(parts, axis=0)

    qf = [q_ref[r, :].astype(jnp.float32) for r in rs]
    kf = [k_ref[r, :].astype(jnp.float32) for r in rs]
    attn = [mk_ref[n_lvl] * _dot(q_ref[r, :], k_ref[r, :], _NT) for r in rs]
    row_in_chunk = lax.broadcasted_iota(jnp.int32, (c, GLA_DK), 0)

    def level_operand(ci, li):
        m = GLA_LEVELS[li]
        if m < SUBLANES:
            return jnp.where(row_in_chunk % (2 * m) >= m, qf[ci], kf[ci])
        parts = []
        for start in range(0, c, 2 * m):
            parts.append(kf[ci][start:start + m])
            parts.append(qf[ci][start + m:start + 2 * m])
        return jnp.concatenate(parts, axis=0)

    for li in range(n_lvl):
        for ci in range(n):
            e = jnp.exp2(level_exponent(ci, li))
            x = (level_operand(ci, li) * e).astype(jnp.bfloat16)
            attn[ci] = attn[ci] + mk_ref[li] * _dot(x, x, _NT)

    o_intra, qd, decay, upd = [], [], [], []
    for ci in range(n):
        b = cum[ci]
        b_last = b[c - 1:c, :]
        v = v_ref[rs[ci], :]
        kd = (kf[ci] * jnp.exp2(b_last - b)).astype(jnp.bfloat16)
        qd.append((qf[ci] * jnp.exp2(b)).astype(jnp.bfloat16))
        decay.append(jnp.exp2(b_last))
        upd.append(_dot(v, kd, _TN))
        o_intra.append(_dot(attn[ci].astype(jnp.bfloat16), v))

    st = st_scr[...]
    for ci in range(n):
        o = (o_intra[ci] + _dot(qd[ci], st.astype(jnp.bfloat16), _NT)) * scale
        st = st * decay[ci] + upd[ci]
        on = o * _rms_scale(o) * gain
        t = 0.5 * g_ref[rs[ci], :].astype(jnp.float32)
        o_ref[rs[ci], :] = (on * (t + t * jnp.tanh(t))).astype(o_ref.dtype)
    st_scr[...] = st


def _gla_core(qkvg, z, w_gate, b_gate, head_gain, batch, seq):
    t = qkvg.shape[0]
    nblk = seq // GLA_ROWS
    lm, mk = _gla_constants()
    lm = jnp.asarray(np.concatenate([lm, lm], axis=1), jnp.bfloat16)
    mk = jnp.asarray(mk, jnp.float32)
    kq = (GLA_HEADS * GLA_DK) // GLA_DK
    kv = (2 * GLA_HEADS * GLA_DK) // GLA_DV
    kg = kv + GLA_HEADS
    row = lambda b, h, s: b * nblk + s
    return pl.pallas_call(
        _gla_core_kernel,
        grid=(batch, GLA_HEADS, nblk),
        in_specs=[
            pl.BlockSpec((GLA_ROWS, GLA_DK), lambda b, h, s: (row(b, h, s), h)),
            pl.BlockSpec((GLA_ROWS, GLA_DK), lambda b, h, s: (row(b, h, s), kq + h)),
            pl.BlockSpec((GLA_ROWS, GLA_DV), lambda b, h, s: (row(b, h, s), kv + h)),
            pl.BlockSpec((GLA_ROWS, GLA_DV), lambda b, h, s: (row(b, h, s), kg + h)),
            pl.BlockSpec((GLA_ROWS, LANES), lambda b, h, s: (row(b, h, s), 0)),
            pl.BlockSpec((LANES, GLA_DK), lambda b, h, s: (0, h)),
            pl.BlockSpec((1, GLA_DK), lambda b, h, s: (0, h)),
            pl.BlockSpec((1, GLA_DV), lambda b, h, s: (0, 0)),
            pl.BlockSpec(lm.shape, lambda b, h, s: (0, 0)),
            pl.BlockSpec(mk.shape, lambda b, h, s: (0, 0, 0)),
        ],
        out_specs=pl.BlockSpec((GLA_ROWS, GLA_DV), lambda b, h, s: (row(b, h, s), h)),
        out_shape=jax.ShapeDtypeStruct((t, GLA_HEADS * GLA_DV), jnp.bfloat16),
        scratch_shapes=[pltpu.VMEM((GLA_DV, GLA_DK), jnp.float32)],
        compiler_params=_params("parallel", "parallel", "arbitrary"),
        name="gla_core",
    )(qkvg, qkvg, qkvg, qkvg, z, w_gate, b_gate, head_gain, lm, mk)


def _outproj_mlp_kernel(a_ref, h_ref, wo_ref, g_ref, w1_ref, w2_ref, o_ref):
    bf = jnp.bfloat16
    h1 = h_ref[...] + _dot(a_ref[...], wo_ref[...])
    xn = (h1 * _rms_scale(h1) * g_ref[...]).astype(bf)
    acc = h1
    d = h1.shape[1]
    for c0 in range(0, w1_ref.shape[1], d):
        hid = jnp.maximum(_dot(xn, w1_ref[:, c0:c0 + d]), 0.0)
        acc = acc + _dot((hid * hid).astype(bf), w2_ref[c0:c0 + d, :])
    o_ref[...] = acc


def _outproj_mlp(a, h, w_out, gain, w1, w2):
    t, d = h.shape
    tile = pl.BlockSpec((ROW_TILE, d), lambda i: (i, 0))
    return pl.pallas_call(
        _outproj_mlp_kernel,
        grid=(t // ROW_TILE,),
        in_specs=[tile, tile, _resident(w_out.shape), _resident((1, d)),
                  _resident(w1.shape), _resident(w2.shape)],
        out_specs=tile,
        out_shape=jax.ShapeDtypeStruct((t, d), jnp.float32),
        compiler_params=_params("parallel"),
        name="outproj_mlp",
    )(a, h, w_out, gain, w1, w2)


def _segment_norm(y, seg_ones, gain):
    ss = _dot((y * y).astype(jnp.bfloat16), seg_ones)
    return y * lax.rsqrt(ss * (1.0 / DIFF_DH) + NORM_EPS) * gain


def _diff_proj_kernel(h_ref, gkv_ref, gq_ref, wkv_ref, wq_ref, kg_ref, qg_ref, so_ref,
                      *rest, n_jobs):
    k_ref, v_ref, q_ref = rest[n_jobs:n_jobs + 3]
    _round_weights(n_jobs, rest[:n_jobs] + rest[n_jobs + 3:])
    nk = k_ref.shape[1]
    seg_ones = so_ref[...]
    bf = jnp.bfloat16
    qscale = DIFF_DH ** -0.5 * math.log2(math.e)
    x = h_ref[...]
    xhat = x * _rms_scale(x)
    xkv = (xhat * gkv_ref[...]).astype(bf)
    xq = (xhat * gq_ref[...]).astype(bf)
    def norm_store(out_ref, gain_ref, scale, cols, raw):
        y = _segment_norm(raw, seg_ones, gain_ref[...])
        out_ref[:, cols] = (y if scale is None else y * scale).astype(out_ref.dtype)

    pending = None
    for x_in, w_ref, out_ref, gain_ref, scale in ((xkv, wkv_ref, k_ref, kg_ref, None),
                                                 (xq, wq_ref, q_ref, qg_ref, qscale)):
        for c0 in range(0, nk, SEG_TILE):
            cols = slice(c0, c0 + SEG_TILE)
            raw = _dot(x_in, w_ref[:, cols].astype(bf))
            if pending is not None:
                norm_store(*pending)
            pending = (out_ref, gain_ref, scale, cols, raw)
        if out_ref is k_ref:
            v_ref[...] = _dot(xkv, wkv_ref[:, nk:].astype(bf)).astype(v_ref.dtype)
    norm_store(*pending)


def _diff_proj(h, g_kv, g_q, w_kv, w_q, k_gain, q_gain, next_weights):
    t, d = h.shape
    n_steps = t // PROJ_ROW_TILE
    jobs, job_in, job_out, job_shapes = _rounding_job(next_weights, n_steps)
    nk = DIFF_HEADS * 2 * DIFF_DH
    nv = DIFF_HEADS * DIFF_VD
    lane = np.arange(SEG_TILE)
    seg_ones = jnp.asarray(lane[:, None] // DIFF_DH == lane[None, :] // DIFF_DH, jnp.bfloat16)
    tile = lambda n: pl.BlockSpec((PROJ_ROW_TILE, n), lambda i: (i, 0))
    return pl.pallas_call(
        functools.partial(_diff_proj_kernel, n_jobs=len(jobs)),
        grid=(n_steps,),
        in_specs=[tile(d), _resident((1, d)), _resident((1, d)), _resident(w_kv.shape),
                  _layer_resident(w_q, 0), _resident((1, SEG_TILE)), _resident((1, SEG_TILE)),
                  _resident((SEG_TILE, SEG_TILE))] + job_in,
        out_specs=[tile(nk), tile(nv), tile(nk)] + job_out,
        out_shape=[jax.ShapeDtypeStruct((t, nk), jnp.bfloat16),
                   jax.ShapeDtypeStruct((t, nv), jnp.bfloat16),
                   jax.ShapeDtypeStruct((t, nk), jnp.bfloat16)] + job_shapes,
        compiler_params=_params("parallel"),
        name="diff_proj",
    )(h, g_kv, g_q, w_kv, w_q, k_gain, q_gain, seg_ones, *jobs)


def _diff_attn_kernel(q_ref, k_ref, v_ref, lam_ref, hg_ref, o_ref,
                      m_scr, l_scr, acc_scr, s_bufs, p_bufs, *, lambda_init):
    tq = ATT_TQ
    tk = ATT_TK
    rb = ATT_ROW_BLOCK
    lane = lax.broadcasted_iota(jnp.int32, (tq, q_ref.shape[1]), 1)
    col_minus_row = (lax.broadcasted_iota(jnp.int32, (rb, tk), 1)
                     - lax.broadcasted_iota(jnp.int32, (rb, tk), 0))

    def block(j):
        return pl.ds(pl.multiple_of(j * tk, tk), tk)

    def q_tile(qi, carry):
        q_rows = pl.ds(pl.multiple_of(qi * tq, tq), tq)
        m_scr[...] = jnp.full_like(m_scr, NEG_BIG)
        l_scr[...] = jnp.zeros_like(l_scr)
        acc_scr[...] = jnp.zeros_like(acc_scr)

        q = q_ref[q_rows, :]
        qcs = [jnp.where((lane // DIFF_DH) == c, q, jnp.zeros_like(q)) for c in range(2)]

        def step(j, buf, row_lo, diag_row0):
            k = k_ref[block(j), :]
            v = v_ref[block(j), :]
            s_scr = s_bufs.at[buf]
            p_scr = p_bufs.at[buf]
            if diag_row0 is None:
                groups = [(slice(row_lo, tq), tk)]
            else:
                half = ATT_MXU_TILE
                groups = [(slice(diag_row0, diag_row0 + half), half),
                          (slice(diag_row0 + half, diag_row0 + tk), tk)]
                if diag_row0 + tk < tq:
                    groups.append((slice(diag_row0 + tk, tq), tk))
            kext_of_row = {r0: kext for g_rows, kext in groups
                           for r0 in range(g_rows.start, g_rows.stop, rb)}
            for c in range(2):
                for g_rows, kext in groups:
                    s_scr[c, g_rows, 0:kext] = _dot(qcs[c][g_rows], k[0:kext], _NT)
            for c in range(2):
                for r0 in range(row_lo, tq, rb):
                    rows = slice(r0, r0 + rb)
                    masked = diag_row0 is not None and r0 - diag_row0 < tk
                    kmax = min(tk, -(-(r0 - diag_row0 + rb) // LANES) * LANES) if masked else tk
                    s = s_scr[c, rows, 0:kmax]
                    if masked:
                        s = jnp.where(col_minus_row[:, 0:kmax] <= r0 - diag_row0, s, NEG_BIG)
                    m_prev = m_scr[c, rows, :]
                    m_new = jnp.maximum(m_prev, jnp.max(s, axis=-1, keepdims=True))
                    alpha = jnp.exp2(m_prev - m_new)
                    p = jnp.exp2(s - jnp.tile(m_new, (1, kmax // LANES)))
                    psum = p[:, 0:LANES]
                    for t0 in range(LANES, kmax, LANES):
                        psum = psum + p[:, t0:t0 + LANES]
                    l_scr[c, rows, :] = alpha * l_scr[c, rows, :] + psum
                    m_scr[c, rows, :] = m_new
                    acc_scr[c, rows, :] = alpha * acc_scr[c, rows, :]
                    p_scr[c, rows, 0:kmax] = p.astype(jnp.bfloat16)
                    kext = kext_of_row[r0]
                    if kmax < kext:
                        p_scr[c, rows, kmax:kext] = jnp.zeros((rb, kext - kmax), jnp.bfloat16)
                for g_rows, kext in groups:
                    acc_scr[c, g_rows, :] += _dot(p_scr[c, g_rows, 0:kext], v[0:kext])

        n_sub = tq // tk

        def pair(i, carry):
            step(2 * i, 0, 0, None)
            step(2 * i + 1, 1, 0, None)
            return carry

        lax.fori_loop(0, (n_sub // 2) * qi, pair, 0)
        for t in range(n_sub):
            step(n_sub * qi + t, t % 2, t * tk, t * tk)

        lp = lam_ref[...]
        lam = (jnp.exp(jnp.sum(lp[0:1] * lp[1:2], axis=-1, keepdims=True))
               - jnp.exp(jnp.sum(lp[2:3] * lp[3:4], axis=-1, keepdims=True)) + lambda_init)
        l0 = jnp.sum(l_scr[0], axis=-1, keepdims=True)
        l1 = jnp.sum(l_scr[1], axis=-1, keepdims=True)
        o = acc_scr[0] / l0 - lam * (acc_scr[1] / l1)
        on = o * _rms_scale(o) * hg_ref[...]
        o_ref[q_rows, :] = (on * (1.0 - lambda_init)).astype(o_ref.dtype)
        return carry

    lax.fori_loop(0, q_ref.shape[0] // tq, q_tile, 0)


def _diff_attn(q, k, v, lam_params, head_gain, batch, seq, lambda_init):
    t = q.shape[0]
    assert ATT_TQ % (2 * ATT_TK) == 0
    assert seq % ATT_TQ == 0
    kern = functools.partial(_diff_attn_kernel, lambda_init=lambda_init)
    head_block = lambda width: pl.BlockSpec((seq, width), lambda b, h: (b, h))
    return pl.pallas_call(
        kern,
        grid=(batch, DIFF_HEADS),
        in_specs=[
            head_block(2 * DIFF_DH),
            head_block(2 * DIFF_DH),
            head_block(DIFF_VD),
            pl.BlockSpec(lam_params.shape, lambda b, h: (0, 0)),
            pl.BlockSpec((1, DIFF_VD), lambda b, h: (0, 0)),
        ],
        out_specs=head_block(DIFF_VD),
        out_shape=jax.ShapeDtypeStruct((t, DIFF_HEADS * DIFF_VD), jnp.bfloat16),
        scratch_shapes=[
            pltpu.VMEM((2, ATT_TQ, LANES), jnp.float32),
            pltpu.VMEM((2, ATT_TQ, LANES), jnp.float32),
            pltpu.VMEM((2, ATT_TQ, DIFF_VD), jnp.float32),
            pltpu.VMEM((2, 2, ATT_TQ, ATT_TK), jnp.float32),
            pltpu.VMEM((2, 2, ATT_TQ, ATT_TK), jnp.bfloat16),
        ],
        compiler_params=_params("parallel", "parallel"),
        name="diff_attn",
    )(q, k, v, lam_params, head_gain)


def kernel(x, a_norm, a_w_in, a_w_gate_up, a_b_gate, a_head_norm, a_w_out, kv_norm, w_kv, k_norm,
           b_norm, b_w_q, b_q_norm, b_lambda, b_head_norm, b_w_out, mlp_norm, mlp_w1, mlp_w2):
    batch, seq, d = x.shape
    bf = jnp.bfloat16
    row = lambda p: p.reshape(1, -1)
    h = x.reshape(batch * seq, d)

    n_main = 2 * GLA_HEADS * GLA_DK + 2 * GLA_HEADS * GLA_DV
    w_z = jnp.pad(a_w_in[0, :, n_main:], ((0, 0), (0, LANES - GLA_RANK))).astype(bf)
    w_gate = jnp.pad(a_w_gate_up[0], ((0, LANES - GLA_RANK), (0, 0))).astype(bf)
    qkvg, z, wo_bf, w1_bf, w2_bf = _gla_inproj(
        h, row(a_norm[0]), jnp.swapaxes(a_w_in, 1, 2), w_z, n_main,
        next_weights=((a_w_out, 0), (mlp_w1, 0), (mlp_w2, 0)))
    og = _gla_core(qkvg, z, w_gate, row(a_b_gate[0]), row(a_head_norm[0]), batch, seq)
    h = _outproj_mlp(og, h, wo_bf, row(mlp_norm[0]), w1_bf, w2_bf)

    layer = 1
    lambda_init = 0.8 - 0.6 * math.exp(-0.3 * layer)
    reps = SEG_TILE // DIFF_DH
    kn, v, qn, wo_bf, w1_bf, w2_bf = _diff_proj(
        h, row(kv_norm), row(b_norm[0]), w_kv, b_w_q,
        row(jnp.tile(k_norm, reps)), row(jnp.tile(b_q_norm[0], reps)),
        next_weights=((b_w_out, 0), (mlp_w1, 1), (mlp_w2, 1)))
    oa = _diff_attn(qn, kn, v, b_lambda[0], row(b_head_norm[0]), batch, seq, lambda_init)
    h = _outproj_mlp(oa, h, wo_bf, row(mlp_norm[1]), w1_bf, w2_bf)
    return h.reshape(batch, seq, d)
```

```python
import functools
import math

import numpy as np
import jax
import jax.numpy as jnp
from jax import lax
from jax.experimental import pallas as pl
from jax.experimental.pallas import tpu as pltpu

D_MODEL = 1024
GLA_HEADS = 4
GLA_DK = 128
GLA_DV = 256
GLA_RANK = 16
GLA_TAU = 16.0
GLA_CHUNK = 64
DIFF_HEADS = 8
DIFF_DH = 64
DIFF_VD = 128
MLP_HIDDEN = 4 * D_MODEL
NORM_EPS = 1e-6

LANES = 128
VMEM_LIMIT = 56 * 1024 * 1024
NEG_BIG = -1e30

ROW_TILE = 1024
PROJ_ROW_TILE = 1024
PROJ_COL_CHUNK = 768
GLA_ROWS = 2048
SEG_TILE = 256
ATT_TQ = 1024
ATT_TK = 512
ATT_ROW_BLOCK = 32
GLA_LEVELS = (32, 16, 8, 4, 2, 1)
SUBLANES = 8
_GLA_N_VPU_LEVELS = sum(m >= SUBLANES for m in GLA_LEVELS)

_NT = (((1,), (1,)), ((), ()))
_TN = (((0,), (0,)), ((), ()))


def _dot(a, b, dims=None):
    if dims is None:
        return jnp.dot(a, b, preferred_element_type=jnp.float32)
    return lax.dot_general(a, b, dims, preferred_element_type=jnp.float32)


def _params(*sem):
    return pltpu.CompilerParams(dimension_semantics=sem, vmem_limit_bytes=VMEM_LIMIT)


def _resident(shape):
    nd = len(shape)
    return pl.BlockSpec(shape, lambda *_: (0,) * nd, pipeline_mode=pl.Buffered(1))


def _layer_resident(w, layer):
    return pl.BlockSpec((None,) + w.shape[1:], lambda *_: (layer, 0, 0),
                        pipeline_mode=pl.Buffered(1))


def _rounding_job(stacked_weights, n_steps):
    inputs, in_specs, out_specs, out_shapes = [], [], [], []
    for w, layer in stacked_weights:
        rows, cols = w.shape[1] // n_steps, w.shape[2]
        inputs.append(w)
        in_specs.append(pl.BlockSpec((None, rows, cols), lambda i, layer=layer: (layer, i, 0)))
        out_specs.append(pl.BlockSpec((rows, cols), lambda i: (i, 0)))
        out_shapes.append(jax.ShapeDtypeStruct(w.shape[1:], jnp.bfloat16))
    return inputs, in_specs, out_specs, out_shapes


def _round_weights(n_jobs, refs):
    for src, dst in zip(refs[:n_jobs], refs[n_jobs:]):
        dst[...] = src[...].astype(dst.dtype)


def _rms_scale(x):
    return lax.rsqrt(jnp.mean(x * x, axis=-1, keepdims=True) + NORM_EPS)


def _gla_inproj_kernel(x_ref, g_ref, w_ref, wz_ref, *rest, n_jobs):
    o_ref, z_ref = rest[n_jobs:n_jobs + 2]
    _round_weights(n_jobs, rest[:n_jobs] + rest[n_jobs + 2:])
    x = x_ref[...]
    xn = (x * _rms_scale(x) * g_ref[...]).astype(jnp.bfloat16)
    n_out = o_ref.shape[1]
    step = PROJ_COL_CHUNK
    for n0 in range(0, n_out, step):
        w = w_ref[n0:n0 + step, :].astype(jnp.bfloat16)
        o_ref[:, n0:n0 + step] = _dot(xn, w, _NT).astype(o_ref.dtype)
    z_ref[...] = _dot(xn, wz_ref[...])


def _gla_inproj(x, gain, w_in, w_z, n, next_weights):
    t, d = x.shape
    n_steps = t // PROJ_ROW_TILE
    jobs, job_in, job_out, job_shapes = _rounding_job(next_weights, n_steps)
    return pl.pallas_call(
        functools.partial(_gla_inproj_kernel, n_jobs=len(jobs)),
        grid=(n_steps,),
        in_specs=[
            pl.BlockSpec((PROJ_ROW_TILE, d), lambda i: (i, 0)),
            _resident((1, d)),
            _layer_resident(w_in, 0),
            _resident((d, LANES)),
        ] + job_in,
        out_specs=[
            pl.BlockSpec((PROJ_ROW_TILE, n), lambda i: (i, 0)),
            pl.BlockSpec((PROJ_ROW_TILE, LANES), lambda i: (i, 0)),
        ] + job_out,
        out_shape=[
            jax.ShapeDtypeStruct((t, n), jnp.bfloat16),
            jax.ShapeDtypeStruct((t, LANES), jnp.float32),
        ] + job_shapes,
        compiler_params=_params("parallel"),
        name="gla_inproj",
    )(x, gain, w_in, w_z, *jobs)


def _gla_constants():
    c = GLA_CHUNK
    t = np.arange(c)
    row, col = t[:, None], t[None, :]
    mats = [col <= row]
    masks = []
    for m in GLA_LEVELS:
        blk = 2 * m
        start = (t // blk) * blk
        mid = (start + m)[:, None]
        second = ((t % blk) >= m)[:, None]
        q_side = second & (col > mid) & (col <= row)
        k_side = (~second) & (col > row) & (col <= mid)
        if m < SUBLANES:
            mats.append(q_side | k_side)
        masks.append((start[:, None] == start[None, :]) & second & (~second).T)
    masks.append(np.eye(c, dtype=bool))
    lm = np.concatenate(mats, axis=0).astype(np.float32)
    mk = np.stack(masks, axis=0).astype(np.float32)
    return lm, mk


def _gla_core_kernel(q_ref, k_ref, v_ref, g_ref, z_ref, wg_ref, bg_ref, hg_ref,
                     lm_ref, mk_ref, o_ref, st_scr):
    c = GLA_CHUNK
    rows = q_ref.shape[0]

    @pl.when(pl.program_id(2) == 0)
    def _():
        st_scr[...] = jnp.zeros_like(st_scr)

    logits = _dot(z_ref[...].astype(jnp.bfloat16), wg_ref[...]) + bg_ref[...]
    log_sig = jnp.minimum(logits, 0.0) - jnp.log(1.0 + jnp.exp(-jnp.abs(logits)))
    la = log_sig * (math.log2(math.e) / GLA_TAU)
    la_hi = la.astype(jnp.bfloat16)
    la_lo = (la - la_hi.astype(jnp.float32)).astype(jnp.bfloat16)

    scale = GLA_DK ** -0.5
    gain = hg_ref[...]
    n = rows // c
    rs = [slice(ci * c, (ci + 1) * c) for ci in range(n)]
    n_lvl = len(GLA_LEVELS)

    sums = []
    for ci in range(n):
        la_stack = jnp.concatenate([la_hi[rs[ci]], la_lo[rs[ci]]], axis=0)
        sums.append(_dot(lm_ref[...], la_stack))
    cum = [s[0:c] for s in sums]

    def level_exponent(ci, li):
        m = GLA_LEVELS[li]
        if m < SUBLANES:
            lo = (1 + li - _GLA_N_VPU_LEVELS) * c
            return sums[ci][lo:lo + c]
        b = cum[ci]
        parts = []
        for start in range(0, c, 2 * m):
            mid = b[start + m:start + m + 1, :]
            parts.append(mid - b[start:start + m])
            parts.append(b[start + m:start + 2 * m] - mid)
        return jnp.concatenate(parts, axis=0)

    qf = [q_ref[r, :].astype(jnp.float32) for r in rs]
    kf = [k_ref[r, :].astype(jnp.float32) for r in rs]
    attn = [mk_ref[n_lvl] * _dot(q_ref[r, :], k_ref[r, :], _NT) for r in rs]
    row_in_chunk = lax.broadcasted_iota(jnp.int32, (c, GLA_DK), 0)

    def level_operand(ci, li):
        m = GLA_LEVELS[li]
        if m < SUBLANES:
            return jnp.where(row_in_chunk % (2 * m) >= m, qf[ci], kf[ci])
        parts = []
        for start in range(0, c, 2 * m):
            parts.append(kf[ci][start:start + m])
            parts.append(qf[ci][start + m:start + 2 * m])
        return jnp.concatenate(parts, axis=0)

    for li in range(n_lvl):
        for ci in range(n):
            e = jnp.exp2(level_exponent(ci, li))
            x = (level_operand(ci, li) * e).astype(jnp.bfloat16)
            attn[ci] = attn[ci] + mk_ref[li] * _dot(x, x, _NT)

    o_intra, qd, decay, upd = [], [], [], []
    for ci in range(n):
        b = cum[ci]
        b_last = b[c - 1:c, :]
        v = v_ref[rs[ci], :]
        kd = (kf[ci] * jnp.exp2(b_last - b)).astype(jnp.bfloat16)
        qd.append((qf[ci] * jnp.exp2(b)).astype(jnp.bfloat16))
        decay.append(jnp.exp2(b_last))
        upd.append(_dot(v, kd, _TN))
        o_intra.append(_dot(attn[ci].astype(jnp.bfloat16), v))

    st = st_scr[...]
    for ci in range(n):
        o = (o_intra[ci] + _dot(qd[ci], st.astype(jnp.bfloat16), _NT)) * scale
        st = st * decay[ci] + upd[ci]
        on = o * _rms_scale(o) * gain
        t = 0.5 * g_ref[rs[ci], :].astype(jnp.float32)
        o_ref[rs[ci], :] = (on * (t + t * jnp.tanh(t))).astype(o_ref.dtype)
    st_scr[...] = st


def _gla_core(qkvg, z, w_gate, b_gate, head_gain, batch, seq):
    t = qkvg.shape[0]
    nblk = seq // GLA_ROWS
    lm, mk = _gla_constants()
    lm = jnp.asarray(np.concatenate([lm, lm], axis=1), jnp.bfloat16)
    mk = jnp.asarray(mk, jnp.float32)
    kq = (GLA_HEADS * GLA_DK) // GLA_DK
    kv = (2 * GLA_HEADS * GLA_DK) // GLA_DV
    kg = kv + GLA_HEADS
    row = lambda b, h, s: b * nblk + s
    return pl.pallas_call(
        _gla_core_kernel,
        grid=(batch, GLA_HEADS, nblk),
        in_specs=[
            pl.BlockSpec((GLA_ROWS, GLA_DK), lambda b, h, s: (row(b, h, s), h)),
            pl.BlockSpec((GLA_ROWS, GLA_DK), lambda b, h, s: (row(b, h, s), kq + h)),
            pl.BlockSpec((GLA_ROWS, GLA_DV), lambda b, h, s: (row(b, h, s), kv + h)),
            pl.BlockSpec((GLA_ROWS, GLA_DV), lambda b, h, s: (row(b, h, s), kg + h)),
            pl.BlockSpec((GLA_ROWS, LANES), lambda b, h, s: (row(b, h, s), 0)),
            pl.BlockSpec((LANES, GLA_DK), lambda b, h, s: (0, h)),
            pl.BlockSpec((1, GLA_DK), lambda b, h, s: (0, h)),
            pl.BlockSpec((1, GLA_DV), lambda b, h, s: (0, 0)),
            pl.BlockSpec(lm.shape, lambda b, h, s: (0, 0)),
            pl.BlockSpec(mk.shape, lambda b, h, s: (0, 0, 0)),
        ],
        out_specs=pl.BlockSpec((GLA_ROWS, GLA_DV), lambda b, h, s: (row(b, h, s), h)),
        out_shape=jax.ShapeDtypeStruct((t, GLA_HEADS * GLA_DV), jnp.bfloat16),
        scratch_shapes=[pltpu.VMEM((GLA_DV, GLA_DK), jnp.float32)],
        compiler_params=_params("parallel", "parallel", "arbitrary"),
        name="gla_core",
    )(qkvg, qkvg, qkvg, qkvg, z, w_gate, b_gate, head_gain, lm, mk)


def _outproj_mlp_kernel(a_ref, h_ref, wo_ref, g_ref, w1_ref, w2_ref, o_ref):
    bf = jnp.bfloat16
    h1 = h_ref[...] + _dot(a_ref[...], wo_ref[...])
    xn = (h1 * _rms_scale(h1) * g_ref[...]).astype(bf)
    acc = h1
    d = h1.shape[1]
    for c0 in range(0, w1_ref.shape[1], d):
        hid = jnp.maximum(_dot(xn, w1_ref[:, c0:c0 + d]), 0.0)
        acc = acc + _dot((hid * hid).astype(bf), w2_ref[c0:c0 + d, :])
    o_ref[...] = acc


def _outproj_mlp(a, h, w_out, gain, w1, w2):
    t, d = h.shape
    tile = pl.BlockSpec((ROW_TILE, d), lambda i: (i, 0))
    return pl.pallas_call(
        _outproj_mlp_kernel,
        grid=(t // ROW_TILE,),
        in_specs=[tile, tile, _resident(w_out.shape), _resident((1, d)),
                  _resident(w1.shape), _resident(w2.shape)],
        out_specs=tile,
        out_shape=jax.ShapeDtypeStruct((t, d), jnp.float32),
        compiler_params=_params("parallel"),
        name="outproj_mlp",
    )(a, h, w_out, gain, w1, w2)


def _segment_norm(y, seg_ones, gain):
    ss = _dot((y * y).astype(jnp.bfloat16), seg_ones)
    return y * lax.rsqrt(ss * (1.0 / DIFF_DH) + NORM_EPS) * gain


def _diff_proj_kernel(h_ref, gkv_ref, gq_ref, wkv_ref, wq_ref, kg_ref, qg_ref, so_ref,
                      *rest, n_jobs):
    k_ref, v_ref, q_ref = rest[n_jobs:n_jobs + 3]
    _round_weights(n_jobs, rest[:n_jobs] + rest[n_jobs + 3:])
    nk = k_ref.shape[1]
    seg_ones = so_ref[...]
    bf = jnp.bfloat16
    qscale = DIFF_DH ** -0.5 * math.log2(math.e)
    x = h_ref[...]
    xhat = x * _rms_scale(x)
    xkv = (xhat * gkv_ref[...]).astype(bf)
    xq = (xhat * gq_ref[...]).astype(bf)
    def norm_store(out_ref, gain_ref, scale, cols, raw):
        y = _segment_norm(raw, seg_ones, gain_ref[...])
        out_ref[:, cols] = (y if scale is None else y * scale).astype(out_ref.dtype)

    pending = None
    for x_in, w_ref, out_ref, gain_ref, scale in ((xkv, wkv_ref, k_ref, kg_ref, None),
                                                 (xq, wq_ref, q_ref, qg_ref, qscale)):
        for c0 in range(0, nk, SEG_TILE):
            cols = slice(c0, c0 + SEG_TILE)
            raw = _dot(x_in, w_ref[:, cols].astype(bf))
            if pending is not None:
                norm_store(*pending)
            pending = (out_ref, gain_ref, scale, cols, raw)
        if out_ref is k_ref:
            v_ref[...] = _dot(xkv, wkv_ref[:, nk:].astype(bf)).astype(v_ref.dtype)
    norm_store(*pending)


def _diff_proj(h, g_kv, g_q, w_kv, w_q, k_gain, q_gain, next_weights):
    t, d = h.shape
    n_steps = t // PROJ_ROW_TILE
    jobs, job_in, job_out, job_shapes = _rounding_job(next_weights, n_steps)
    nk = DIFF_HEADS * 2 * DIFF_DH
    nv = DIFF_HEADS * DIFF_VD
    lane = np.arange(SEG_TILE)
    seg_ones = jnp.asarray(lane[:, None] // DIFF_DH == lane[None, :] // DIFF_DH, jnp.bfloat16)
    tile = lambda n: pl.BlockSpec((PROJ_ROW_TILE, n), lambda i: (i, 0))
    return pl.pallas_call(
        functools.partial(_diff_proj_kernel, n_jobs=len(jobs)),
        grid=(n_steps,),
        in_specs=[tile(d), _resident((1, d)), _resident((1, d)), _resident(w_kv.shape),
                  _layer_resident(w_q, 0), _resident((1, SEG_TILE)), _resident((1, SEG_TILE)),
                  _resident((SEG_TILE, SEG_TILE))] + job_in,
        out_specs=[tile(nk), tile(nv), tile(nk)] + job_out,
        out_shape=[jax.ShapeDtypeStruct((t, nk), jnp.bfloat16),
                   jax.ShapeDtypeStruct((t, nv), jnp.bfloat16),
                   jax.ShapeDtypeStruct((t, nk), jnp.bfloat16)] + job_shapes,
        compiler_params=_params("parallel"),
        name="diff_proj",
    )(h, g_kv, g_q, w_kv, w_q, k_gain, q_gain, seg_ones, *jobs)


def _diff_attn_kernel(q_ref, k_ref, v_ref, lam_ref, hg_ref, o_ref,
                      m_scr, l_scr, acc_scr, s_bufs, p_bufs, *, lambda_init):
    tq = ATT_TQ
    tk = ATT_TK
    rb = ATT_ROW_BLOCK
    lane = lax.broadcasted_iota(jnp.int32, (tq, q_ref.shape[1]), 1)
    col_minus_row = (lax.broadcasted_iota(jnp.int32, (rb, tk), 1)
                     - lax.broadcasted_iota(jnp.int32, (rb, tk), 0))

    def block(j):
        return pl.ds(pl.multiple_of(j * tk, tk), tk)

    def q_tile(qi, carry):
        q_rows = pl.ds(pl.multiple_of(qi * tq, tq), tq)
        m_scr[...] = jnp.full_like(m_scr, NEG_BIG)
        l_scr[...] = jnp.zeros_like(l_scr)
        acc_scr[...] = jnp.zeros_like(acc_scr)

        q = q_ref[q_rows, :]
        qcs = [jnp.where((lane // DIFF_DH) == c, q, jnp.zeros_like(q)) for c in range(2)]

        def step(j, buf, row_lo, diag_row0, comps):
            k = k_ref[block(j), :]
            v = v_ref[block(j), :]
            s_scr = s_bufs.at[buf]
            p_scr = p_bufs.at[buf]
            all_rows = slice(row_lo, tq)
            for c in comps:
                s_scr[c, all_rows, :] = _dot(qcs[c][all_rows], k, _NT)
            for c in comps:
                for r0 in range(row_lo, tq, rb):
                    rows = slice(r0, r0 + rb)
                    masked = diag_row0 is not None and r0 - diag_row0 < tk
                    kmax = min(tk, -(-(r0 - diag_row0 + rb) // LANES) * LANES) if masked else tk
                    s = s_scr[c, rows, 0:kmax]
                    if masked:
                        s = jnp.where(col_minus_row[:, 0:kmax] <= r0 - diag_row0, s, NEG_BIG)
                    m_prev = m_scr[c, rows, :]
                    m_new = jnp.maximum(m_prev, jnp.max(s, axis=-1, keepdims=True))
                    alpha = jnp.exp2(m_prev - m_new)
                    p = jnp.exp2(s - jnp.tile(m_new, (1, kmax // LANES)))
                    psum = p[:, 0:LANES]
                    for t0 in range(LANES, kmax, LANES):
                        psum = psum + p[:, t0:t0 + LANES]
                    l_scr[c, rows, :] = alpha * l_scr[c, rows, :] + psum
                    m_scr[c, rows, :] = m_new
                    acc_scr[c, rows, :] = alpha * acc_scr[c, rows, :]
                    p_scr[c, rows, 0:kmax] = p.astype(jnp.bfloat16)
                    if kmax < tk:
                        p_scr[c, rows, kmax:tk] = jnp.zeros((rb, tk - kmax), jnp.bfloat16)
                acc_scr[c, all_rows, :] += _dot(p_scr[c, all_rows, :], v)

        n_sub = tq // tk

        for c in range(2):
            def pair(i, carry, c=c):
                step(2 * i, 0, 0, None, (c,))
                step(2 * i + 1, 1, 0, None, (c,))
                return carry

            lax.fori_loop(0, (n_sub // 2) * qi, pair, 0)
            for t in range(n_sub):
                step(n_sub * qi + t, t % 2, t * tk, t * tk, (c,))

        lp = lam_ref[...]
        lam = (jnp.exp(jnp.sum(lp[0:1] * lp[1:2], axis=-1, keepdims=True))
               - jnp.exp(jnp.sum(lp[2:3] * lp[3:4], axis=-1, keepdims=True)) + lambda_init)
        l0 = jnp.sum(l_scr[0], axis=-1, keepdims=True)
        l1 = jnp.sum(l_scr[1], axis=-1, keepdims=True)
        o = acc_scr[0] / l0 - lam * (acc_scr[1] / l1)
        on = o * _rms_scale(o) * hg_ref[...]
        o_ref[q_rows, :] = (on * (1.0 - lambda_init)).astype(o_ref.dtype)
        return carry

    lax.fori_loop(0, q_ref.shape[0] // tq, q_tile, 0)


def _diff_attn(q, k, v, lam_params, head_gain, batch, seq, lambda_init):
    t = q.shape[0]
    assert ATT_TQ % (2 * ATT_TK) == 0
    assert seq % ATT_TQ == 0
    kern = functools.partial(_diff_attn_kernel, lambda_init=lambda_init)
    head_block = lambda width: pl.BlockSpec((seq, width), lambda b, h: (b, h))
    return pl.pallas_call(
        kern,
        grid=(batch, DIFF_HEADS),
        in_specs=[
            head_block(2 * DIFF_DH),
            head_block(2 * DIFF_DH),
            head_block(DIFF_VD),
            pl.BlockSpec(lam_params.shape, lambda b, h: (0, 0)),
            pl.BlockSpec((1, DIFF_VD), lambda b, h: (0, 0)),
        ],
        out_specs=head_block(DIFF_VD),
        out_shape=jax.ShapeDtypeStruct((t, DIFF_HEADS * DIFF_VD), jnp.bfloat16),
        scratch_shapes=[
            pltpu.VMEM((2, ATT_TQ, LANES), jnp.float32),
            pltpu.VMEM((2, ATT_TQ, LANES), jnp.float32),
            pltpu.VMEM((2, ATT_TQ, DIFF_VD), jnp.float32),
            pltpu.VMEM((2, 2, ATT_TQ, ATT_TK), jnp.float32),
            pltpu.VMEM((2, 2, ATT_TQ, ATT_TK), jnp.bfloat16),
        ],
        compiler_params=_params("parallel", "parallel"),
        name="diff_attn",
    )(q, k, v, lam_params, head_gain)


def kernel(x, a_norm, a_w_in, a_w_gate_up, a_b_gate, a_head_norm, a_w_out, kv_norm, w_kv, k_norm,
           b_norm, b_w_q, b_q_norm, b_lambda, b_head_norm, b_w_out, mlp_norm, mlp_w1, mlp_w2):
    batch, seq, d = x.shape
    bf = jnp.bfloat16
    row = lambda p: p.reshape(1, -1)
    h = x.reshape(batch * seq, d)

    n_main = 2 * GLA_HEADS * GLA_DK + 2 * GLA_HEADS * GLA_DV
    w_z = jnp.pad(a_w_in[0, :, n_main:], ((0, 0), (0, LANES - GLA_RANK))).astype(bf)
    w_gate = jnp.pad(a_w_gate_up[0], ((0, LANES - GLA_RANK), (0, 0))).astype(bf)
    qkvg, z, wo_bf, w1_bf, w2_bf = _gla_inproj(
        h, row(a_norm[0]), jnp.swapaxes(a_w_in, 1, 2), w_z, n_main,
        next_weights=((a_w_out, 0), (mlp_w1, 0), (mlp_w2, 0)))
    og = _gla_core(qkvg, z, w_gate, row(a_b_gate[0]), row(a_head_norm[0]), batch, seq)
    h = _outproj_mlp(og, h, wo_bf, row(mlp_norm[0]), w1_bf, w2_bf)

    layer = 1
    lambda_init = 0.8 - 0.6 * math.exp(-0.3 * layer)
    reps = SEG_TILE // DIFF_DH
    kn, v, qn, wo_bf, w1_bf, w2_bf = _diff_proj(
        h, row(kv_norm), row(b_norm[0]), w_kv, b_w_q,
        row(jnp.tile(k_norm, reps)), row(jnp.tile(b_q_norm[0], reps)),
        next_weights=((b_w_out, 0), (mlp_w1, 1), (mlp_w2, 1)))
    oa = _diff_attn(qn, kn, v, b_lambda[0], row(b_head_norm[0]), batch, seq, lambda_init)
    h = _outproj_mlp(oa, h, wo_bf, row(mlp_norm[1]), w1_bf, w2_bf)
    return h.reshape(batch, seq, d)
```
